```python
import math
import jax
import jax.numpy as jnp
from jax import lax
import numpy as np

D_MODEL = 2048
BATCH = 1
SEQ = 8192
DEPTH = 2

LRU_WIDTH = 1024
LRU_BLOCKS = 8
LRU_BLOCK_W = LRU_WIDTH // LRU_BLOCKS
CONV_WIDTH = 4
LRU_C = 8.0
NSA_HEADS = 8
NSA_KV_HEADS = 2
HEADS_PER_GROUP = NSA_HEADS // NSA_KV_HEADS
HEAD_DIM = 128
NSA_WIDTH = NSA_HEADS * HEAD_DIM
KV_WIDTH = NSA_KV_HEADS * HEAD_DIM
CMP_BLOCK = 32
CMP_STRIDE = 16
SEL_BLOCK = 64
N_SEL = 16
WINDOW = 512
Q_BLOCK = 128
N_BUCKETS = 32
MAX_DISTANCE = 128
N_EXPERTS = 32
TOP_K = 4
D_FF = 2048
SWIGLU_LIMIT = 7.0
SWIGLU_ALPHA = 1.702
MOE_BLOCK = 128
MIX_WIDTH = LRU_WIDTH + NSA_WIDTH
IN_WIDTH = 2 * LRU_WIDTH + NSA_WIDTH + 6 * KV_WIDTH + 3 * NSA_HEADS
DN_ALPHA = (2 * DEPTH) ** 0.25
DN_BETA = (8 * DEPTH) ** -0.25
NORM_EPS = 1e-5
NEG_INF = -1e30
FORCED_SCORE = 1e9

kernel_name = 'hybrid_rglru_nsa_moe_deepnorm'


def layer_norm(x, g, b):
    xf = x.astype(jnp.float32)
    xc = xf - jnp.mean(xf, axis=-1, keepdims=True)
    var = jnp.mean(xc * xc, axis=-1, keepdims=True)
    y = xc * lax.rsqrt(var + NORM_EPS) * g.astype(jnp.float32) + b.astype(jnp.float32)
    return y.astype(x.dtype)


def rms_norm(x, g):
    xf = x.astype(jnp.float32)
    y = xf * lax.rsqrt(jnp.mean(xf * xf, axis=-1, keepdims=True) + NORM_EPS) * g.astype(jnp.float32)
    return y.astype(x.dtype)


def masked_softmax(logits, mask):
    z = jnp.where(mask, logits.astype(jnp.float32), NEG_INF)
    p = jax.nn.softmax(z, axis=-1)
    return jnp.where(mask, p, 0.0)


def t5_bucket(dist):
    max_exact = N_BUCKETS // 2
    d = jnp.maximum(dist.astype(jnp.float32), 1.0)
    large = max_exact + (jnp.log(d / max_exact) / math.log(MAX_DISTANCE / max_exact)
                         * (N_BUCKETS - max_exact)).astype(jnp.int32)
    large = jnp.minimum(large, N_BUCKETS - 1)
    return jnp.where(dist < max_exact, dist, large)


def rg_lru_group(xr, xg, conv_w, conv_b, wa, ba, wi, bi, lam):
    B, S, W = xr.shape
    xp = jnp.pad(xr, ((0, 0), (CONV_WIDTH - 1, 0), (0, 0)))
    xc = conv_b + xp[:, 0:S] * conv_w[0]
    for k in range(1, CONV_WIDTH):
        xc = xc + xp[:, k:k + S] * conv_w[k]
    xb = xc.reshape(B, S, LRU_BLOCKS, LRU_BLOCK_W)
    r = jax.nn.sigmoid(jnp.einsum('bsnc,ncd->bsnd', xb, wa).reshape(B, S, W) + ba)
    i = jax.nn.sigmoid(jnp.einsum('bsnc,ncd->bsnd', xb, wi).reshape(B, S, W) + bi)
    log_a = -LRU_C * r.astype(jnp.float32) * jax.nn.softplus(-lam.astype(jnp.float32))
    a = jnp.exp(log_a)
    u = jnp.sqrt(-jnp.expm1(2.0 * log_a)) * (i * xc).astype(jnp.float32)

    def combine(left, right):
        a_l, u_l = left
        a_r, u_r = right
        return a_l * a_r, a_r * u_l + u_r

    _, h = lax.associative_scan(combine, (a, u), axis=1)
    return h.astype(xr.dtype) * jax.nn.gelu(xg)


def compress_kv(kv, pos, w1, w2):
    B, S, G, HD = kv.shape
    n_cmp = (S - CMP_BLOCK) // CMP_STRIDE + 1
    idx = jnp.arange(n_cmp)[:, None] * CMP_STRIDE + jnp.arange(CMP_BLOCK)[None, :]
    blk = kv[:, idx] + pos[None, None, :, None, :]
    blk = blk.transpose(0, 1, 3, 2, 4).reshape(B, n_cmp, G, CMP_BLOCK * HD)
    return jax.nn.gelu(blk @ w1) @ w2


def nsa_group(q, k_cmp, v_cmp, k_slc, v_slc, k_win, v_win, gate_logits,
              cmp_pos, cmp_w1, cmp_w2, rel_bias):
    B, S, _ = q.shape
    G, HPG, HD = NSA_KV_HEADS, HEADS_PER_GROUP, HEAD_DIM
    q = q.reshape(B, S, G, HPG, HD)
    kc = compress_kv(k_cmp.reshape(B, S, G, HD), cmp_pos[0], cmp_w1[0], cmp_w2[0])
    vc = compress_kv(v_cmp.reshape(B, S, G, HD), cmp_pos[1], cmp_w1[1], cmp_w2[1])
    n_cmp = kc.shape[1]
    n_blk = S // SEL_BLOCK
    n_sel = min(N_SEL, n_blk)
    c_idx = jnp.arange(n_cmp)
    c_start = c_idx * CMP_STRIDE
    c_end = c_start + CMP_BLOCK - 1
    c_mid = c_start + CMP_BLOCK // 2
    j_idx = jnp.arange(n_blk)
    overlap = ((c_start[:, None] <= (j_idx * SEL_BLOCK + SEL_BLOCK - 1)[None, :])
               & (c_end[:, None] >= (j_idx * SEL_BLOCK)[None, :])).astype(jnp.float32)
    ks_blk = k_slc.reshape(B, n_blk, SEL_BLOCK, G, HD).transpose(0, 3, 1, 2, 4)
    vs_blk = v_slc.reshape(B, n_blk, SEL_BLOCK, G, HD).transpose(0, 3, 1, 2, 4)
    kw_pad = jnp.pad(k_win.reshape(B, S, G, HD), ((0, 0), (WINDOW, 0), (0, 0), (0, 0)))
    vw_pad = jnp.pad(v_win.reshape(B, S, G, HD), ((0, 0), (WINDOW, 0), (0, 0), (0, 0)))
    rb_group = rel_bias.reshape(N_BUCKETS, G, HPG).transpose(1, 0, 2)
    gates = jax.nn.sigmoid(gate_logits.astype(jnp.float32)).reshape(B, S, G, HPG, 3)
    n_qb = S // Q_BLOCK
    q_chunks = q.reshape(B, n_qb, Q_BLOCK, G, HPG, HD).swapaxes(0, 1)
    g_chunks = gates.reshape(B, n_qb, Q_BLOCK, G, HPG, 3).swapaxes(0, 1)
    t0s = jnp.arange(n_qb) * Q_BLOCK
    scale = HEAD_DIM ** -0.5

    def head_bias(dist):
        return rel_bias[t5_bucket(dist)].reshape(*dist.shape, G, HPG).transpose(2, 3, 0, 1)

    def gather_blocks(blocks, sel):
        out = jax.vmap(jax.vmap(lambda kb, ix: kb[ix]))(blocks, sel)
        return out.reshape(B, G, Q_BLOCK, n_sel * SEL_BLOCK, HD)

    def chunk(args):
        qc, gc, t0 = args
        t = t0 + jnp.arange(Q_BLOCK)
        lc = (jnp.einsum('bqghd,bcgd->bghqc', qc, kc).astype(jnp.float32) * scale
              + head_bias(jnp.maximum(t[:, None] - c_mid[None, :], 0)))
        pc = masked_softmax(lc, c_end[None, :] <= t[:, None])
        o_c = jnp.einsum('bghqc,bcgd->bqghd', pc.astype(vc.dtype), vc)
        imp = jnp.einsum('bghqc,cn->bgqn', pc, overlap)
        cur = t // SEL_BLOCK
        blk_ok = j_idx[None, :] <= cur[:, None]
        forced = ((j_idx[None, :] == 0) | (j_idx[None, :] == cur[:, None])
                  | (j_idx[None, :] == cur[:, None] - 1))
        score = jnp.where(blk_ok, jnp.where(forced, FORCED_SCORE, imp), NEG_INF)
        top_s, sel = lax.top_k(score, n_sel)
        sel_ok = top_s > 0.5 * NEG_INF
        ksel = gather_blocks(ks_blk, sel)
        vsel = gather_blocks(vs_blk, sel)
        kpos = (sel[..., None] * SEL_BLOCK + jnp.arange(SEL_BLOCK)).reshape(B, G, Q_BLOCK, n_sel * SEL_BLOCK)
        mask_s = (kpos <= t[:, None]) & jnp.repeat(sel_ok, SEL_BLOCK, axis=-1)
        bucket_s = t5_bucket(jnp.maximum(t[:, None] - kpos, 0))
        bias_s = jax.vmap(lambda tab, bk: tab[bk], in_axes=(0, 1), out_axes=1)(rb_group, bucket_s)
        bias_s = bias_s.transpose(0, 1, 4, 2, 3)
        ls = jnp.einsum('bqghd,bgqnd->bghqn', qc, ksel).astype(jnp.float32) * scale + bias_s
        ps = masked_softmax(ls, mask_s[:, :, None])
        o_s = jnp.einsum('bghqn,bgqnd->bqghd', ps.astype(vsel.dtype), vsel)
        kwc = lax.dynamic_slice_in_dim(kw_pad, t0, WINDOW + Q_BLOCK, axis=1)
        vwc = lax.dynamic_slice_in_dim(vw_pad, t0, WINDOW + Q_BLOCK, axis=1)
        s = t0 - WINDOW + jnp.arange(WINDOW + Q_BLOCK)
        dist_w = t[:, None] - s[None, :]
        mask_w = (dist_w >= 0) & (dist_w < WINDOW) & (s[None, :] >= 0)
        lw = (jnp.einsum('bqghd,bkgd->bghqk', qc, kwc).astype(jnp.float32) * scale
              + head_bias(jnp.maximum(dist_w, 0)))
        pw = masked_softmax(lw, mask_w)
        o_w = jnp.einsum('bghqk,bkgd->bqghd', pw.astype(vwc.dtype), vwc)
        o = gc[..., 0:1] * o_c + gc[..., 1:2] * o_s + gc[..., 2:3] * o_w
        return o.reshape(B, Q_BLOCK, NSA_WIDTH).astype(qc.dtype)

    out = lax.map(chunk, (q_chunks, g_chunks, t0s))
    return out.swapaxes(0, 1).reshape(B, S, NSA_WIDTH)


def moe_ffn(x, w_router, b_router, w_gu, b_gu, w_dn, b_dn):
    B, S, D = x.shape
    N = B * S
    xf = x.reshape(N, D)
    logits = (xf @ w_router + b_router).astype(jnp.float32)
    top_v, top_e = lax.top_k(logits, TOP_K)
    gates = jax.nn.softmax(top_v, axis=-1)
    n_asg = N * TOP_K
    e_flat = top_e.reshape(-1)
    tok = jnp.arange(n_asg) // TOP_K
    g_flat = gates.reshape(-1)
    order = jnp.argsort(e_flat)
    e_s, tok_s, g_s = e_flat[order], tok[order], g_flat[order]
    counts = jnp.bincount(e_flat, length=N_EXPERTS)
    padded = (counts + MOE_BLOCK - 1) // MOE_BLOCK * MOE_BLOCK
    p_end = jnp.cumsum(padded)
    p_start = p_end - padded
    g_start = jnp.cumsum(counts) - counts
    dest = p_start[e_s] + jnp.arange(n_asg) - g_start[e_s]
    n_rows = -(-n_asg // MOE_BLOCK) * MOE_BLOCK + N_EXPERTS * MOE_BLOCK
    n_blocks = n_rows // MOE_BLOCK
    buf = jnp.zeros((n_rows, D), x.dtype).at[dest].set(xf[tok_s])
    blk_e = jnp.minimum(jnp.searchsorted(p_end, jnp.arange(n_blocks) * MOE_BLOCK, side='right'),
                        N_EXPERTS - 1)

    def expert_block(args):
        xb, e = args
        gu = xb @ w_gu[e] + b_gu[e]
        gate = jnp.minimum(gu[:, 0::2], SWIGLU_LIMIT)
        up = jnp.clip(gu[:, 1::2], -SWIGLU_LIMIT, SWIGLU_LIMIT)
        h = (up + 1.0) * (gate * jax.nn.sigmoid(SWIGLU_ALPHA * gate))
        return h @ w_dn[e] + b_dn[e]

    ybuf = lax.map(expert_block, (buf.reshape(n_blocks, MOE_BLOCK, D), blk_e)).reshape(n_rows, D)
    y = jax.ops.segment_sum(ybuf[dest] * g_s[:, None].astype(ybuf.dtype), tok_s, num_segments=N)
    return y.reshape(B, S, D).astype(x.dtype)


def setup_inputs(seed: int = 0) -> dict:
    key = jax.random.key(seed)
    ks = jax.random.split(key, 32)
    L, D, E, F = DEPTH, D_MODEL, N_EXPERTS, D_FF

    def nrm(k, shape, scale):
        return jax.random.normal(k, shape, jnp.float32) * scale

    a0 = jax.random.uniform(ks[7], (L, LRU_WIDTH), jnp.float32, 0.9, 0.999)
    return {
        'x': nrm(ks[0], (BATCH, SEQ, D), 1.0),
        'w_in': nrm(ks[1], (L, D, IN_WIDTH), D ** -0.5),
        'conv_w': nrm(ks[2], (L, CONV_WIDTH, LRU_WIDTH), CONV_WIDTH ** -0.5),
        'conv_b': nrm(ks[3], (L, LRU_WIDTH), 0.01),
        'lru_wa': nrm(ks[4], (L, LRU_BLOCKS, LRU_BLOCK_W, LRU_BLOCK_W), LRU_BLOCK_W ** -0.5),
        'lru_ba': nrm(ks[5], (L, LRU_WIDTH), 0.01),
        'lru_wi': nrm(ks[6], (L, LRU_BLOCKS, LRU_BLOCK_W, LRU_BLOCK_W), LRU_BLOCK_W ** -0.5),
        'lru_bi': nrm(ks[8], (L, LRU_WIDTH), 0.01),
        'lru_lambda': jnp.log(a0) - jnp.log1p(-a0),
        'cmp_pos': nrm(ks[9], (L, 2, CMP_BLOCK, HEAD_DIM), 0.02),
        'cmp_w1': nrm(ks[10], (L, 2, CMP_BLOCK * HEAD_DIM, HEAD_DIM), (CMP_BLOCK * HEAD_DIM) ** -0.5),
        'cmp_w2': nrm(ks[11], (L, 2, HEAD_DIM, HEAD_DIM), HEAD_DIM ** -0.5),
        'rel_bias': nrm(ks[12], (N_BUCKETS, NSA_HEADS), 0.1),
        'g_lru': 1.0 + nrm(ks[13], (L, LRU_WIDTH), 0.02),
        'g_nsa': 1.0 + nrm(ks[14], (L, NSA_WIDTH), 0.02),
        'w_out': nrm(ks[15], (L, MIX_WIDTH, D), MIX_WIDTH ** -0.5 * DN_BETA),
        'ln1_g': 1.0 + nrm(ks[16], (L, D), 0.02),
        'ln1_b': nrm(ks[17], (L, D), 0.02),
        'w_router': nrm(ks[18], (L, D, E), D ** -0.5),
        'b_router': nrm(ks[19], (L, E), 0.01),
        'w_gate_up': nrm(ks[20], (L, E, D, 2 * F), D ** -0.5),
        'b_gate_up': nrm(ks[21], (L, E, 2 * F), 0.01),
        'w_down': nrm(ks[22], (L, E, F, D), F ** -0.5 * DN_BETA),
        'b_down': nrm(ks[23], (L, E, D), 0.01),
        'ln2_g': 1.0 + nrm(ks[24], (L, D), 0.02),
        'ln2_b': nrm(ks[25], (L, D), 0.02),
    }


def reference(x, w_in, conv_w, conv_b, lru_wa, lru_ba, lru_wi, lru_bi, lru_lambda,
              cmp_pos, cmp_w1, cmp_w2, rel_bias, g_lru, g_nsa, w_out, ln1_g, ln1_b,
              w_router, b_router, w_gate_up, b_gate_up, w_down, b_down, ln2_g, ln2_b):
    sizes = [LRU_WIDTH, LRU_WIDTH, NSA_WIDTH] + [KV_WIDTH] * 6 + [3 * NSA_HEADS]
    split_at = [int(v) for v in np.cumsum(sizes)[:-1]]
    B, S, _ = x.shape
    for l in range(DEPTH):
        u = x @ w_in[l]
        (xr, xg, q, k_c, v_c, k_s, v_s, k_w, v_w, gl) = jnp.split(u, split_at, axis=-1)
        y_lru = rg_lru_group(xr, xg, conv_w[l], conv_b[l], lru_wa[l], lru_ba[l],
                             lru_wi[l], lru_bi[l], lru_lambda[l])
        y_nsa = nsa_group(q, k_c, v_c, k_s, v_s, k_w, v_w, gl.reshape(B, S, NSA_HEADS, 3),
                          cmp_pos[l], cmp_w1[l], cmp_w2[l], rel_bias)
        mixed = jnp.concatenate([rms_norm(y_lru, g_lru[l]), rms_norm(y_nsa, g_nsa[l])], axis=-1)
        x = layer_norm(DN_ALPHA * x + mixed @ w_out[l], ln1_g[l], ln1_b[l])
        y_moe = moe_ffn(x, w_router[l], b_router[l], w_gate_up[l], b_gate_up[l],
                        w_down[l], b_down[l])
        x = layer_norm(DN_ALPHA * x + y_moe, ln2_g[l], ln2_b[l])
    return x
```

```python
import functools
import math

import numpy as np
import jax
import jax.numpy as jnp
from jax import lax
from jax.experimental import pallas as pl
from jax.experimental.pallas import tpu as pltpu

D_MODEL = 2048
DEPTH = 2
LRU_WIDTH = 1024
LRU_BLOCKS = 8
LRU_BLOCK_W = LRU_WIDTH // LRU_BLOCKS
CONV_WIDTH = 4
LRU_C = 8.0
NSA_HEADS = 8
NSA_KV_HEADS = 2
HEADS_PER_GROUP = NSA_HEADS // NSA_KV_HEADS
HEAD_DIM = 128
NSA_WIDTH = NSA_HEADS * HEAD_DIM
KV_WIDTH = NSA_KV_HEADS * HEAD_DIM
CMP_BLOCK = 32
CMP_STRIDE = 16
SEL_BLOCK = 64
N_SEL = 16
WINDOW = 512
Q_BLOCK = 128
N_BUCKETS = 32
MAX_DISTANCE = 128
N_EXPERTS = 32
TOP_K = 4
D_FF = 2048
SWIGLU_LIMIT = 7.0
SWIGLU_ALPHA = 1.702
DN_ALPHA = (2 * DEPTH) ** 0.25
NORM_EPS = 1e-5
NEG_INF = -1e30
FORCED_SCORE = 1e9
ATTN_SCALE = HEAD_DIM ** -0.5

LANES = 128
MIB = 1024 * 1024
MAIN_COLS = 2 * LRU_WIDTH + NSA_WIDTH + 6 * KV_WIDTH
BF16 = jnp.bfloat16
F32 = jnp.float32

MOE_SUB = 256
MOE_J = 6
MOE_FT = 512
MOE_NF = D_FF // MOE_FT


def _bucket_thresholds():
    max_exact = N_BUCKETS // 2
    out = []
    for b in range(1, N_BUCKETS):
        if b <= max_exact:
            out.append(b)
            continue
        d = max_exact
        while True:
            v = math.log(d / max_exact) / math.log(MAX_DISTANCE / max_exact) * (N_BUCKETS - max_exact)
            assert abs(v - round(v)) > 1e-3 or d == max_exact
            if min(max_exact + int(v), N_BUCKETS - 1) >= b:
                break
            d += 1
        out.append(d)
    return tuple(out)


BUCKET_TH = _bucket_thresholds()


def _cparams(semantics, vmem_mib):
    return pltpu.CompilerParams(dimension_semantics=semantics, vmem_limit_bytes=vmem_mib * MIB)


def _nt_dot(a, b):
    return lax.dot_general(a, b, (((1,), (1,)), ((), ())), preferred_element_type=F32)


def _bias_of_dist(rb_ref, head, d):
    b = jnp.full(d.shape, rb_ref[0, head], F32)
    for k in range(1, N_BUCKETS):
        b = jnp.where(d >= BUCKET_TH[k - 1], rb_ref[k, head], b)
    return b


def _mm_kernel(x_ref, w_ref, o_ref, wb_ref):
    @pl.when(pl.program_id(1) == 0)
    def _():
        wb_ref[...] = w_ref[...].astype(BF16)

    o_ref[...] = jnp.dot(x_ref[...], wb_ref[...], preferred_element_type=F32).astype(o_ref.dtype)


def _project(xb, w3, layer, col_blk0, n_blks, tn, out_dtype, tm=512):
    S, D = xb.shape
    return pl.pallas_call(
        _mm_kernel,
        grid=(n_blks, S // tm),
        in_specs=[pl.BlockSpec((tm, D), lambda n, m: (m, 0)),
                  pl.BlockSpec((None, D, tn), lambda n, m: (layer, 0, col_blk0 + n))],
        out_specs=pl.BlockSpec((tm, tn), lambda n, m: (m, n)),
        out_shape=jax.ShapeDtypeStruct((S, n_blks * tn), out_dtype),
        scratch_shapes=[pltpu.VMEM((D, tn), BF16)],
        compiler_params=_cparams(("arbitrary", "arbitrary"), 40),
        name="in_proj",
    )(xb, w3)


def _gelu_tanh(x):
    return 0.5 * x * (1.0 + jnp.tanh(math.sqrt(2.0 / math.pi) * (x + 0.044715 * (x * x * x))))


def _lru_kernel(xr_ref, xg_ref, cw_ref, cb_ref, wa_ref, ba_ref, wi_ref, bi_ref, lam_ref, g_ref,
                o_ref, xbuf, hc):
    i = pl.program_id(0)
    T = xr_ref.shape[0]
    W = xr_ref.shape[1]
    HALO = 8

    @pl.when(i == 0)
    def _():
        xbuf[0:HALO, :] = jnp.zeros((HALO, W), F32)
        hc[...] = jnp.zeros(hc.shape, F32)

    @pl.when(i > 0)
    def _():
        xbuf[0:HALO, :] = xbuf[T:T + HALO, :]

    xbuf[HALO:HALO + T, :] = xr_ref[...]
    cw = cw_ref[...]
    xc = cb_ref[...] + xbuf[HALO - 3:HALO - 3 + T, :] * cw[0:1]
    for k in range(1, CONV_WIDTH):
        xc = xc + xbuf[HALO - 3 + k:HALO - 3 + k + T, :] * cw[k:k + 1]

    xcb = xc.astype(BF16)
    ra, ia = [], []
    for n in range(LRU_BLOCKS):
        blk = xcb[:, n * LRU_BLOCK_W:(n + 1) * LRU_BLOCK_W]
        ra.append(jnp.dot(blk, wa_ref[n].astype(BF16), preferred_element_type=F32))
        ia.append(jnp.dot(blk, wi_ref[n].astype(BF16), preferred_element_type=F32))
    r = jax.nn.sigmoid(jnp.concatenate(ra, axis=1) + ba_ref[...])
    ig = jax.nn.sigmoid(jnp.concatenate(ia, axis=1) + bi_ref[...])

    z = -lam_ref[...]
    softplus = jnp.maximum(z, 0.0) + jnp.log(1.0 + jnp.exp(-jnp.abs(z)))
    log_a = -LRU_C * r * softplus
    a = jnp.exp(log_a)
    u = jnp.sqrt(1.0 - a * a) * (ig * xc)

    row = lax.broadcasted_iota(jnp.int32, (T, W), 0)
    A, U = a, u
    d = 1
    while d < T:
        keep = row >= d
        a_sh = jnp.where(keep, pltpu.roll(A, d, 0), 1.0)
        u_sh = jnp.where(keep, pltpu.roll(U, d, 0), 0.0)
        U = A * u_sh + U
        A = A * a_sh
        d *= 2
    h = U + A * hc[...]
    hc[...] = h[T - 1:T, :]

    y = h * _gelu_tanh(xg_ref[...])
    ms = jnp.mean(y * y, axis=-1, keepdims=True)
    o_ref[...] = (y * lax.rsqrt(ms + NORM_EPS) * g_ref[...]).astype(o_ref.dtype)


def _lru_group(u_f32, conv_w, conv_b, wa, ba, wi, bi, lam, g_lru, T=256):
    S = u_f32.shape[0]
    W = LRU_WIDTH
    row = lambda v: v.reshape(1, W)
    vec = pl.BlockSpec((1, W), lambda i: (0, 0))
    mat = pl.BlockSpec((LRU_BLOCKS, LRU_BLOCK_W, LRU_BLOCK_W), lambda i: (0, 0, 0))
    return pl.pallas_call(
        _lru_kernel,
        grid=(S // T,),
        in_specs=[pl.BlockSpec((T, W), lambda i: (i, 0)),
                  pl.BlockSpec((T, W), lambda i: (i, 1)),
                  pl.BlockSpec((CONV_WIDTH, W), lambda i: (0, 0)),
                  vec, mat, vec, mat, vec, vec, vec],
        out_specs=pl.BlockSpec((T, W), lambda i: (i, 0)),
        out_shape=jax.ShapeDtypeStruct((S, W), BF16),
        scratch_shapes=[pltpu.VMEM((T + 8, W), F32), pltpu.VMEM((1, W), F32)],
        compiler_params=_cparams(("arbitrary",), 48),
        name="rg_lru",
    )(u_f32, u_f32, conv_w, row(conv_b), wa, row(ba), wi, row(bi), row(lam), row(g_lru))


def _compress_kernel(x_ref, pos_ref, w1_ref, w2_ref, o_ref):
    NJ = x_ref.shape[0]
    half = (CMP_BLOCK // 2) * HEAD_DIM
    x = x_ref[...].astype(F32)
    a = (x + pos_ref[0:1, :]).astype(BF16)
    b = (x + pos_ref[1:2, :]).astype(BF16)
    y1 = jnp.dot(a, w1_ref[0:half, :].astype(BF16), preferred_element_type=F32)
    y2 = jnp.dot(b, w1_ref[half:2 * half, :].astype(BF16), preferred_element_type=F32)
    hmid = y1 + pltpu.roll(y2, NJ - 1, 0)
    o_ref[...] = jnp.dot(_gelu_tanh(hmid).astype(BF16), w2_ref[...].astype(BF16),
                         preferred_element_type=F32).astype(o_ref.dtype)


def _compress(kv_rows, pos, w1, w2):
    _, NJ, RW = kv_rows.shape
    return pl.pallas_call(
        _compress_kernel,
        grid=(4,),
        in_specs=[pl.BlockSpec((None, NJ, RW), lambda a: (a, 0, 0)),
                  pl.BlockSpec((None, 2, RW), lambda a: (a // 2, 0, 0)),
                  pl.BlockSpec((None, CMP_BLOCK * HEAD_DIM, HEAD_DIM), lambda a: (a // 2, 0, 0)),
                  pl.BlockSpec((None, HEAD_DIM, HEAD_DIM), lambda a: (a // 2, 0, 0))],
        out_specs=pl.BlockSpec((None, NJ, HEAD_DIM), lambda a: (a, 0, 0)),
        out_shape=jax.ShapeDtypeStruct((4, NJ, HEAD_DIM), BF16),
        compiler_params=_cparams(("arbitrary",), 32),
        name="kv_compress",
    )(kv_rows, pos.reshape(2, 2, RW), w1, w2)


def _cmp_attn_kernel(rb_ref, q_ref, kcv_ref, oc_ref, sel_ref, tchi, tclo):
    qb = pl.program_id(0)
    t0 = qb * Q_BLOCK
    NJ = kcv_ref.shape[1]
    NB = sel_ref.shape[2]
    n_sel = min(N_SEL, NB)
    LOCAL = 2 * Q_BLOCK // CMP_STRIDE

    @pl.when(qb == 0)
    def _():
        i = lax.broadcasted_iota(jnp.int32, (Q_BLOCK, LANES), 0)
        j = lax.broadcasted_iota(jnp.int32, (Q_BLOCK, LANES), 1)
        d = jnp.maximum(i - CMP_STRIDE * j + (Q_BLOCK - CMP_STRIDE), 0)
        for h in range(NSA_HEADS):
            corr = jnp.where(j < LOCAL, _bias_of_dist(rb_ref, h, d) - rb_ref[N_BUCKETS - 1, h], 0.0)
            hi = corr.astype(BF16)
            tchi[h] = hi
            tclo[h] = (corr - hi.astype(F32)).astype(BF16)

    row = lax.broadcasted_iota(jnp.int32, (Q_BLOCK, NJ), 0)
    col = lax.broadcasted_iota(jnp.int32, (Q_BLOCK, NJ), 1)
    mask = CMP_STRIDE * col + (CMP_BLOCK - 1) <= t0 + row
    jj = lax.broadcasted_iota(jnp.int32, (LANES, NJ), 0)
    cc = lax.broadcasted_iota(jnp.int32, (LANES, NJ), 1)
    place = (cc == (Q_BLOCK // CMP_STRIDE) * (qb - 1) + jj).astype(F32).astype(BF16)
    bj = lax.broadcasted_iota(jnp.int32, (NB, NJ), 0)
    bc = lax.broadcasted_iota(jnp.int32, (NB, NJ), 1)
    ratio = SEL_BLOCK // CMP_STRIDE
    overlap_t = ((bc <= ratio * bj + ratio - 1) & (bc >= ratio * bj - 1)).astype(F32).astype(BF16)

    sj = lax.broadcasted_iota(jnp.int32, (NB, Q_BLOCK), 0)
    st = t0 + lax.broadcasted_iota(jnp.int32, (NB, Q_BLOCK), 1)
    cur = st // SEL_BLOCK
    blk_ok = sj <= cur
    forced = (sj == 0) | (sj == cur) | (sj == cur - 1)

    for g in range(NSA_KV_HEADS):
        kc = kcv_ref[g]
        vc = kcv_ref[NSA_KV_HEADS + g]
        psum = jnp.zeros((Q_BLOCK, NJ), F32)
        for h in range(HEADS_PER_GROUP):
            hh = g * HEADS_PER_GROUP + h
            qh = q_ref[:, hh * HEAD_DIM:(hh + 1) * HEAD_DIM]
            lc = _nt_dot(qh, kc) * ATTN_SCALE + rb_ref[N_BUCKETS - 1, hh]
            lc = lc + jnp.dot(tchi[hh], place, preferred_element_type=F32)
            lc = lc + jnp.dot(tclo[hh], place, preferred_element_type=F32)
            z = jnp.where(mask, lc, NEG_INF)
            m = jnp.max(z, axis=-1, keepdims=True)
            e = jnp.where(mask, jnp.exp(z - m), 0.0)
            l = jnp.sum(e, axis=-1, keepdims=True)
            p = e / jnp.where(l > 0.0, l, 1.0)
            oc_ref[:, hh * HEAD_DIM:(hh + 1) * HEAD_DIM] = jnp.dot(
                p.astype(BF16), vc, preferred_element_type=F32)
            psum = psum + p
        p_hi = psum.astype(BF16)
        p_lo = (psum - p_hi.astype(F32)).astype(BF16)
        imp_t = _nt_dot(overlap_t, p_hi) + _nt_dot(overlap_t, p_lo)
        score = jnp.where(blk_ok, jnp.where(forced, FORCED_SCORE, imp_t), NEG_INF)
        chosen = jnp.zeros((NB, Q_BLOCK), jnp.bool_)
        sjf = sj.astype(F32)
        for _ in range(n_sel):
            mx = jnp.max(score, axis=0, keepdims=True)
            first = jnp.min(jnp.where(score == mx, sjf, float(NB)), axis=0, keepdims=True)
            pick = sjf == first
            chosen = chosen | pick
            score = jnp.where(pick, -3e38, score)
        sel_t = jnp.where(chosen & blk_ok, 1.0, 0.0)
        sel_ref[g] = sel_t.T.astype(sel_ref.dtype)


def _cmp_attention(rel_bias, u_bf, kcv, NB):
    S = u_bf.shape[0]
    NJ = kcv.shape[1]
    return pl.pallas_call(
        _cmp_attn_kernel,
        grid=(S // Q_BLOCK,),
        in_specs=[pl.BlockSpec(memory_space=pltpu.SMEM),
                  pl.BlockSpec((Q_BLOCK, NSA_WIDTH), lambda i: (i, 0)),
                  pl.BlockSpec((4, NJ, HEAD_DIM), lambda i: (0, 0, 0))],
        out_specs=[pl.BlockSpec((Q_BLOCK, NSA_WIDTH), lambda i: (i, 0)),
                   pl.BlockSpec((NSA_KV_HEADS, Q_BLOCK, NB), lambda i: (0, i, 0))],
        out_shape=[jax.ShapeDtypeStruct((S, NSA_WIDTH), F32),
                   jax.ShapeDtypeStruct((NSA_KV_HEADS, S, NB), BF16)],
        scratch_shapes=[pltpu.VMEM((NSA_HEADS, Q_BLOCK, LANES), BF16),
                        pltpu.VMEM((NSA_HEADS, Q_BLOCK, LANES), BF16)],
        compiler_params=_cparams(("arbitrary",), 32),
        name="nsa_compressed",
    )(rel_bias, u_bf, kcv)


def _sel_attn_kernel(rb_ref, q_ref, ks_ref, vs_ref, kw_ref, vw_ref, sel_ref, oc_ref, gl_ref, g_ref,
                     o_ref, tb, yacc):
    qb = pl.program_id(0)
    Q = Q_BLOCK
    HPG = HEADS_PER_GROUP
    NB = sel_ref.shape[2]
    n_win = WINDOW // Q

    ii = lax.broadcasted_iota(jnp.int32, (Q, Q), 0)
    jj = lax.broadcasted_iota(jnp.int32, (Q, Q), 1)

    @pl.when(qb == 0)
    def _():
        for rel in range(2):
            d = jnp.maximum(ii - jj + Q * rel, 0)
            for h in range(NSA_HEADS):
                tb[h, rel] = _bias_of_dist(rb_ref, h, d)

    gates = jax.nn.sigmoid(gl_ref[...])
    eb = lax.broadcasted_iota(jnp.int32, (NB, Q), 0)
    ek = lax.broadcasted_iota(jnp.int32, (NB, Q), 1) // SEL_BLOCK
    causal = ii >= jj
    all_true = jnp.ones((Q, Q), jnp.bool_)

    for g in range(NSA_KV_HEADS):
        q4 = jnp.concatenate([q_ref[:, (g * HPG + h) * HEAD_DIM:(g * HPG + h + 1) * HEAD_DIM]
                              for h in range(HPG)], axis=0)
        selg = sel_ref[g]
        far_bias = jnp.concatenate([jnp.full((Q, 1), rb_ref[N_BUCKETS - 1, g * HPG + h], F32)
                                    for h in range(HPG)], axis=0)
        near_bias = [jnp.concatenate([tb[g * HPG + h, rel] for h in range(HPG)], axis=0)
                     for rel in range(2)]
        cols = slice(g * HEAD_DIM, (g + 1) * HEAD_DIM)

        def tile(kt, k_ref, v_ref, mask_q, bias, carry):
            m, l, acc = carry
            k0 = pl.multiple_of(kt * Q, Q)
            kk = k_ref[pl.ds(k0, Q), cols]
            vv = v_ref[pl.ds(k0, Q), cols]
            s = _nt_dot(q4, kk) * ATTN_SCALE + bias
            keep = jnp.where(mask_q, 1.0, 0.0)
            mask = jnp.concatenate([keep] * HPG, axis=0) > 0.5
            z = jnp.where(mask, s, NEG_INF)
            m_new = jnp.maximum(m, jnp.max(z, axis=-1, keepdims=True))
            alpha = jnp.exp(m - m_new)
            e = jnp.where(mask, jnp.exp(z - m_new), 0.0)
            l = alpha * l + jnp.sum(e, axis=-1, keepdims=True)
            acc = alpha * acc + jnp.dot(e.astype(BF16), vv, preferred_element_type=F32)
            return m_new, l, acc

        def block_mask(kt):
            expand = (eb == (Q // SEL_BLOCK) * kt + ek).astype(F32).astype(BF16)
            return jnp.dot(selg, expand, preferred_element_type=F32) > 0.5

        init = (jnp.full((HPG * Q, 1), NEG_INF, F32), jnp.zeros((HPG * Q, 1), F32),
                jnp.zeros((HPG * Q, HEAD_DIM), F32))

        def far_body(kt, carry):
            return tile(kt, ks_ref, vs_ref, block_mask(kt), far_bias, carry)

        carry = lax.fori_loop(0, jnp.maximum(qb - 1, 0), far_body, init)
        kt1 = jnp.maximum(qb - 1, 0)
        carry = tile(kt1, ks_ref, vs_ref, block_mask(kt1) & (qb >= 1), near_bias[1], carry)
        m_s, l_s, acc_s = tile(qb, ks_ref, vs_ref, block_mask(qb) & causal, near_bias[0], carry)
        o_s = acc_s / jnp.where(l_s > 0.0, l_s, 1.0)

        carry = init
        for rel in range(n_win, -1, -1):
            if rel == n_win:
                mq = ii < jj
            elif rel == 0:
                mq = causal
            else:
                mq = all_true
            bias = near_bias[rel] if rel < 2 else far_bias
            carry = tile(jnp.maximum(qb - rel, 0), kw_ref, vw_ref, mq & (qb >= rel), bias, carry)
        m_w, l_w, acc_w = carry
        o_w = acc_w / jnp.where(l_w > 0.0, l_w, 1.0)

        for h in range(HPG):
            hh = g * HPG + h
            hc = slice(hh * HEAD_DIM, (hh + 1) * HEAD_DIM)
            rows = slice(h * Q, (h + 1) * Q)
            yacc[:, hc] = (gates[:, 3 * hh:3 * hh + 1] * oc_ref[:, hc]
                           + gates[:, 3 * hh + 1:3 * hh + 2] * o_s[rows]
                           + gates[:, 3 * hh + 2:3 * hh + 3] * o_w[rows])

    y = yacc[...]
    ms = jnp.mean(y * y, axis=-1, keepdims=True)
    o_ref[...] = (y * lax.rsqrt(ms + NORM_EPS) * g_ref[...]).astype(o_ref.dtype)


def _sel_attention(rel_bias, u_bf, sel, o_c, gate_logits, g_nsa):
    S = u_bf.shape[0]
    NB = sel.shape[2]
    kv_blk0 = NSA_WIDTH // KV_WIDTH + 2

    def kv_spec(j):
        return pl.BlockSpec((S, KV_WIDTH), lambda i: (0, kv_blk0 + j))

    return pl.pallas_call(
        _sel_attn_kernel,
        grid=(S // Q_BLOCK,),
        in_specs=[pl.BlockSpec(memory_space=pltpu.SMEM),
                  pl.BlockSpec((Q_BLOCK, NSA_WIDTH), lambda i: (i, 0)),
                  kv_spec(0), kv_spec(1), kv_spec(2), kv_spec(3),
                  pl.BlockSpec((NSA_KV_HEADS, Q_BLOCK, NB), lambda i: (0, i, 0)),
                  pl.BlockSpec((Q_BLOCK, NSA_WIDTH), lambda i: (i, 0)),
                  pl.BlockSpec((Q_BLOCK, LANES), lambda i: (i, 0)),
                  pl.BlockSpec((1, NSA_WIDTH), lambda i: (0, 0))],
        out_specs=pl.BlockSpec((Q_BLOCK, NSA_WIDTH), lambda i: (i, 0)),
        out_shape=jax.ShapeDtypeStruct((S, NSA_WIDTH), BF16),
        scratch_shapes=[pltpu.VMEM((NSA_HEADS, 2, Q_BLOCK, Q_BLOCK), F32),
                        pltpu.VMEM((Q_BLOCK, NSA_WIDTH), F32)],
        compiler_params=_cparams(("arbitrary",), 56),
        name="nsa_selected_window",
    )(rel_bias, u_bf, u_bf, u_bf, u_bf, u_bf, sel, o_c, gate_logits, g_nsa.reshape(1, NSA_WIDTH))


def _layer_norm_rows(z, g, b):
    mu = jnp.mean(z, axis=-1, keepdims=True)
    zc = z - mu
    var = jnp.mean(zc * zc, axis=-1, keepdims=True)
    return zc * lax.rsqrt(var + NORM_EPS) * g + b


def _outproj_kernel(yl_ref, yn_ref, w_ref, x_ref, g_ref, b_ref, o_ref, ob_ref):
    acc = jnp.dot(yl_ref[...], w_ref[0:LRU_WIDTH, :], preferred_element_type=F32)
    acc = acc + jnp.dot(yn_ref[...], w_ref[LRU_WIDTH:LRU_WIDTH + NSA_WIDTH, :], preferred_element_type=F32)
    y = _layer_norm_rows(DN_ALPHA * x_ref[...] + acc, g_ref[...], b_ref[...])
    o_ref[...] = y
    ob_ref[...] = y.astype(BF16)


def _out_proj_ln(y_lru, y_nsa, w_out_b, x, g, b, tm=256):
    S, D = x.shape
    vec = pl.BlockSpec((1, D), lambda i: (0, 0))
    return pl.pallas_call(
        _outproj_kernel,
        grid=(S // tm,),
        in_specs=[pl.BlockSpec((tm, LRU_WIDTH), lambda i: (i, 0)),
                  pl.BlockSpec((tm, NSA_WIDTH), lambda i: (i, 0)),
                  pl.BlockSpec((LRU_WIDTH + NSA_WIDTH, D), lambda i: (0, 0)),
                  pl.BlockSpec((tm, D), lambda i: (i, 0)), vec, vec],
        out_specs=[pl.BlockSpec((tm, D), lambda i: (i, 0)), pl.BlockSpec((tm, D), lambda i: (i, 0))],
        out_shape=[jax.ShapeDtypeStruct((S, D), F32), jax.ShapeDtypeStruct((S, D), BF16)],
        compiler_params=_cparams(("arbitrary",), 48),
        name="out_proj_ln",
    )(y_lru, y_nsa, w_out_b, x, g.reshape(1, D), b.reshape(1, D))


def _router_kernel(x_ref, w_ref, b_ref, e_ref, g_ref, r_ref, cnt_ref, carry):
    i = pl.program_id(0)
    tm = x_ref.shape[0]

    @pl.when(i == 0)
    def _():
        carry[...] = jnp.zeros(carry.shape, F32)

    logits = jnp.dot(x_ref[...], w_ref[...], preferred_element_type=F32,
                     precision=lax.Precision.HIGHEST) + b_ref[...]
    lane = lax.broadcasted_iota(jnp.int32, (tm, LANES), 1)
    lanef = lane.astype(F32)
    s = logits
    picks = []
    chosen = jnp.zeros((tm, LANES), jnp.bool_)
    for _ in range(TOP_K):
        mx = jnp.max(s, axis=-1, keepdims=True)
        first = jnp.min(jnp.where(s == mx, lanef, float(LANES)), axis=-1, keepdims=True)
        pick = lanef == first
        picks.append((pick, first))
        chosen = chosen | pick
        s = jnp.where(pick, -3e38, s)
    top = jnp.max(logits, axis=-1, keepdims=True)
    ex = jnp.where(chosen, jnp.exp(logits - top), 0.0)
    gate = ex / jnp.sum(ex, axis=-1, keepdims=True)

    onehot = jnp.where(chosen, 1.0, 0.0)
    rr = lax.broadcasted_iota(jnp.int32, (tm, tm), 0)
    rc = lax.broadcasted_iota(jnp.int32, (tm, tm), 1)
    before = (rr > rc).astype(F32).astype(BF16)
    rank = jnp.dot(before, onehot.astype(BF16), preferred_element_type=F32) + carry[0:1, :]
    carry[0:1, :] = carry[0:1, :] + jnp.sum(onehot, axis=0, keepdims=True)
    cnt_ref[...] = carry[...]

    e_out = jnp.zeros((tm, LANES), jnp.int32)
    g_out = jnp.zeros((tm, LANES), F32)
    r_out = jnp.zeros((tm, LANES), F32)
    for k, (pick, first) in enumerate(picks):
        e_out = jnp.where(lane == k, first.astype(jnp.int32), e_out)
        g_out = jnp.where(lane == k, jnp.sum(jnp.where(pick, gate, 0.0), axis=-1, keepdims=True), g_out)
        r_out = jnp.where(lane == k, jnp.sum(jnp.where(pick, rank, 0.0), axis=-1, keepdims=True), r_out)
    e_ref[...] = e_out
    g_ref[...] = g_out
    r_ref[...] = r_out


def _router(x, w_pad, b_pad, tm=256):
    S, D = x.shape
    out = pl.BlockSpec((tm, LANES), lambda i: (i, 0))
    return pl.pallas_call(
        _router_kernel,
        grid=(S // tm,),
        in_specs=[pl.BlockSpec((tm, D), lambda i: (i, 0)),
                  pl.BlockSpec((D, LANES), lambda i: (0, 0)),
                  pl.BlockSpec((1, LANES), lambda i: (0, 0))],
        out_specs=[out, out, out, pl.BlockSpec((8, LANES), lambda i: (0, 0))],
        out_shape=[jax.ShapeDtypeStruct((S, LANES), jnp.int32), jax.ShapeDtypeStruct((S, LANES), F32),
                   jax.ShapeDtypeStruct((S, LANES), F32), jax.ShapeDtypeStruct((8, LANES), F32)],
        scratch_shapes=[pltpu.VMEM((8, LANES), F32)],
        compiler_params=_cparams(("arbitrary",), 32),
        name="moe_router",
    )(x, w_pad, b_pad)


def _expert_kernel(sb_ref, se_ref, sf_ref, sj_ref, so_ref, sv_ref,
                   x_ref, wgu_ref, bgu_ref, wdn_ref, bdn_ref, o_ref, wgu_b, wdn_b, acc):
    s = pl.program_id(0)
    valid = sv_ref[s] == 1
    f = sf_ref[s]
    j = sj_ref[s]
    SUB = x_ref.shape[0]
    FT2 = wgu_ref.shape[1]

    @pl.when(valid & (j == 0))
    def _():
        wgu_b[...] = wgu_ref[...].astype(BF16)
        wdn_b[...] = wdn_ref[...].astype(BF16)

    @pl.when(valid)
    def _():
        gu = jnp.dot(x_ref[...], wgu_b[...], preferred_element_type=F32) + bgu_ref[...]
        gate = jnp.minimum(gu, SWIGLU_LIMIT)
        act = gate * jax.nn.sigmoid(SWIGLU_ALPHA * gate)
        up1 = jnp.clip(gu, -SWIGLU_LIMIT, SWIGLU_LIMIT) + 1.0
        pr = lax.broadcasted_iota(jnp.int32, (2 * LANES, LANES), 0)
        pc = lax.broadcasted_iota(jnp.int32, (2 * LANES, LANES), 1)
        pick_even = (pr == 2 * pc).astype(F32).astype(BF16)
        hs = []
        for c in range(FT2 // (2 * LANES)):
            parts = []
            for k in range(2):
                cs = slice((2 * c + k) * LANES, (2 * c + k + 1) * LANES)
                parts.append((act[:, cs] * pltpu.roll(up1[:, cs], LANES - 1, 1)).astype(BF16))
            pair = jnp.concatenate(parts, axis=1)
            hs.append(jnp.dot(pair, pick_even, preferred_element_type=F32).astype(BF16))
        h = jnp.concatenate(hs, axis=1)
        y = jnp.dot(h, wdn_b[...], preferred_element_type=F32)
        rows = pl.ds(pl.multiple_of(j * SUB, SUB), SUB)

        @pl.when(f == 0)
        def _():
            acc[rows, :] = y + bdn_ref[...]

        @pl.when((f > 0) & (f < MOE_NF - 1))
        def _():
            acc[rows, :] = acc[rows, :] + y

        @pl.when(f == MOE_NF - 1)
        def _():
            o_ref[...] = acc[rows, :] + y


def _experts(meta, buf, w_gu, b_gu3, w_dn, b_dn3, layer):
    sb, se, sf, sj, so, sv = meta
    NS = sb.shape[0]
    R, D = buf.shape
    grid_spec = pltpu.PrefetchScalarGridSpec(
        num_scalar_prefetch=6,
        grid=(NS,),
        in_specs=[
            pl.BlockSpec((MOE_SUB, D), lambda s, sb, se, sf, sj, so, sv: (sb[s], 0)),
            pl.BlockSpec((None, None, D, 2 * MOE_FT), lambda s, sb, se, sf, sj, so, sv: (layer, se[s], 0, sf[s])),
            pl.BlockSpec((None, None, 1, 2 * MOE_FT), lambda s, sb, se, sf, sj, so, sv: (layer, se[s], 0, sf[s])),
            pl.BlockSpec((None, None, MOE_FT, D), lambda s, sb, se, sf, sj, so, sv: (layer, se[s], sf[s], 0)),
            pl.BlockSpec((None, None, 1, D), lambda s, sb, se, sf, sj, so, sv: (layer, se[s], 0, 0)),
        ],
        out_specs=pl.BlockSpec((MOE_SUB, D), lambda s, sb, se, sf, sj, so, sv: (so[s], 0)),
        scratch_shapes=[pltpu.VMEM((D, 2 * MOE_FT), BF16), pltpu.VMEM((MOE_FT, D), BF16),
                        pltpu.VMEM((MOE_J * MOE_SUB, D), F32)],
    )
    return pl.pallas_call(
        _expert_kernel,
        grid_spec=grid_spec,
        out_shape=jax.ShapeDtypeStruct((R, D), F32),
        compiler_params=_cparams(("arbitrary",), 60),
        name="moe_experts",
    )(sb, se, sf, sj, so, sv, buf, w_gu, b_gu3, w_dn, b_dn3)


def _moe_tables(e4, r4, cnt, n_sub_max):
    E = N_EXPERTS
    nsub = (cnt + MOE_SUB - 1) // MOE_SUB
    sub_end = jnp.cumsum(nsub)
    sub_start = sub_end - nsub
    n_blk = sub_end[-1]
    dest = sub_start[e4] * MOE_SUB + r4

    b = jnp.arange(n_sub_max, dtype=jnp.int32)
    be = jnp.minimum(jnp.searchsorted(sub_end, b, side='right'), E - 1).astype(jnp.int32)
    lb = b - sub_start[be]
    jb = lb % MOE_J
    b0 = b - jb
    nj = jnp.minimum(MOE_J, nsub[be] - (lb // MOE_J) * MOE_J)
    NS = MOE_NF * n_sub_max
    bb = jnp.repeat(b, MOE_NF)
    ff = jnp.tile(jnp.arange(MOE_NF, dtype=jnp.int32), n_sub_max)
    step = jnp.where(bb < n_blk, MOE_NF * b0[bb] + ff * nj[bb] + jb[bb], NS)
    step_b = jnp.zeros((NS,), jnp.int32).at[step].set(bb, mode='drop')
    step_f = jnp.zeros((NS,), jnp.int32).at[step].set(ff, mode='drop')
    sidx = jnp.arange(NS, dtype=jnp.int32)
    n_steps = MOE_NF * n_blk
    valid = sidx < n_steps
    last = jnp.maximum(n_steps - 1, 0)
    step_b = jnp.where(valid, step_b, step_b[last])
    step_f = jnp.where(valid, step_f, step_f[last])
    out_b = jnp.where(step_f == MOE_NF - 1, step_b, b0[step_b])
    meta = (step_b, be[step_b], step_f, jb[step_b], out_b, valid.astype(jnp.int32))
    return dest.astype(jnp.int32), meta


def _combine_kernel(rows_ref, g4_ref, x_ref, g_ref, b_ref, o_ref, ob_ref):
    g4 = g4_ref[...]
    y = g4[:, 0:1] * rows_ref[0]
    for k in range(1, TOP_K):
        y = y + g4[:, k:k + 1] * rows_ref[k]
    out = _layer_norm_rows(DN_ALPHA * x_ref[...] + y, g_ref[...], b_ref[...])
    o_ref[...] = out
    ob_ref[...] = out.astype(BF16)


def _combine_ln(rows, g4, x, g, b, tm=128):
    S, D = x.shape
    vec = pl.BlockSpec((1, D), lambda i: (0, 0))
    return pl.pallas_call(
        _combine_kernel,
        grid=(S // tm,),
        in_specs=[pl.BlockSpec((TOP_K, tm, D), lambda i: (0, i, 0)),
                  pl.BlockSpec((tm, LANES), lambda i: (i, 0)),
                  pl.BlockSpec((tm, D), lambda i: (i, 0)), vec, vec],
        out_specs=[pl.BlockSpec((tm, D), lambda i: (i, 0)), pl.BlockSpec((tm, D), lambda i: (i, 0))],
        out_shape=[jax.ShapeDtypeStruct((S, D), F32), jax.ShapeDtypeStruct((S, D), BF16)],
        compiler_params=_cparams(("arbitrary",), 48),
        name="moe_combine_ln",
    )(rows, g4, x, g.reshape(1, D), b.reshape(1, D))


def _nsa_group(u_bf, gate_logits, cmp_pos, cmp_w1, cmp_w2, rel_bias, g_nsa):
    S = u_bf.shape[0]
    G = NSA_KV_HEADS
    kvc = u_bf[:, NSA_WIDTH:NSA_WIDTH + 2 * KV_WIDTH]
    kv_rows = kvc.reshape(S, 2 * G, HEAD_DIM).transpose(1, 0, 2).reshape(2 * G, S // CMP_STRIDE,
                                                                          CMP_STRIDE * HEAD_DIM)
    kcv = _compress(kv_rows, cmp_pos, cmp_w1, cmp_w2)
    o_c, sel = _cmp_attention(rel_bias, u_bf, kcv, S // SEL_BLOCK)
    return _sel_attention(rel_bias, u_bf, sel, o_c, gate_logits, g_nsa)


def _moe(x, xb, w_router, b_router, w_gu, b_gu, w_dn, b_dn, ln_g, ln_b, layer):
    S, D = x.shape
    E = N_EXPERTS
    w_pad = jnp.pad(w_router, ((0, 0), (0, LANES - E)))
    b_pad = jnp.pad(b_router, (0, LANES - E), constant_values=NEG_INF).reshape(1, LANES)
    e_out, g_out, r_out, cnt_out = _router(x, w_pad, b_pad)
    e4 = e_out[:, :TOP_K]
    r4 = r_out[:, :TOP_K].astype(jnp.int32)
    cnt = cnt_out[0, :E].astype(jnp.int32)
    n_sub_max = (S * TOP_K) // MOE_SUB + E
    dest, meta = _moe_tables(e4, r4, cnt, n_sub_max)
    flat = dest.reshape(-1)
    buf = jnp.zeros((n_sub_max * MOE_SUB, D), BF16).at[flat].set(jnp.repeat(xb, TOP_K, axis=0))
    ybuf = _experts(meta, buf, w_gu, b_gu.reshape(DEPTH, E, 1, 2 * D_FF), w_dn,
                    b_dn.reshape(DEPTH, E, 1, D), layer)
    rows = ybuf[dest.T]
    return _combine_ln(rows, g_out, x, ln_g, ln_b)


def kernel(x, w_in, conv_w, conv_b, lru_wa, lru_ba, lru_wi, lru_bi, lru_lambda, cmp_pos, cmp_w1, cmp_w2,
           rel_bias, g_lru, g_nsa, w_out, ln1_g, ln1_b, w_router, b_router, w_gate_up, b_gate_up, w_down,
           b_down, ln2_g, ln2_b):
    B, S, D = x.shape
    assert B == 1
    x = x.reshape(S, D)
    xb = x.astype(BF16)
    n_gate = 3 * NSA_HEADS
    for l in range(DEPTH):
        u_f32 = _project(xb, w_in, l, 0, 2 * LRU_WIDTH // 512, 512, F32)
        u_bf = _project(xb, w_in, l, 2 * LRU_WIDTH // 512, (MAIN_COLS - 2 * LRU_WIDTH) // 512, 512, BF16)
        w_gate = jnp.pad(w_in[l, :, MAIN_COLS:], ((0, 0), (0, LANES - n_gate)))[None]
        gate_logits = _project(xb, w_gate, 0, 0, 1, LANES, F32)
        y_lru = _lru_group(u_f32, conv_w[l], conv_b[l], lru_wa[l], lru_ba[l], lru_wi[l], lru_bi[l],
                           lru_lambda[l], g_lru[l])
        y_nsa = _nsa_group(u_bf, gate_logits, cmp_pos[l], cmp_w1[l], cmp_w2[l], rel_bias, g_nsa[l])
        x, xb = _out_proj_ln(y_lru, y_nsa, w_out[l].astype(BF16), x, ln1_g[l], ln1_b[l])
        x, xb = _moe(x, xb, w_router[l], b_router[l], w_gate_up, b_gate_up, w_down, b_down,
                     ln2_g[l], ln2_b[l], l)
    return x.reshape(B, S, D)
```

```python
import functools
import math

import numpy as np
import jax
import jax.numpy as jnp
from jax import lax
from jax.experimental import pallas as pl
from jax.experimental.pallas import tpu as pltpu

D_MODEL = 2048
DEPTH = 2
LRU_WIDTH = 1024
LRU_BLOCKS = 8
LRU_BLOCK_W = LRU_WIDTH // LRU_BLOCKS
CONV_WIDTH = 4
LRU_C = 8.0
NSA_HEADS = 8
NSA_KV_HEADS = 2
HEADS_PER_GROUP = NSA_HEADS // NSA_KV_HEADS
HEAD_DIM = 128
NSA_WIDTH = NSA_HEADS * HEAD_DIM
KV_WIDTH = NSA_KV_HEADS * HEAD_DIM
CMP_BLOCK = 32
CMP_STRIDE = 16
SEL_BLOCK = 64
N_SEL = 16
WINDOW = 512
Q_BLOCK = 128
N_BUCKETS = 32
MAX_DISTANCE = 128
N_EXPERTS = 32
TOP_K = 4
D_FF = 2048
SWIGLU_LIMIT = 7.0
SWIGLU_ALPHA = 1.702
DN_ALPHA = (2 * DEPTH) ** 0.25
NORM_EPS = 1e-5
NEG_INF = -1e30
FORCED_SCORE = 1e9
ATTN_SCALE = HEAD_DIM ** -0.5

LOG2E = math.log2(math.e)
PAD_BLK = WINDOW // SEL_BLOCK
M_INIT = -1e9

LANES = 128
MIB = 1024 * 1024
MAIN_COLS = 2 * LRU_WIDTH + NSA_WIDTH + 6 * KV_WIDTH
BF16 = jnp.bfloat16
F32 = jnp.float32

MOE_SUB = 256
MOE_J = 5
MOE_FT = 512
MOE_NF = D_FF // MOE_FT
ROW_TILES = D_MODEL // LANES


def _bucket_thresholds():
    max_exact = N_BUCKETS // 2
    out = []
    for b in range(1, N_BUCKETS):
        if b <= max_exact:
            out.append(b)
            continue
        d = max_exact
        while True:
            v = math.log(d / max_exact) / math.log(MAX_DISTANCE / max_exact) * (N_BUCKETS - max_exact)
            assert abs(v - round(v)) > 1e-3 or d == max_exact
            if min(max_exact + int(v), N_BUCKETS - 1) >= b:
                break
            d += 1
        out.append(d)
    return tuple(out)


BUCKET_TH = _bucket_thresholds()


def _cparams(semantics, vmem_mib):
    return pltpu.CompilerParams(dimension_semantics=semantics, vmem_limit_bytes=vmem_mib * MIB)


def _nt_dot(a, b):
    return lax.dot_general(a, b, (((1,), (1,)), ((), ())), preferred_element_type=F32)


def _bias_of_dist(rb_ref, head, d):
    b = jnp.full(d.shape, rb_ref[0, head], F32)
    for k in range(1, N_BUCKETS):
        b = jnp.where(d >= BUCKET_TH[k - 1], rb_ref[k, head], b)
    return b


def _mm_kernel(x_ref, w_ref, o_ref, wb_ref):
    @pl.when(pl.program_id(1) == 0)
    def _():
        wb_ref[...] = w_ref[...].astype(BF16)

    o_ref[...] = jnp.dot(x_ref[...], wb_ref[...], preferred_element_type=F32).astype(o_ref.dtype)


def _project(xb, w3, layer, col_blk0, n_blks, tn, out_dtype, tm=512):
    S, D = xb.shape
    return pl.pallas_call(
        _mm_kernel,
        grid=(n_blks, S // tm),
        in_specs=[pl.BlockSpec((tm, D), lambda n, m: (m, 0)),
                  pl.BlockSpec((None, D, tn), lambda n, m: (layer, 0, col_blk0 + n))],
        out_specs=pl.BlockSpec((tm, tn), lambda n, m: (m, n)),
        out_shape=jax.ShapeDtypeStruct((S, n_blks * tn), out_dtype),
        scratch_shapes=[pltpu.VMEM((D, tn), BF16)],
        compiler_params=_cparams(("arbitrary", "arbitrary"), 40),
        name="in_proj",
    )(xb, w3)


def _gelu_tanh(x):
    return 0.5 * x * (1.0 + jnp.tanh(math.sqrt(2.0 / math.pi) * (x + 0.044715 * (x * x * x))))


def _lru_kernel(xr_ref, xg_ref, cw_ref, cb_ref, wa_ref, ba_ref, wi_ref, bi_ref, lam_ref, g_ref,
                o_ref, xbuf, hc):
    i = pl.program_id(0)
    T = xr_ref.shape[0]
    W = xr_ref.shape[1]
    HALO = 8

    @pl.when(i == 0)
    def _():
        xbuf[0:HALO, :] = jnp.zeros((HALO, W), F32)
        hc[...] = jnp.zeros(hc.shape, F32)

    @pl.when(i > 0)
    def _():
        xbuf[0:HALO, :] = xbuf[T:T + HALO, :]

    xbuf[HALO:HALO + T, :] = xr_ref[...]
    cw = cw_ref[...]
    xc = cb_ref[...] + xbuf[HALO - 3:HALO - 3 + T, :] * cw[0:1]
    for k in range(1, CONV_WIDTH):
        xc = xc + xbuf[HALO - 3 + k:HALO - 3 + k + T, :] * cw[k:k + 1]

    xcb = xc.astype(BF16)
    ra, ia = [], []
    for n in range(LRU_BLOCKS):
        blk = xcb[:, n * LRU_BLOCK_W:(n + 1) * LRU_BLOCK_W]
        ra.append(jnp.dot(blk, wa_ref[n].astype(BF16), preferred_element_type=F32))
        ia.append(jnp.dot(blk, wi_ref[n].astype(BF16), preferred_element_type=F32))
    r = jax.nn.sigmoid(jnp.concatenate(ra, axis=1) + ba_ref[...])
    ig = jax.nn.sigmoid(jnp.concatenate(ia, axis=1) + bi_ref[...])

    z = -lam_ref[...]
    softplus = jnp.maximum(z, 0.0) + jnp.log(1.0 + jnp.exp(-jnp.abs(z)))
    log_a = -LRU_C * r * softplus
    a = jnp.exp(log_a)
    u = jnp.sqrt(1.0 - a * a) * (ig * xc)

    row = lax.broadcasted_iota(jnp.int32, (T, W), 0)
    A, U = a, u
    d = 1
    while d < T:
        keep = row >= d
        a_sh = jnp.where(keep, pltpu.roll(A, d, 0), 1.0)
        u_sh = jnp.where(keep, pltpu.roll(U, d, 0), 0.0)
        U = A * u_sh + U
        A = A * a_sh
        d *= 2
    h = U + A * hc[...]
    hc[...] = h[T - 1:T, :]

    y = h * _gelu_tanh(xg_ref[...])
    ms = jnp.mean(y * y, axis=-1, keepdims=True)
    o_ref[...] = (y * lax.rsqrt(ms + NORM_EPS) * g_ref[...]).astype(o_ref.dtype)


def _lru_group(u_f32, conv_w, conv_b, wa, ba, wi, bi, lam, g_lru, T=256):
    S = u_f32.shape[0]
    W = LRU_WIDTH
    row = lambda v: v.reshape(1, W)
    vec = pl.BlockSpec((1, W), lambda i: (0, 0))
    mat = pl.BlockSpec((LRU_BLOCKS, LRU_BLOCK_W, LRU_BLOCK_W), lambda i: (0, 0, 0))
    return pl.pallas_call(
        _lru_kernel,
        grid=(S // T,),
        in_specs=[pl.BlockSpec((T, W), lambda i: (i, 0)),
                  pl.BlockSpec((T, W), lambda i: (i, 1)),
                  pl.BlockSpec((CONV_WIDTH, W), lambda i: (0, 0)),
                  vec, mat, vec, mat, vec, vec, vec],
        out_specs=pl.BlockSpec((T, W), lambda i: (i, 0)),
        out_shape=jax.ShapeDtypeStruct((S, W), BF16),
        scratch_shapes=[pltpu.VMEM((T + 8, W), F32), pltpu.VMEM((1, W), F32)],
        compiler_params=_cparams(("arbitrary",), 48),
        name="rg_lru",
    )(u_f32, u_f32, conv_w, row(conv_b), wa, row(ba), wi, row(bi), row(lam), row(g_lru))


def _compress_kernel(x_ref, pos_ref, w1_ref, w2_ref, o_ref):
    NJ = x_ref.shape[0]
    half = (CMP_BLOCK // 2) * HEAD_DIM
    x = x_ref[...].astype(F32)
    a = (x + pos_ref[0:1, :]).astype(BF16)
    b = (x + pos_ref[1:2, :]).astype(BF16)
    y1 = jnp.dot(a, w1_ref[0:half, :].astype(BF16), preferred_element_type=F32)
    y2 = jnp.dot(b, w1_ref[half:2 * half, :].astype(BF16), preferred_element_type=F32)
    hmid = y1 + pltpu.roll(y2, NJ - 1, 0)
    o_ref[...] = jnp.dot(_gelu_tanh(hmid).astype(BF16), w2_ref[...].astype(BF16),
                         preferred_element_type=F32).astype(o_ref.dtype)


def _compress(kv_rows, pos, w1, w2):
    _, NJ, RW = kv_rows.shape
    return pl.pallas_call(
        _compress_kernel,
        grid=(4,),
        in_specs=[pl.BlockSpec((None, NJ, RW), lambda a: (a, 0, 0)),
                  pl.BlockSpec((None, 2, RW), lambda a: (a // 2, 0, 0)),
                  pl.BlockSpec((None, CMP_BLOCK * HEAD_DIM, HEAD_DIM), lambda a: (a // 2, 0, 0)),
                  pl.BlockSpec((None, HEAD_DIM, HEAD_DIM), lambda a: (a // 2, 0, 0))],
        out_specs=pl.BlockSpec((None, NJ, HEAD_DIM), lambda a: (a, 0, 0)),
        out_shape=jax.ShapeDtypeStruct((4, NJ, HEAD_DIM), BF16),
        compiler_params=_cparams(("arbitrary",), 32),
        name="kv_compress",
    )(kv_rows, pos.reshape(2, 2, RW), w1, w2)


def _cmp_attn_kernel(rb_ref, q_ref, kcv_ref, oc_ref, sel_ref, tchi, tclo):
    qb = pl.program_id(0)
    t0 = qb * Q_BLOCK
    NJ = kcv_ref.shape[1]
    NB = NJ * CMP_STRIDE // SEL_BLOCK
    NBP = sel_ref.shape[2]
    n_sel = min(N_SEL, NB)
    LOCAL = 2 * Q_BLOCK // CMP_STRIDE

    @pl.when(qb == 0)
    def _():
        i = lax.broadcasted_iota(jnp.int32, (Q_BLOCK, LANES), 0)
        j = lax.broadcasted_iota(jnp.int32, (Q_BLOCK, LANES), 1)
        d = jnp.maximum(i - CMP_STRIDE * j + (Q_BLOCK - CMP_STRIDE), 0)
        for h in range(NSA_HEADS):
            corr = jnp.where(j < LOCAL, _bias_of_dist(rb_ref, h, d) - rb_ref[N_BUCKETS - 1, h], 0.0)
            hi = corr.astype(BF16)
            tchi[h] = hi
            tclo[h] = (corr - hi.astype(F32)).astype(BF16)

    row = lax.broadcasted_iota(jnp.int32, (Q_BLOCK, NJ), 0)
    col = lax.broadcasted_iota(jnp.int32, (Q_BLOCK, NJ), 1)
    mask = CMP_STRIDE * col + (CMP_BLOCK - 1) <= t0 + row
    jj = lax.broadcasted_iota(jnp.int32, (LANES, NJ), 0)
    cc = lax.broadcasted_iota(jnp.int32, (LANES, NJ), 1)
    place = (cc == (Q_BLOCK // CMP_STRIDE) * (qb - 1) + jj).astype(F32).astype(BF16)
    bj = lax.broadcasted_iota(jnp.int32, (NB, NJ), 0)
    bc = lax.broadcasted_iota(jnp.int32, (NB, NJ), 1)
    ratio = SEL_BLOCK // CMP_STRIDE
    overlap_t = ((bc <= ratio * bj + ratio - 1) & (bc >= ratio * bj - 1)).astype(F32).astype(BF16)

    sj = lax.broadcasted_iota(jnp.int32, (NB, Q_BLOCK), 0)
    st = t0 + lax.broadcasted_iota(jnp.int32, (NB, Q_BLOCK), 1)
    cur = st // SEL_BLOCK
    blk_ok = sj <= cur
    forced = (sj == 0) | (sj == cur) | (sj == cur - 1)

    for g in range(NSA_KV_HEADS):
        kc = kcv_ref[g]
        vc = kcv_ref[NSA_KV_HEADS + g]
        psum = jnp.zeros((Q_BLOCK, NJ), F32)
        for h in range(HEADS_PER_GROUP):
            hh = g * HEADS_PER_GROUP + h
            qh = q_ref[:, hh * HEAD_DIM:(hh + 1) * HEAD_DIM]
            lc = _nt_dot(qh, kc) * ATTN_SCALE + rb_ref[N_BUCKETS - 1, hh]
            lc = lc + jnp.dot(tchi[hh], place, preferred_element_type=F32)
            lc = lc + jnp.dot(tclo[hh], place, preferred_element_type=F32)
            z = jnp.where(mask, lc, NEG_INF)
            m = jnp.max(z, axis=-1, keepdims=True)
            e = jnp.where(mask, jnp.exp(z - m), 0.0)
            l = jnp.sum(e, axis=-1, keepdims=True)
            p = e / jnp.where(l > 0.0, l, 1.0)
            oc_ref[:, hh * HEAD_DIM:(hh + 1) * HEAD_DIM] = jnp.dot(
                p.astype(BF16), vc, preferred_element_type=F32)
            psum = psum + p
        p_hi = psum.astype(BF16)
        p_lo = (psum - p_hi.astype(F32)).astype(BF16)
        imp_t = _nt_dot(overlap_t, p_hi) + _nt_dot(overlap_t, p_lo)
        score = jnp.where(blk_ok, jnp.where(forced, FORCED_SCORE, imp_t), NEG_INF)
        chosen = jnp.zeros((NB, Q_BLOCK), jnp.bool_)
        sjf = sj.astype(F32)
        for _ in range(n_sel):
            mx = jnp.max(score, axis=0, keepdims=True)
            first = jnp.min(jnp.where(score == mx, sjf, float(NB)), axis=0, keepdims=True)
            pick = sjf == first
            chosen = chosen | pick
            score = jnp.where(pick, -3e38, score)
        pen_t = jnp.where(chosen & blk_ok, 0.0, NEG_INF)
        padded = jnp.concatenate([jnp.full((PAD_BLK, Q_BLOCK), NEG_INF, F32), pen_t,
                                  jnp.zeros((NBP - PAD_BLK - NB, Q_BLOCK), F32)], axis=0)
        pen = jnp.concatenate([padded[r:r + LANES].T for r in range(0, NBP, LANES)], axis=1)
        sel_ref[g] = pen.astype(sel_ref.dtype)


def _cmp_attention(rel_bias, u_bf, kcv):
    S = u_bf.shape[0]
    NJ = kcv.shape[1]
    NB = -(-(S // SEL_BLOCK + PAD_BLK) // LANES) * LANES
    return pl.pallas_call(
        _cmp_attn_kernel,
        grid=(S // Q_BLOCK,),
        in_specs=[pl.BlockSpec(memory_space=pltpu.SMEM),
                  pl.BlockSpec((Q_BLOCK, NSA_WIDTH), lambda i: (i, 0)),
                  pl.BlockSpec((4, NJ, HEAD_DIM), lambda i: (0, 0, 0))],
        out_specs=[pl.BlockSpec((Q_BLOCK, NSA_WIDTH), lambda i: (i, 0)),
                   pl.BlockSpec((NSA_KV_HEADS, Q_BLOCK, NB), lambda i: (0, i, 0))],
        out_shape=[jax.ShapeDtypeStruct((S, NSA_WIDTH), F32),
                   jax.ShapeDtypeStruct((NSA_KV_HEADS, S, NB), BF16)],
        scratch_shapes=[pltpu.VMEM((NSA_HEADS, Q_BLOCK, LANES), BF16),
                        pltpu.VMEM((NSA_HEADS, Q_BLOCK, LANES), BF16)],
        compiler_params=_cparams(("arbitrary",), 32),
        name="nsa_compressed",
    )(rel_bias, u_bf, kcv)


def _sel_attn_kernel(rb_ref, q_ref, ks_ref, vs_ref, kw_ref, vw_ref, sel_ref, oc_ref, gl_ref, g_ref,
                     o_ref, tbl_s, tbl_w, yacc):
    qb = pl.program_id(0)
    Q = Q_BLOCK
    HPG = HEADS_PER_GROUP
    TW = WINDOW + Q
    FAR = 512
    NBP = sel_ref.shape[2]
    t0 = qb * Q

    @pl.when(qb == 0)
    def _():
        i = lax.broadcasted_iota(jnp.int32, (Q, TW), 0)
        c = lax.broadcasted_iota(jnp.int32, (Q, TW), 1)
        d = i - c + WINDOW
        dd = jnp.maximum(d, 0)
        for h in range(NSA_HEADS):
            corr = (_bias_of_dist(rb_ref, h, dd) - rb_ref[N_BUCKETS - 1, h]) * LOG2E
            tbl_s[h] = jnp.where(d < 0, NEG_INF, corr)
            tbl_w[h] = jnp.where((d < 0) | (d >= WINDOW), NEG_INF, corr)

    gates = jax.nn.sigmoid(gl_ref[...])
    col_w = lax.broadcasted_iota(jnp.int32, (1, TW), 1)
    before_start = jnp.where(col_w < WINDOW - t0, NEG_INF, 0.0)
    p_tail = pl.multiple_of(t0, Q)

    for g in range(NSA_KV_HEADS):
        q4 = jnp.concatenate(
            [(q_ref[:, (g * HPG + h) * HEAD_DIM:(g * HPG + h + 1) * HEAD_DIM].astype(F32)
              * (ATTN_SCALE * LOG2E)).astype(BF16) for h in range(HPG)], axis=0)
        selg = sel_ref[g]
        cols = slice(g * HEAD_DIM, (g + 1) * HEAD_DIM)
        heads = slice(g * HPG, (g + 1) * HPG)

        def attend(k_ref, v_ref, p0, width, add, carry):
            m, l, acc = carry
            kk = k_ref[pl.ds(p0, width), cols]
            vv = v_ref[pl.ds(p0, width), cols]
            s = _nt_dot(q4, kk)
            z = (s.reshape(HPG, Q, width) + add).reshape(HPG * Q, width)
            m_new = jnp.maximum(m, jnp.max(z, axis=-1, keepdims=True))
            alpha = jnp.exp2(m - m_new)
            e = jnp.exp2(z - m_new)
            l = alpha * l + jnp.sum(e, axis=-1, keepdims=True)
            acc = alpha * acc + jnp.dot(e.astype(BF16), vv, preferred_element_type=F32)
            return m_new, l, acc

        def block_mask(p0, width):
            bp = lax.broadcasted_iota(jnp.int32, (NBP, width), 0)
            kb = lax.broadcasted_iota(jnp.int32, (NBP, width), 1) // SEL_BLOCK
            expand = (bp - p0 // SEL_BLOCK == kb).astype(F32).astype(BF16)
            return jnp.dot(selg, expand, preferred_element_type=F32)[None]

        init = (jnp.full((HPG * Q, 1), M_INIT, F32), jnp.zeros((HPG * Q, 1), F32),
                jnp.zeros((HPG * Q, HEAD_DIM), F32))

        def far_body(i, carry):
            p0 = pl.multiple_of((qb - (FAR // Q) * (i + 1)) * Q, Q)
            return attend(ks_ref, vs_ref, p0, FAR, block_mask(p0, FAR), carry)

        carry = lax.fori_loop(0, jnp.maximum(qb - 1, 0) // (FAR // Q), far_body, init)
        m_s, l_s, acc_s = attend(ks_ref, vs_ref, p_tail, TW, tbl_s[heads] + block_mask(p_tail, TW), carry)
        o_s = acc_s / jnp.where(l_s > 0.0, l_s, 1.0)

        m_w, l_w, acc_w = attend(kw_ref, vw_ref, p_tail, TW, tbl_w[heads] + before_start[None], init)
        o_w = acc_w / jnp.where(l_w > 0.0, l_w, 1.0)

        for h in range(HPG):
            hh = g * HPG + h
            hc = slice(hh * HEAD_DIM, (hh + 1) * HEAD_DIM)
            rows = slice(h * Q, (h + 1) * Q)
            yacc[:, hc] = (gates[:, 3 * hh:3 * hh + 1] * oc_ref[:, hc]
                           + gates[:, 3 * hh + 1:3 * hh + 2] * o_s[rows]
                           + gates[:, 3 * hh + 2:3 * hh + 3] * o_w[rows])

    y = yacc[...]
    ms = jnp.mean(y * y, axis=-1, keepdims=True)
    o_ref[...] = (y * lax.rsqrt(ms + NORM_EPS) * g_ref[...]).astype(o_ref.dtype)


def _sel_attention(rel_bias, u_bf, kv_pad, sel, o_c, gate_logits, g_nsa):
    S = u_bf.shape[0]
    NB = sel.shape[2]

    def kv_spec(j):
        return pl.BlockSpec((WINDOW + S, KV_WIDTH), lambda i: (0, j))

    return pl.pallas_call(
        _sel_attn_kernel,
        grid=(S // Q_BLOCK,),
        in_specs=[pl.BlockSpec(memory_space=pltpu.SMEM),
                  pl.BlockSpec((Q_BLOCK, NSA_WIDTH), lambda i: (i, 0)),
                  kv_spec(0), kv_spec(1), kv_spec(2), kv_spec(3),
                  pl.BlockSpec((NSA_KV_HEADS, Q_BLOCK, NB), lambda i: (0, i, 0)),
                  pl.BlockSpec((Q_BLOCK, NSA_WIDTH), lambda i: (i, 0)),
                  pl.BlockSpec((Q_BLOCK, LANES), lambda i: (i, 0)),
                  pl.BlockSpec((1, NSA_WIDTH), lambda i: (0, 0))],
        out_specs=pl.BlockSpec((Q_BLOCK, NSA_WIDTH), lambda i: (i, 0)),
        out_shape=jax.ShapeDtypeStruct((S, NSA_WIDTH), BF16),
        scratch_shapes=[pltpu.VMEM((NSA_HEADS, Q_BLOCK, WINDOW + Q_BLOCK), F32),
                        pltpu.VMEM((NSA_HEADS, Q_BLOCK, WINDOW + Q_BLOCK), F32),
                        pltpu.VMEM((Q_BLOCK, NSA_WIDTH), F32)],
        compiler_params=_cparams(("arbitrary",), 60),
        name="nsa_selected_window",
    )(rel_bias, u_bf, kv_pad, kv_pad, kv_pad, kv_pad, sel, o_c, gate_logits, g_nsa.reshape(1, NSA_WIDTH))


def _layer_norm_rows(z, g, b):
    mu = jnp.mean(z, axis=-1, keepdims=True)
    zc = z - mu
    var = jnp.mean(zc * zc, axis=-1, keepdims=True)
    return zc * lax.rsqrt(var + NORM_EPS) * g + b


def _outproj_kernel(yl_ref, yn_ref, w_ref, x_ref, g_ref, b_ref, o_ref, ob_ref):
    acc = jnp.dot(yl_ref[...], w_ref[0:LRU_WIDTH, :], preferred_element_type=F32)
    acc = acc + jnp.dot(yn_ref[...], w_ref[LRU_WIDTH:LRU_WIDTH + NSA_WIDTH, :], preferred_element_type=F32)
    y = _layer_norm_rows(DN_ALPHA * x_ref[...] + acc, g_ref[...], b_ref[...])
    o_ref[...] = y
    ob_ref[...] = y.astype(BF16)


def _out_proj_ln(y_lru, y_nsa, w_out_b, x, g, b, tm=256):
    S, D = x.shape
    vec = pl.BlockSpec((1, D), lambda i: (0, 0))
    return pl.pallas_call(
        _outproj_kernel,
        grid=(S // tm,),
        in_specs=[pl.BlockSpec((tm, LRU_WIDTH), lambda i: (i, 0)),
                  pl.BlockSpec((tm, NSA_WIDTH), lambda i: (i, 0)),
                  pl.BlockSpec((LRU_WIDTH + NSA_WIDTH, D), lambda i: (0, 0)),
                  pl.BlockSpec((tm, D), lambda i: (i, 0)), vec, vec],
        out_specs=[pl.BlockSpec((tm, D), lambda i: (i, 0)), pl.BlockSpec((tm, D), lambda i: (i, 0))],
        out_shape=[jax.ShapeDtypeStruct((S, D), F32), jax.ShapeDtypeStruct((S, D), BF16)],
        compiler_params=_cparams(("arbitrary",), 48),
        name="out_proj_ln",
    )(y_lru, y_nsa, w_out_b, x, g.reshape(1, D), b.reshape(1, D))


def _router_kernel(x_ref, w_ref, b_ref, e_ref, g_ref, r_ref, cnt_ref, carry):
    i = pl.program_id(0)
    tm = x_ref.shape[0]

    @pl.when(i == 0)
    def _():
        carry[...] = jnp.zeros(carry.shape, F32)

    logits = jnp.dot(x_ref[...], w_ref[...], preferred_element_type=F32,
                     precision=lax.Precision.HIGHEST) + b_ref[...]
    lane = lax.broadcasted_iota(jnp.int32, (tm, LANES), 1)
    lanef = lane.astype(F32)
    s = logits
    picks = []
    chosen = jnp.zeros((tm, LANES), jnp.bool_)
    for _ in range(TOP_K):
        mx = jnp.max(s, axis=-1, keepdims=True)
        first = jnp.min(jnp.where(s == mx, lanef, float(LANES)), axis=-1, keepdims=True)
        pick = lanef == first
        picks.append((pick, first))
        chosen = chosen | pick
        s = jnp.where(pick, -3e38, s)
    top = jnp.max(logits, axis=-1, keepdims=True)
    ex = jnp.where(chosen, jnp.exp(logits - top), 0.0)
    gate = ex / jnp.sum(ex, axis=-1, keepdims=True)

    onehot = jnp.where(chosen, 1.0, 0.0)
    rr = lax.broadcasted_iota(jnp.int32, (tm, tm), 0)
    rc = lax.broadcasted_iota(jnp.int32, (tm, tm), 1)
    before = (rr > rc).astype(F32).astype(BF16)
    rank = jnp.dot(before, onehot.astype(BF16), preferred_element_type=F32) + carry[0:1, :]
    carry[0:1, :] = carry[0:1, :] + jnp.sum(onehot, axis=0, keepdims=True)
    cnt_ref[...] = carry[...]

    e_out = jnp.zeros((tm, LANES), jnp.int32)
    g_out = jnp.zeros((tm, LANES), F32)
    r_out = jnp.zeros((tm, LANES), F32)
    for k, (pick, first) in enumerate(picks):
        e_out = jnp.where(lane == k, first.astype(jnp.int32), e_out)
        g_out = jnp.where(lane == k, jnp.sum(jnp.where(pick, gate, 0.0), axis=-1, keepdims=True), g_out)
        r_out = jnp.where(lane == k, jnp.sum(jnp.where(pick, rank, 0.0), axis=-1, keepdims=True), r_out)
    e_ref[...] = e_out
    g_ref[...] = g_out
    r_ref[...] = r_out


def _router(x, w_pad, b_pad, tm=256):
    S, D = x.shape
    out = pl.BlockSpec((tm, LANES), lambda i: (i, 0))
    return pl.pallas_call(
        _router_kernel,
        grid=(S // tm,),
        in_specs=[pl.BlockSpec((tm, D), lambda i: (i, 0)),
                  pl.BlockSpec((D, LANES), lambda i: (0, 0)),
                  pl.BlockSpec((1, LANES), lambda i: (0, 0))],
        out_specs=[out, out, out, pl.BlockSpec((8, LANES), lambda i: (0, 0))],
        out_shape=[jax.ShapeDtypeStruct((S, LANES), jnp.int32), jax.ShapeDtypeStruct((S, LANES), F32),
                   jax.ShapeDtypeStruct((S, LANES), F32), jax.ShapeDtypeStruct((8, LANES), F32)],
        scratch_shapes=[pltpu.VMEM((8, LANES), F32)],
        compiler_params=_cparams(("arbitrary",), 32),
        name="moe_router",
    )(x, w_pad, b_pad)


def _expert_kernel(sb_ref, se_ref, sf_ref, sj_ref, so_ref, sv_ref, tok_ref,
                   x_hbm, wgu_ref, bgu_ref, wdn_ref, bdn_ref, o_ref, wgu_b, wdn_b, acc, stage, xrows, sem):
    s = pl.program_id(0)
    n_steps = pl.num_programs(0)
    valid = sv_ref[s] == 1
    f = sf_ref[s]
    j = sj_ref[s]
    SUB = MOE_SUB
    FT2 = wgu_ref.shape[1]
    RT = ROW_TILES

    def gather_rows(step):
        base = sb_ref[step] * SUB
        slot = sj_ref[step] % 2

        def issue(r, carry):
            t = tok_ref[base + r]
            pltpu.make_async_copy(x_hbm.at[pl.ds(pl.multiple_of(t * RT, RT), RT), :],
                                  stage.at[slot, pl.ds(pl.multiple_of(r * RT, RT), RT), :],
                                  sem.at[slot]).start()
            return carry

        lax.fori_loop(0, SUB, issue, 0, unroll=8)

    @pl.when(s == 0)
    def _():
        gather_rows(0)

    nxt = jnp.minimum(s + 1, n_steps - 1)

    @pl.when((s + 1 < n_steps) & (sv_ref[nxt] == 1) & (sf_ref[nxt] == 0))
    def _():
        gather_rows(nxt)

    rows = pl.ds(pl.multiple_of(j * SUB, SUB), SUB)

    @pl.when(valid & (f == 0))
    def _():
        slot = j % 2
        pltpu.make_async_copy(x_hbm.at[pl.ds(0, SUB * RT), :], stage.at[slot], sem.at[slot]).wait()
        for c in range(RT):
            xrows[rows, c * LANES:(c + 1) * LANES] = stage[slot, pl.ds(c, SUB, stride=RT), :].astype(BF16)

    @pl.when(valid & (j == 0))
    def _():
        wgu_b[...] = wgu_ref[...].astype(BF16)
        wdn_b[...] = wdn_ref[...].astype(BF16)

    @pl.when(jnp.logical_not(valid))
    def _():
        o_ref[...] = jnp.zeros(o_ref.shape, F32)

    @pl.when(valid)
    def _():
        gu = jnp.dot(xrows[rows, :], wgu_b[...], preferred_element_type=F32) + bgu_ref[...]
        gate = jnp.minimum(gu, SWIGLU_LIMIT)
        act = gate * jax.nn.sigmoid(SWIGLU_ALPHA * gate)
        up1 = jnp.clip(gu, -SWIGLU_LIMIT, SWIGLU_LIMIT) + 1.0
        pr = lax.broadcasted_iota(jnp.int32, (2 * LANES, LANES), 0)
        pc = lax.broadcasted_iota(jnp.int32, (2 * LANES, LANES), 1)
        pick_even = (pr == 2 * pc).astype(F32).astype(BF16)
        hs = []
        for c in range(FT2 // (2 * LANES)):
            parts = []
            for k in range(2):
                cs = slice((2 * c + k) * LANES, (2 * c + k + 1) * LANES)
                parts.append((act[:, cs] * pltpu.roll(up1[:, cs], LANES - 1, 1)).astype(BF16))
            pair = jnp.concatenate(parts, axis=1)
            hs.append(jnp.dot(pair, pick_even, preferred_element_type=F32).astype(BF16))
        h = jnp.concatenate(hs, axis=1)
        y = jnp.dot(h, wdn_b[...], preferred_element_type=F32)

        @pl.when(f == 0)
        def _():
            acc[rows, :] = y + bdn_ref[...]

        @pl.when((f > 0) & (f < MOE_NF - 1))
        def _():
            acc[rows, :] = acc[rows, :] + y

        @pl.when(f == MOE_NF - 1)
        def _():
            yfin = acc[rows, :] + y
            for c in range(RT):
                o_ref[pl.ds(c, SUB, stride=RT), :] = yfin[:, c * LANES:(c + 1) * LANES]


def _experts(meta, tok, x_rows, w_gu, b_gu3, w_dn, b_dn3, layer):
    sb, se, sf, sj, so, sv = meta
    NS = sb.shape[0]
    R = tok.shape[0]
    D = D_MODEL
    RT = ROW_TILES

    def wmap(fn):
        return lambda s, sb, se, sf, sj, so, sv, tok: fn(s, se, sf, so)

    grid_spec = pltpu.PrefetchScalarGridSpec(
        num_scalar_prefetch=7,
        grid=(NS,),
        in_specs=[
            pl.BlockSpec(memory_space=pl.ANY),
            pl.BlockSpec((None, None, D, 2 * MOE_FT), wmap(lambda s, se, sf, so: (layer, se[s], 0, sf[s]))),
            pl.BlockSpec((None, None, 1, 2 * MOE_FT), wmap(lambda s, se, sf, so: (layer, se[s], 0, sf[s]))),
            pl.BlockSpec((None, None, MOE_FT, D), wmap(lambda s, se, sf, so: (layer, se[s], sf[s], 0))),
            pl.BlockSpec((None, None, 1, D), wmap(lambda s, se, sf, so: (layer, se[s], 0, 0))),
        ],
        out_specs=pl.BlockSpec((MOE_SUB * RT, LANES), wmap(lambda s, se, sf, so: (so[s], 0))),
        scratch_shapes=[pltpu.VMEM((D, 2 * MOE_FT), BF16), pltpu.VMEM((MOE_FT, D), BF16),
                        pltpu.VMEM((MOE_J * MOE_SUB, D), F32),
                        pltpu.VMEM((2, MOE_SUB * RT, LANES), F32),
                        pltpu.VMEM((MOE_J * MOE_SUB, D), BF16),
                        pltpu.SemaphoreType.DMA((2,))],
    )
    return pl.pallas_call(
        _expert_kernel,
        grid_spec=grid_spec,
        out_shape=jax.ShapeDtypeStruct((R * RT, LANES), F32),
        compiler_params=_cparams(("arbitrary",), 60),
        name="moe_experts",
    )(sb, se, sf, sj, so, sv, tok, x_rows, w_gu, b_gu3, w_dn, b_dn3)


def _moe_tables(e4, r4, cnt, n_sub_max):
    E = N_EXPERTS
    nsub = (cnt + MOE_SUB - 1) // MOE_SUB
    sub_end = jnp.cumsum(nsub)
    sub_start = sub_end - nsub
    n_blk = sub_end[-1]
    dest = sub_start[e4] * MOE_SUB + r4

    b = jnp.arange(n_sub_max, dtype=jnp.int32)
    be = jnp.minimum(jnp.sum(b[:, None] >= sub_end[None, :], axis=1), E - 1).astype(jnp.int32)
    lb = b - sub_start[be]
    jb = lb % MOE_J
    b0 = b - jb
    nj = jnp.minimum(MOE_J, nsub[be] - (lb // MOE_J) * MOE_J)
    NS = MOE_NF * n_sub_max
    bb = jnp.repeat(b, MOE_NF)
    ff = jnp.tile(jnp.arange(MOE_NF, dtype=jnp.int32), n_sub_max)
    step = jnp.where(bb < n_blk, MOE_NF * b0[bb] + ff * nj[bb] + jb[bb], NS)
    step_b = jnp.zeros((NS,), jnp.int32).at[step].set(bb, mode='drop')
    step_f = jnp.zeros((NS,), jnp.int32).at[step].set(ff, mode='drop')
    sidx = jnp.arange(NS, dtype=jnp.int32)
    n_steps = MOE_NF * n_blk
    valid = sidx < n_steps
    last = jnp.maximum(n_steps - 1, 0)
    step_b = jnp.where(valid, step_b, step_b[last])
    step_f = jnp.where(valid, step_f, step_f[last])
    out_b = jnp.where(step_f == MOE_NF - 1, step_b, b0[step_b])
    out_b = jnp.where(valid, out_b, jnp.minimum(n_blk + (sidx - n_steps), n_sub_max - 1))
    meta = (step_b, be[step_b], step_f, jb[step_b], out_b, valid.astype(jnp.int32))
    return dest.astype(jnp.int32), meta


def _combine_kernel(dest_ref, y_hbm, g4_ref, x_ref, g_ref, b_ref, o_ref, ob_ref, stage, ysum, sem):
    i = pl.program_id(0)
    n = pl.num_programs(0)
    tm = x_ref.shape[0]
    RT = ROW_TILES
    n_rows = tm * TOP_K

    def gather_rows(step, slot):
        base = step * n_rows

        def issue(r, carry):
            d = dest_ref[base + r]
            pltpu.make_async_copy(y_hbm.at[pl.ds(pl.multiple_of(d * RT, RT), RT), :],
                                  stage.at[slot, pl.ds(pl.multiple_of(r * RT, RT), RT), :],
                                  sem.at[slot]).start()
            return carry

        lax.fori_loop(0, n_rows, issue, 0, unroll=8)

    @pl.when(i == 0)
    def _():
        gather_rows(0, 0)

    @pl.when(i + 1 < n)
    def _():
        gather_rows(i + 1, (i + 1) % 2)

    slot = i % 2
    pltpu.make_async_copy(y_hbm.at[pl.ds(0, n_rows * RT), :], stage.at[slot], sem.at[slot]).wait()
    g4 = g4_ref[...]
    for c in range(RT):
        part = g4[:, 0:1] * stage[slot, pl.ds(c, tm, stride=TOP_K * RT), :]
        for k in range(1, TOP_K):
            part = part + g4[:, k:k + 1] * stage[slot, pl.ds(k * RT + c, tm, stride=TOP_K * RT), :]
        ysum[:, c * LANES:(c + 1) * LANES] = part
    out = _layer_norm_rows(DN_ALPHA * x_ref[...] + ysum[...], g_ref[...], b_ref[...])
    o_ref[...] = out
    ob_ref[...] = out.astype(BF16)


def _combine_ln(dest_flat, y_rows, g4, x, g, b, tm=128):
    S, D = x.shape
    vec = pl.BlockSpec((1, D), lambda i, d: (0, 0))
    grid_spec = pltpu.PrefetchScalarGridSpec(
        num_scalar_prefetch=1,
        grid=(S // tm,),
        in_specs=[pl.BlockSpec(memory_space=pl.ANY),
                  pl.BlockSpec((tm, LANES), lambda i, d: (i, 0)),
                  pl.BlockSpec((tm, D), lambda i, d: (i, 0)), vec, vec],
        out_specs=[pl.BlockSpec((tm, D), lambda i, d: (i, 0)), pl.BlockSpec((tm, D), lambda i, d: (i, 0))],
        scratch_shapes=[pltpu.VMEM((2, tm * TOP_K * ROW_TILES, LANES), F32),
                        pltpu.VMEM((tm, D), F32),
                        pltpu.SemaphoreType.DMA((2,))],
    )
    return pl.pallas_call(
        _combine_kernel,
        grid_spec=grid_spec,
        out_shape=[jax.ShapeDtypeStruct((S, D), F32), jax.ShapeDtypeStruct((S, D), BF16)],
        compiler_params=_cparams(("arbitrary",), 48),
        name="moe_combine_ln",
    )(dest_flat, y_rows, g4, x, g.reshape(1, D), b.reshape(1, D))


def _nsa_group(u_bf, gate_logits, cmp_pos, cmp_w1, cmp_w2, rel_bias, g_nsa):
    S = u_bf.shape[0]
    G = NSA_KV_HEADS
    kvc = u_bf[:, NSA_WIDTH:NSA_WIDTH + 2 * KV_WIDTH]
    kv_rows = kvc.reshape(S, 2 * G, HEAD_DIM).transpose(1, 0, 2).reshape(2 * G, S // CMP_STRIDE,
                                                                          CMP_STRIDE * HEAD_DIM)
    kcv = _compress(kv_rows, cmp_pos, cmp_w1, cmp_w2)
    o_c, sel = _cmp_attention(rel_bias, u_bf, kcv)
    kv_pad = jnp.pad(u_bf[:, NSA_WIDTH + 2 * KV_WIDTH:], ((WINDOW, 0), (0, 0)))
    return _sel_attention(rel_bias, u_bf, kv_pad, sel, o_c, gate_logits, g_nsa)


def _moe(x, w_router, b_router, w_gu, b_gu, w_dn, b_dn, ln_g, ln_b, layer):
    S, D = x.shape
    E = N_EXPERTS
    w_pad = jnp.pad(w_router, ((0, 0), (0, LANES - E)))
    b_pad = jnp.pad(b_router, (0, LANES - E), constant_values=NEG_INF).reshape(1, LANES)
    e_out, g_out, r_out, cnt_out = _router(x, w_pad, b_pad)
    e4 = e_out[:, :TOP_K]
    r4 = r_out[:, :TOP_K].astype(jnp.int32)
    cnt = cnt_out[0, :E].astype(jnp.int32)
    n_sub_max = (S * TOP_K) // MOE_SUB + E
    dest, meta = _moe_tables(e4, r4, cnt, n_sub_max)
    flat = dest.reshape(-1)
    tok = jnp.zeros((n_sub_max * MOE_SUB,), jnp.int32).at[flat].set(
        jnp.repeat(jnp.arange(S, dtype=jnp.int32), TOP_K))
    x_rows = x.reshape(S * ROW_TILES, LANES)
    y_rows = _experts(meta, tok, x_rows, w_gu, b_gu.reshape(DEPTH, E, 1, 2 * D_FF), w_dn,
                      b_dn.reshape(DEPTH, E, 1, D), layer)
    return _combine_ln(flat, y_rows, g_out, x, ln_g, ln_b)


def kernel(x, w_in, conv_w, conv_b, lru_wa, lru_ba, lru_wi, lru_bi, lru_lambda, cmp_pos, cmp_w1, cmp_w2,
           rel_bias, g_lru, g_nsa, w_out, ln1_g, ln1_b, w_router, b_router, w_gate_up, b_gate_up, w_down,
           b_down, ln2_g, ln2_b):
    B, S, D = x.shape
    assert B == 1
    x = x.reshape(S, D)
    xb = x.astype(BF16)
    n_gate = 3 * NSA_HEADS
    for l in range(DEPTH):
        u_f32 = _project(xb, w_in, l, 0, 2 * LRU_WIDTH // 512, 512, F32)
        u_bf = _project(xb, w_in, l, 2 * LRU_WIDTH // 512, (MAIN_COLS - 2 * LRU_WIDTH) // 512, 512, BF16)
        w_gate = jnp.pad(w_in[l, :, MAIN_COLS:], ((0, 0), (0, LANES - n_gate)))[None]
        gate_logits = _project(xb, w_gate, 0, 0, 1, LANES, F32)
        y_lru = _lru_group(u_f32, conv_w[l], conv_b[l], lru_wa[l], lru_ba[l], lru_wi[l], lru_bi[l],
                           lru_lambda[l], g_lru[l])
        y_nsa = _nsa_group(u_bf, gate_logits, cmp_pos[l], cmp_w1[l], cmp_w2[l], rel_bias, g_nsa[l])
        x, xb = _out_proj_ln(y_lru, y_nsa, w_out[l].astype(BF16), x, ln1_g[l], ln1_b[l])
        x, xb = _moe(x, w_router[l], b_router[l], w_gate_up, b_gate_up, w_down, b_down,
                     ln2_g[l], ln2_b[l], l)
    return x.reshape(B, S, D)
```

```python
import functools
import math

import numpy as np
import jax
import jax.numpy as jnp
from jax import lax
from jax.experimental import pallas as pl
from jax.experimental.pallas import tpu as pltpu

D_MODEL = 2048
DEPTH = 2
LRU_WIDTH = 1024
LRU_BLOCKS = 8
LRU_BLOCK_W = LRU_WIDTH // LRU_BLOCKS
CONV_WIDTH = 4
LRU_C = 8.0
NSA_HEADS = 8
NSA_KV_HEADS = 2
HEADS_PER_GROUP = NSA_HEADS // NSA_KV_HEADS
HEAD_DIM = 128
NSA_WIDTH = NSA_HEADS * HEAD_DIM
KV_WIDTH = NSA_KV_HEADS * HEAD_DIM
CMP_BLOCK = 32
CMP_STRIDE = 16
SEL_BLOCK = 64
N_SEL = 16
WINDOW = 512
Q_BLOCK = 128
N_BUCKETS = 32
MAX_DISTANCE = 128
N_EXPERTS = 32
TOP_K = 4
D_FF = 2048
SWIGLU_LIMIT = 7.0
SWIGLU_ALPHA = 1.702
DN_ALPHA = (2 * DEPTH) ** 0.25
NORM_EPS = 1e-5
NEG_INF = -1e30
FORCED_SCORE = 1e9
ATTN_SCALE = HEAD_DIM ** -0.5

LOG2E = math.log2(math.e)
PAD_BLK = WINDOW // SEL_BLOCK
M_INIT = -1e9

LANES = 128
MIB = 1024 * 1024
MAIN_COLS = 2 * LRU_WIDTH + NSA_WIDTH + 6 * KV_WIDTH
BF16 = jnp.bfloat16
F32 = jnp.float32

MOE_SUB = 256
MOE_J = 5
MOE_FT = 512
MOE_NF = D_FF // MOE_FT
ROW_TILES = D_MODEL // LANES


def _bucket_thresholds():
    max_exact = N_BUCKETS // 2
    out = []
    for b in range(1, N_BUCKETS):
        if b <= max_exact:
            out.append(b)
            continue
        d = max_exact
        while True:
            v = math.log(d / max_exact) / math.log(MAX_DISTANCE / max_exact) * (N_BUCKETS - max_exact)
            assert abs(v - round(v)) > 1e-3 or d == max_exact
            if min(max_exact + int(v), N_BUCKETS - 1) >= b:
                break
            d += 1
        out.append(d)
    return tuple(out)


BUCKET_TH = _bucket_thresholds()


def _cparams(semantics, vmem_mib):
    return pltpu.CompilerParams(dimension_semantics=semantics, vmem_limit_bytes=vmem_mib * MIB)


def _nt_dot(a, b):
    return lax.dot_general(a, b, (((1,), (1,)), ((), ())), preferred_element_type=F32)


def _bias_of_dist(rb_ref, head, d):
    b = jnp.full(d.shape, rb_ref[0, head], F32)
    for k in range(1, N_BUCKETS):
        b = jnp.where(d >= BUCKET_TH[k - 1], rb_ref[k, head], b)
    return b


def _mm_kernel(x_ref, w_ref, o_ref, wb_ref):
    @pl.when(pl.program_id(1) == 0)
    def _():
        wb_ref[...] = w_ref[...].astype(BF16)

    o_ref[...] = jnp.dot(x_ref[...], wb_ref[...], preferred_element_type=F32).astype(o_ref.dtype)


def _project(xb, w3, layer, col_blk0, n_blks, tn, out_dtype, tm=512):
    S, D = xb.shape
    return pl.pallas_call(
        _mm_kernel,
        grid=(n_blks, S // tm),
        in_specs=[pl.BlockSpec((tm, D), lambda n, m: (m, 0)),
                  pl.BlockSpec((None, D, tn), lambda n, m: (layer, 0, col_blk0 + n))],
        out_specs=pl.BlockSpec((tm, tn), lambda n, m: (m, n)),
        out_shape=jax.ShapeDtypeStruct((S, n_blks * tn), out_dtype),
        scratch_shapes=[pltpu.VMEM((D, tn), BF16)],
        compiler_params=_cparams(("arbitrary", "arbitrary"), 40),
        name="in_proj",
    )(xb, w3)


def _gelu_tanh(x):
    return 0.5 * x * (1.0 + jnp.tanh(math.sqrt(2.0 / math.pi) * (x + 0.044715 * (x * x * x))))


def _lru_kernel(xr_ref, xg_ref, cw_ref, cb_ref, wa_ref, ba_ref, wi_ref, bi_ref, lam_ref, g_ref,
                o_ref, xbuf, hc):
    i = pl.program_id(0)
    T = xr_ref.shape[0]
    W = xr_ref.shape[1]
    HALO = 8

    @pl.when(i == 0)
    def _():
        xbuf[0:HALO, :] = jnp.zeros((HALO, W), F32)
        hc[...] = jnp.zeros(hc.shape, F32)

    @pl.when(i > 0)
    def _():
        xbuf[0:HALO, :] = xbuf[T:T + HALO, :]

    xbuf[HALO:HALO + T, :] = xr_ref[...]
    cw = cw_ref[...]
    xc = cb_ref[...] + xbuf[HALO - 3:HALO - 3 + T, :] * cw[0:1]
    for k in range(1, CONV_WIDTH):
        xc = xc + xbuf[HALO - 3 + k:HALO - 3 + k + T, :] * cw[k:k + 1]

    xcb = xc.astype(BF16)
    ra, ia = [], []
    for n in range(LRU_BLOCKS):
        blk = xcb[:, n * LRU_BLOCK_W:(n + 1) * LRU_BLOCK_W]
        ra.append(jnp.dot(blk, wa_ref[n].astype(BF16), preferred_element_type=F32))
        ia.append(jnp.dot(blk, wi_ref[n].astype(BF16), preferred_element_type=F32))
    r = jax.nn.sigmoid(jnp.concatenate(ra, axis=1) + ba_ref[...])
    ig = jax.nn.sigmoid(jnp.concatenate(ia, axis=1) + bi_ref[...])

    z = -lam_ref[...]
    softplus = jnp.maximum(z, 0.0) + jnp.log(1.0 + jnp.exp(-jnp.abs(z)))
    log_a = -LRU_C * r * softplus
    a = jnp.exp(log_a)
    u = jnp.sqrt(1.0 - a * a) * (ig * xc)

    row = lax.broadcasted_iota(jnp.int32, (T, W), 0)
    A, U = a, u
    d = 1
    while d < T:
        keep = row >= d
        a_sh = jnp.where(keep, pltpu.roll(A, d, 0), 1.0)
        u_sh = jnp.where(keep, pltpu.roll(U, d, 0), 0.0)
        U = A * u_sh + U
        A = A * a_sh
        d *= 2
    h = U + A * hc[...]
    hc[...] = h[T - 1:T, :]

    y = h * _gelu_tanh(xg_ref[...])
    ms = jnp.mean(y * y, axis=-1, keepdims=True)
    o_ref[...] = (y * lax.rsqrt(ms + NORM_EPS) * g_ref[...]).astype(o_ref.dtype)


def _lru_group(u_f32, conv_w, conv_b, wa, ba, wi, bi, lam, g_lru, T=256):
    S = u_f32.shape[0]
    W = LRU_WIDTH
    row = lambda v: v.reshape(1, W)
    vec = pl.BlockSpec((1, W), lambda i: (0, 0))
    mat = pl.BlockSpec((LRU_BLOCKS, LRU_BLOCK_W, LRU_BLOCK_W), lambda i: (0, 0, 0))
    return pl.pallas_call(
        _lru_kernel,
        grid=(S // T,),
        in_specs=[pl.BlockSpec((T, W), lambda i: (i, 0)),
                  pl.BlockSpec((T, W), lambda i: (i, 1)),
                  pl.BlockSpec((CONV_WIDTH, W), lambda i: (0, 0)),
                  vec, mat, vec, mat, vec, vec, vec],
        out_specs=pl.BlockSpec((T, W), lambda i: (i, 0)),
        out_shape=jax.ShapeDtypeStruct((S, W), BF16),
        scratch_shapes=[pltpu.VMEM((T + 8, W), F32), pltpu.VMEM((1, W), F32)],
        compiler_params=_cparams(("arbitrary",), 48),
        name="rg_lru",
    )(u_f32, u_f32, conv_w, row(conv_b), wa, row(ba), wi, row(bi), row(lam), row(g_lru))


def _compress_kernel(x_ref, pos_ref, w1_ref, w2_ref, o_ref):
    NJ = x_ref.shape[0]
    half = (CMP_BLOCK // 2) * HEAD_DIM
    x = x_ref[...].astype(F32)
    a = (x + pos_ref[0:1, :]).astype(BF16)
    b = (x + pos_ref[1:2, :]).astype(BF16)
    y1 = jnp.dot(a, w1_ref[0:half, :].astype(BF16), preferred_element_type=F32)
    y2 = jnp.dot(b, w1_ref[half:2 * half, :].astype(BF16), preferred_element_type=F32)
    hmid = y1 + pltpu.roll(y2, NJ - 1, 0)
    o_ref[...] = jnp.dot(_gelu_tanh(hmid).astype(BF16), w2_ref[...].astype(BF16),
                         preferred_element_type=F32).astype(o_ref.dtype)


def _compress(kv_rows, pos, w1, w2):
    _, NJ, RW = kv_rows.shape
    return pl.pallas_call(
        _compress_kernel,
        grid=(4,),
        in_specs=[pl.BlockSpec((None, NJ, RW), lambda a: (a, 0, 0)),
                  pl.BlockSpec((None, 2, RW), lambda a: (a // 2, 0, 0)),
                  pl.BlockSpec((None, CMP_BLOCK * HEAD_DIM, HEAD_DIM), lambda a: (a // 2, 0, 0)),
                  pl.BlockSpec((None, HEAD_DIM, HEAD_DIM), lambda a: (a // 2, 0, 0))],
        out_specs=pl.BlockSpec((None, NJ, HEAD_DIM), lambda a: (a, 0, 0)),
        out_shape=jax.ShapeDtypeStruct((4, NJ, HEAD_DIM), BF16),
        compiler_params=_cparams(("arbitrary",), 32),
        name="kv_compress",
    )(kv_rows, pos.reshape(2, 2, RW), w1, w2)


def _cmp_attn_kernel(rb_ref, q_ref, kcv_ref, oc_ref, sel_ref, tcat):
    qb = pl.program_id(0)
    t0 = qb * Q_BLOCK
    NJ = kcv_ref.shape[1]
    NB = NJ * CMP_STRIDE // SEL_BLOCK
    NBP = sel_ref.shape[2]
    n_sel = min(N_SEL, NB)
    LOCAL = 2 * Q_BLOCK // CMP_STRIDE

    @pl.when(qb == 0)
    def _():
        i = lax.broadcasted_iota(jnp.int32, (Q_BLOCK, LANES), 0)
        j = lax.broadcasted_iota(jnp.int32, (Q_BLOCK, LANES), 1)
        d = jnp.maximum(i - CMP_STRIDE * j + (Q_BLOCK - CMP_STRIDE), 0)
        for h in range(NSA_HEADS):
            corr = jnp.where(j < LOCAL, (_bias_of_dist(rb_ref, h, d) - rb_ref[N_BUCKETS - 1, h]) * LOG2E, 0.0)
            hi = corr.astype(BF16)
            tcat[h, :, 0:LANES] = hi
            tcat[h, :, LANES:2 * LANES] = (corr - hi.astype(F32)).astype(BF16)

    row = lax.broadcasted_iota(jnp.int32, (Q_BLOCK, NJ), 0)
    col = lax.broadcasted_iota(jnp.int32, (Q_BLOCK, NJ), 1)
    mask_add = jnp.where(CMP_STRIDE * col + (CMP_BLOCK - 1) <= t0 + row, 0.0, NEG_INF)
    jj = lax.broadcasted_iota(jnp.int32, (2 * LANES, NJ), 0) % LANES
    cc = lax.broadcasted_iota(jnp.int32, (2 * LANES, NJ), 1)
    place = (cc == (Q_BLOCK // CMP_STRIDE) * (qb - 1) + jj).astype(F32).astype(BF16)
    bj = lax.broadcasted_iota(jnp.int32, (NB, NJ), 0)
    bc = lax.broadcasted_iota(jnp.int32, (NB, NJ), 1)
    ratio = SEL_BLOCK // CMP_STRIDE
    overlap_t = ((bc <= ratio * bj + ratio - 1) & (bc >= ratio * bj - 1)).astype(F32).astype(BF16)

    sj = lax.broadcasted_iota(jnp.int32, (NB, Q_BLOCK), 0)
    st = t0 + lax.broadcasted_iota(jnp.int32, (NB, Q_BLOCK), 1)
    cur = st // SEL_BLOCK
    blk_ok = sj <= cur
    forced = (sj == 0) | (sj == cur) | (sj == cur - 1)

    for g in range(NSA_KV_HEADS):
        kc = kcv_ref[g]
        vc = kcv_ref[NSA_KV_HEADS + g]
        psum = jnp.zeros((Q_BLOCK, NJ), F32)
        for h in range(HEADS_PER_GROUP):
            hh = g * HEADS_PER_GROUP + h
            qh = (q_ref[:, hh * HEAD_DIM:(hh + 1) * HEAD_DIM].astype(F32) * (ATTN_SCALE * LOG2E)).astype(BF16)
            z = _nt_dot(qh, kc) + jnp.dot(tcat[hh], place, preferred_element_type=F32) + mask_add
            m = jnp.maximum(jnp.max(z, axis=-1, keepdims=True), M_INIT)
            e = jnp.exp2(z - m)
            l = jnp.sum(e, axis=-1, keepdims=True)
            p = e * (1.0 / jnp.where(l > 0.0, l, 1.0))
            oc_ref[:, hh * HEAD_DIM:(hh + 1) * HEAD_DIM] = jnp.dot(
                p.astype(BF16), vc, preferred_element_type=F32)
            psum = psum + p
        p_hi = psum.astype(BF16)
        p_lo = (psum - p_hi.astype(F32)).astype(BF16)
        imp_t = _nt_dot(overlap_t, p_hi) + _nt_dot(overlap_t, p_lo)
        score = jnp.where(blk_ok, jnp.where(forced, FORCED_SCORE, imp_t), NEG_INF)
        chosen = jnp.zeros((NB, Q_BLOCK), jnp.bool_)
        sjf = sj.astype(F32)
        for _ in range(n_sel):
            mx = jnp.max(score, axis=0, keepdims=True)
            first = jnp.min(jnp.where(score == mx, sjf, float(NB)), axis=0, keepdims=True)
            pick = sjf == first
            chosen = chosen | pick
            score = jnp.where(pick, -3e38, score)
        pen_t = jnp.where(chosen & blk_ok, 0.0, NEG_INF)
        padded = jnp.concatenate([jnp.full((PAD_BLK, Q_BLOCK), NEG_INF, F32), pen_t,
                                  jnp.zeros((NBP - PAD_BLK - NB, Q_BLOCK), F32)], axis=0)
        pen = jnp.concatenate([padded[r:r + LANES].T for r in range(0, NBP, LANES)], axis=1)
        sel_ref[g] = pen.astype(sel_ref.dtype)


def _cmp_attention(rel_bias, u_bf, kcv):
    S = u_bf.shape[0]
    NJ = kcv.shape[1]
    NB = -(-(S // SEL_BLOCK + PAD_BLK) // LANES) * LANES
    return pl.pallas_call(
        _cmp_attn_kernel,
        grid=(S // Q_BLOCK,),
        in_specs=[pl.BlockSpec(memory_space=pltpu.SMEM),
                  pl.BlockSpec((Q_BLOCK, NSA_WIDTH), lambda i: (i, 0)),
                  pl.BlockSpec((4, NJ, HEAD_DIM), lambda i: (0, 0, 0))],
        out_specs=[pl.BlockSpec((Q_BLOCK, NSA_WIDTH), lambda i: (i, 0)),
                   pl.BlockSpec((NSA_KV_HEADS, Q_BLOCK, NB), lambda i: (0, i, 0))],
        out_shape=[jax.ShapeDtypeStruct((S, NSA_WIDTH), F32),
                   jax.ShapeDtypeStruct((NSA_KV_HEADS, S, NB), BF16)],
        scratch_shapes=[pltpu.VMEM((NSA_HEADS, Q_BLOCK, 2 * LANES), BF16)],
        compiler_params=_cparams(("arbitrary",), 32),
        name="nsa_compressed",
    )(rel_bias, u_bf, kcv)


def _sel_attn_kernel(rb_ref, q_ref, ks_ref, vs_ref, kw_ref, vw_ref, sel_ref, oc_ref, gl_ref, g_ref,
                     o_ref, tbl_s, tbl_w, yacc):
    qb = pl.program_id(0)
    Q = Q_BLOCK
    HPG = HEADS_PER_GROUP
    TW = WINDOW + Q
    FAR = 512
    NBP = sel_ref.shape[2]
    t0 = qb * Q

    @pl.when(qb == 0)
    def _():
        i = lax.broadcasted_iota(jnp.int32, (Q, TW), 0)
        c = lax.broadcasted_iota(jnp.int32, (Q, TW), 1)
        d = i - c + WINDOW
        dd = jnp.maximum(d, 0)
        for h in range(NSA_HEADS):
            corr = (_bias_of_dist(rb_ref, h, dd) - rb_ref[N_BUCKETS - 1, h]) * LOG2E
            tbl_s[h] = jnp.where(d < 0, NEG_INF, corr)
            tbl_w[h] = jnp.where((d < 0) | (d >= WINDOW), NEG_INF, corr)

    gates = jax.nn.sigmoid(gl_ref[...])
    col_w = lax.broadcasted_iota(jnp.int32, (1, TW), 1)
    before_start = jnp.where(col_w < WINDOW - t0, NEG_INF, 0.0)
    p_tail = pl.multiple_of(t0, Q)

    G = NSA_KV_HEADS
    q4 = [jnp.concatenate(
        [(q_ref[:, (g * HPG + h) * HEAD_DIM:(g * HPG + h + 1) * HEAD_DIM].astype(F32)
          * (ATTN_SCALE * LOG2E)).astype(BF16) for h in range(HPG)], axis=0) for g in range(G)]

    def attend(g, k_ref, v_ref, p0, width, add, carry):
        m, l, acc = carry
        cols = slice(g * HEAD_DIM, (g + 1) * HEAD_DIM)
        kk = k_ref[pl.ds(p0, width), cols]
        vv = v_ref[pl.ds(p0, width), cols]
        s = _nt_dot(q4[g], kk)
        z = (s.reshape(HPG, Q, width) + add).reshape(HPG * Q, width)
        m_new = jnp.maximum(m, jnp.max(z, axis=-1, keepdims=True))
        alpha = jnp.exp2(m - m_new)
        e = jnp.exp2(z - m_new)
        l = alpha * l + jnp.sum(e, axis=-1, keepdims=True)
        acc = alpha * acc + jnp.dot(e.astype(BF16), vv, preferred_element_type=F32)
        return m_new, l, acc

    def block_mask(g, p0, width):
        bp = lax.broadcasted_iota(jnp.int32, (NBP, width), 0)
        kb = lax.broadcasted_iota(jnp.int32, (NBP, width), 1) // SEL_BLOCK
        expand = (bp - p0 // SEL_BLOCK == kb).astype(F32).astype(BF16)
        return jnp.dot(sel_ref[g], expand, preferred_element_type=F32)[None]

    init = (jnp.full((HPG * Q, 1), M_INIT, F32), jnp.zeros((HPG * Q, 1), F32),
            jnp.zeros((HPG * Q, HEAD_DIM), F32))

    def far_body(i, carry):
        p0 = pl.multiple_of((qb - (FAR // Q) * (i + 1)) * Q, Q)
        return tuple(attend(g, ks_ref, vs_ref, p0, FAR, block_mask(g, p0, FAR), carry[g]) for g in range(G))

    far = lax.fori_loop(0, jnp.maximum(qb - 1, 0) // (FAR // Q), far_body, (init,) * G)

    for g in range(G):
        heads = slice(g * HPG, (g + 1) * HPG)
        m_s, l_s, acc_s = attend(g, ks_ref, vs_ref, p_tail, TW, tbl_s[heads] + block_mask(g, p_tail, TW), far[g])
        o_s = acc_s / jnp.where(l_s > 0.0, l_s, 1.0)

        m_w, l_w, acc_w = attend(g, kw_ref, vw_ref, p_tail, TW, tbl_w[heads] + before_start[None], init)
        o_w = acc_w / jnp.where(l_w > 0.0, l_w, 1.0)

        for h in range(HPG):
            hh = g * HPG + h
            hc = slice(hh * HEAD_DIM, (hh + 1) * HEAD_DIM)
            rows = slice(h * Q, (h + 1) * Q)
            yacc[:, hc] = (gates[:, 3 * hh:3 * hh + 1] * oc_ref[:, hc]
                           + gates[:, 3 * hh + 1:3 * hh + 2] * o_s[rows]
                           + gates[:, 3 * hh + 2:3 * hh + 3] * o_w[rows])

    y = yacc[...]
    ms = jnp.mean(y * y, axis=-1, keepdims=True)
    o_ref[...] = (y * lax.rsqrt(ms + NORM_EPS) * g_ref[...]).astype(o_ref.dtype)


def _sel_attention(rel_bias, u_bf, kv_pad, sel, o_c, gate_logits, g_nsa):
    S = u_bf.shape[0]
    NB = sel.shape[2]

    def kv_spec(j):
        return pl.BlockSpec((WINDOW + S, KV_WIDTH), lambda i: (0, j))

    return pl.pallas_call(
        _sel_attn_kernel,
        grid=(S // Q_BLOCK,),
        in_specs=[pl.BlockSpec(memory_space=pltpu.SMEM),
                  pl.BlockSpec((Q_BLOCK, NSA_WIDTH), lambda i: (i, 0)),
                  kv_spec(0), kv_spec(1), kv_spec(2), kv_spec(3),
                  pl.BlockSpec((NSA_KV_HEADS, Q_BLOCK, NB), lambda i: (0, i, 0)),
                  pl.BlockSpec((Q_BLOCK, NSA_WIDTH), lambda i: (i, 0)),
                  pl.BlockSpec((Q_BLOCK, LANES), lambda i: (i, 0)),
                  pl.BlockSpec((1, NSA_WIDTH), lambda i: (0, 0))],
        out_specs=pl.BlockSpec((Q_BLOCK, NSA_WIDTH), lambda i: (i, 0)),
        out_shape=jax.ShapeDtypeStruct((S, NSA_WIDTH), BF16),
        scratch_shapes=[pltpu.VMEM((NSA_HEADS, Q_BLOCK, WINDOW + Q_BLOCK), F32),
                        pltpu.VMEM((NSA_HEADS, Q_BLOCK, WINDOW + Q_BLOCK), F32),
                        pltpu.VMEM((Q_BLOCK, NSA_WIDTH), F32)],
        compiler_params=_cparams(("arbitrary",), 60),
        name="nsa_selected_window",
    )(rel_bias, u_bf, kv_pad, kv_pad, kv_pad, kv_pad, sel, o_c, gate_logits, g_nsa.reshape(1, NSA_WIDTH))


def _layer_norm_rows(z, g, b):
    mu = jnp.mean(z, axis=-1, keepdims=True)
    zc = z - mu
    var = jnp.mean(zc * zc, axis=-1, keepdims=True)
    return zc * lax.rsqrt(var + NORM_EPS) * g + b


def _outproj_kernel(yl_ref, yn_ref, w_ref, x_ref, g_ref, b_ref, o_ref, or_ref):
    tm = x_ref.shape[0]
    acc = jnp.dot(yl_ref[...], w_ref[0:LRU_WIDTH, :], preferred_element_type=F32)
    acc = acc + jnp.dot(yn_ref[...], w_ref[LRU_WIDTH:LRU_WIDTH + NSA_WIDTH, :], preferred_element_type=F32)
    y = _layer_norm_rows(DN_ALPHA * x_ref[...] + acc, g_ref[...], b_ref[...])
    o_ref[...] = y
    for c in range(ROW_TILES):
        or_ref[pl.ds(c, tm, stride=ROW_TILES), :] = y[:, c * LANES:(c + 1) * LANES]


def _out_proj_ln(y_lru, y_nsa, w_out_b, x, g, b, tm=256):
    S, D = x.shape
    vec = pl.BlockSpec((1, D), lambda i: (0, 0))
    return pl.pallas_call(
        _outproj_kernel,
        grid=(S // tm,),
        in_specs=[pl.BlockSpec((tm, LRU_WIDTH), lambda i: (i, 0)),
                  pl.BlockSpec((tm, NSA_WIDTH), lambda i: (i, 0)),
                  pl.BlockSpec((LRU_WIDTH + NSA_WIDTH, D), lambda i: (0, 0)),
                  pl.BlockSpec((tm, D), lambda i: (i, 0)), vec, vec],
        out_specs=[pl.BlockSpec((tm, D), lambda i: (i, 0)),
                   pl.BlockSpec((tm * ROW_TILES, LANES), lambda i: (i, 0))],
        out_shape=[jax.ShapeDtypeStruct((S, D), F32), jax.ShapeDtypeStruct((S * ROW_TILES, LANES), F32)],
        compiler_params=_cparams(("arbitrary",), 48),
        name="out_proj_ln",
    )(y_lru, y_nsa, w_out_b, x, g.reshape(1, D), b.reshape(1, D))


def _router_kernel(x_ref, w_ref, b_ref, e_ref, g_ref, r_ref, cnt_ref, carry):
    i = pl.program_id(0)
    tm = x_ref.shape[0]

    @pl.when(i == 0)
    def _():
        carry[...] = jnp.zeros(carry.shape, F32)

    logits = jnp.dot(x_ref[...], w_ref[...], preferred_element_type=F32,
                     precision=lax.Precision.HIGHEST) + b_ref[...]
    lane = lax.broadcasted_iota(jnp.int32, (tm, LANES), 1)
    lanef = lane.astype(F32)
    s = logits
    picks = []
    chosen = jnp.zeros((tm, LANES), jnp.bool_)
    for _ in range(TOP_K):
        mx = jnp.max(s, axis=-1, keepdims=True)
        first = jnp.min(jnp.where(s == mx, lanef, float(LANES)), axis=-1, keepdims=True)
        pick = lanef == first
        picks.append((pick, first))
        chosen = chosen | pick
        s = jnp.where(pick, -3e38, s)
    top = jnp.max(logits, axis=-1, keepdims=True)
    ex = jnp.where(chosen, jnp.exp(logits - top), 0.0)
    gate = ex / jnp.sum(ex, axis=-1, keepdims=True)

    onehot = jnp.where(chosen, 1.0, 0.0)
    rr = lax.broadcasted_iota(jnp.int32, (tm, tm), 0)
    rc = lax.broadcasted_iota(jnp.int32, (tm, tm), 1)
    before = (rr > rc).astype(F32).astype(BF16)
    rank = jnp.dot(before, onehot.astype(BF16), preferred_element_type=F32) + carry[0:1, :]
    carry[0:1, :] = carry[0:1, :] + jnp.sum(onehot, axis=0, keepdims=True)
    cnt_ref[...] = carry[...]

    e_out = jnp.zeros((tm, LANES), jnp.int32)
    g_out = jnp.zeros((tm, LANES), F32)
    r_out = jnp.zeros((tm, LANES), F32)
    for k, (pick, first) in enumerate(picks):
        e_out = jnp.where(lane == k, first.astype(jnp.int32), e_out)
        g_out = jnp.where(lane == k, jnp.sum(jnp.where(pick, gate, 0.0), axis=-1, keepdims=True), g_out)
        r_out = jnp.where(lane == k, jnp.sum(jnp.where(pick, rank, 0.0), axis=-1, keepdims=True), r_out)
    e_ref[...] = e_out
    g_ref[...] = g_out
    r_ref[...] = r_out


def _router(x, w_pad, b_pad, tm=256):
    S, D = x.shape
    out = pl.BlockSpec((tm, LANES), lambda i: (i, 0))
    return pl.pallas_call(
        _router_kernel,
        grid=(S // tm,),
        in_specs=[pl.BlockSpec((tm, D), lambda i: (i, 0)),
                  pl.BlockSpec((D, LANES), lambda i: (0, 0)),
                  pl.BlockSpec((1, LANES), lambda i: (0, 0))],
        out_specs=[out, out, out, pl.BlockSpec((8, LANES), lambda i: (0, 0))],
        out_shape=[jax.ShapeDtypeStruct((S, LANES), jnp.int32), jax.ShapeDtypeStruct((S, LANES), F32),
                   jax.ShapeDtypeStruct((S, LANES), F32), jax.ShapeDtypeStruct((8, LANES), F32)],
        scratch_shapes=[pltpu.VMEM((8, LANES), F32)],
        compiler_params=_cparams(("arbitrary",), 32),
        name="moe_router",
    )(x, w_pad, b_pad)


def _expert_kernel(sb_ref, se_ref, sf_ref, sj_ref, so_ref, sv_ref, tok_ref,
                   x_hbm, wgu_ref, bgu_ref, wdn_ref, bdn_ref, o_ref, wgu_b, wdn_b, acc, stage, xrows, sem):
    s = pl.program_id(0)
    n_steps = pl.num_programs(0)
    valid = sv_ref[s] == 1
    f = sf_ref[s]
    j = sj_ref[s]
    SUB = MOE_SUB
    FT2 = wgu_ref.shape[1]
    RT = ROW_TILES

    def gather_rows(step):
        base = sb_ref[step] * SUB
        slot = sj_ref[step] % 2

        def issue(r, carry):
            t = tok_ref[base + r]
            pltpu.make_async_copy(x_hbm.at[pl.ds(pl.multiple_of(t * RT, RT), RT), :],
                                  stage.at[slot, pl.ds(pl.multiple_of(r * RT, RT), RT), :],
                                  sem.at[slot]).start()
            return carry

        lax.fori_loop(0, SUB, issue, 0, unroll=8)

    @pl.when(s == 0)
    def _():
        gather_rows(0)

    nxt = jnp.minimum(s + 1, n_steps - 1)

    @pl.when((s + 1 < n_steps) & (sv_ref[nxt] == 1) & (sf_ref[nxt] == 0))
    def _():
        gather_rows(nxt)

    rows = pl.ds(pl.multiple_of(j * SUB, SUB), SUB)

    @pl.when(valid & (f == 0))
    def _():
        slot = j % 2
        pltpu.make_async_copy(x_hbm.at[pl.ds(0, SUB * RT), :], stage.at[slot], sem.at[slot]).wait()
        for c in range(RT):
            xrows[rows, c * LANES:(c + 1) * LANES] = stage[slot, pl.ds(c, SUB, stride=RT), :].astype(BF16)

    @pl.when(valid & (j == 0))
    def _():
        wgu_b[...] = wgu_ref[...].astype(BF16)
        wdn_b[...] = wdn_ref[...].astype(BF16)

    @pl.when(jnp.logical_not(valid))
    def _():
        o_ref[...] = jnp.zeros(o_ref.shape, F32)

    @pl.when(valid)
    def _():
        gu = jnp.dot(xrows[rows, :], wgu_b[...], preferred_element_type=F32) + bgu_ref[...]
        gate = jnp.minimum(gu, SWIGLU_LIMIT)
        act = gate * jax.nn.sigmoid(SWIGLU_ALPHA * gate)
        up1 = jnp.clip(gu, -SWIGLU_LIMIT, SWIGLU_LIMIT) + 1.0
        pr = lax.broadcasted_iota(jnp.int32, (2 * LANES, LANES), 0)
        pc = lax.broadcasted_iota(jnp.int32, (2 * LANES, LANES), 1)
        pick_even = (pr == 2 * pc).astype(F32).astype(BF16)
        hs = []
        for c in range(FT2 // (2 * LANES)):
            parts = []
            for k in range(2):
                cs = slice((2 * c + k) * LANES, (2 * c + k + 1) * LANES)
                parts.append((act[:, cs] * pltpu.roll(up1[:, cs], LANES - 1, 1)).astype(BF16))
            pair = jnp.concatenate(parts, axis=1)
            hs.append(jnp.dot(pair, pick_even, preferred_element_type=F32).astype(BF16))
        h = jnp.concatenate(hs, axis=1)
        y = jnp.dot(h, wdn_b[...], preferred_element_type=F32)

        @pl.when(f == 0)
        def _():
            acc[rows, :] = y + bdn_ref[...]

        @pl.when((f > 0) & (f < MOE_NF - 1))
        def _():
            acc[rows, :] = acc[rows, :] + y

        @pl.when(f == MOE_NF - 1)
        def _():
            yfin = acc[rows, :] + y
            for c in range(RT):
                o_ref[pl.ds(c, SUB, stride=RT), :] = yfin[:, c * LANES:(c + 1) * LANES]


def _experts(meta, tok, x_rows, w_gu, b_gu3, w_dn, b_dn3, layer):
    sb, se, sf, sj, so, sv = meta
    NS = sb.shape[0]
    R = tok.shape[0]
    D = D_MODEL
    RT = ROW_TILES

    def wmap(fn):
        return lambda s, sb, se, sf, sj, so, sv, tok: fn(s, se, sf, so)

    grid_spec = pltpu.PrefetchScalarGridSpec(
        num_scalar_prefetch=7,
        grid=(NS,),
        in_specs=[
            pl.BlockSpec(memory_space=pl.ANY),
            pl.BlockSpec((None, None, D, 2 * MOE_FT), wmap(lambda s, se, sf, so: (layer, se[s], 0, sf[s]))),
            pl.BlockSpec((None, None, 1, 2 * MOE_FT), wmap(lambda s, se, sf, so: (layer, se[s], 0, sf[s]))),
            pl.BlockSpec((None, None, MOE_FT, D), wmap(lambda s, se, sf, so: (layer, se[s], sf[s], 0))),
            pl.BlockSpec((None, None, 1, D), wmap(lambda s, se, sf, so: (layer, se[s], 0, 0))),
        ],
        out_specs=pl.BlockSpec((MOE_SUB * RT, LANES), wmap(lambda s, se, sf, so: (so[s], 0))),
        scratch_shapes=[pltpu.VMEM((D, 2 * MOE_FT), BF16), pltpu.VMEM((MOE_FT, D), BF16),
                        pltpu.VMEM((MOE_J * MOE_SUB, D), F32),
                        pltpu.VMEM((2, MOE_SUB * RT, LANES), F32),
                        pltpu.VMEM((MOE_J * MOE_SUB, D), BF16),
                        pltpu.SemaphoreType.DMA((2,))],
    )
    return pl.pallas_call(
        _expert_kernel,
        grid_spec=grid_spec,
        out_shape=jax.ShapeDtypeStruct((R * RT, LANES), F32),
        compiler_params=_cparams(("arbitrary",), 60),
        name="moe_experts",
    )(sb, se, sf, sj, so, sv, tok, x_rows, w_gu, b_gu3, w_dn, b_dn3)


def _moe_tables(e4, r4, cnt, n_sub_max):
    E = N_EXPERTS
    nsub = (cnt + MOE_SUB - 1) // MOE_SUB
    sub_end = jnp.cumsum(nsub)
    sub_start = sub_end - nsub
    n_blk = sub_end[-1]
    dest = sub_start[e4] * MOE_SUB + r4

    b = jnp.arange(n_sub_max, dtype=jnp.int32)
    be = jnp.minimum(jnp.sum(b[:, None] >= sub_end[None, :], axis=1), E - 1).astype(jnp.int32)
    lb = b - sub_start[be]
    jb = lb % MOE_J
    b0 = b - jb
    nj = jnp.minimum(MOE_J, nsub[be] - (lb // MOE_J) * MOE_J)
    NS = MOE_NF * n_sub_max
    bb = jnp.repeat(b, MOE_NF)
    ff = jnp.tile(jnp.arange(MOE_NF, dtype=jnp.int32), n_sub_max)
    step = jnp.where(bb < n_blk, MOE_NF * b0[bb] + ff * nj[bb] + jb[bb], NS)
    step_b = jnp.zeros((NS,), jnp.int32).at[step].set(bb, mode='drop')
    step_f = jnp.zeros((NS,), jnp.int32).at[step].set(ff, mode='drop')
    sidx = jnp.arange(NS, dtype=jnp.int32)
    n_steps = MOE_NF * n_blk
    valid = sidx < n_steps
    last = jnp.maximum(n_steps - 1, 0)
    step_b = jnp.where(valid, step_b, step_b[last])
    step_f = jnp.where(valid, step_f, step_f[last])
    out_b = jnp.where(step_f == MOE_NF - 1, step_b, b0[step_b])
    out_b = jnp.where(valid, out_b, jnp.minimum(n_blk + (sidx - n_steps), n_sub_max - 1))
    meta = (step_b, be[step_b], step_f, jb[step_b], out_b, valid.astype(jnp.int32))

    dest = dest.astype(jnp.int32)
    n_tok = e4.shape[0]
    _, tok_sorted = lax.sort_key_val(dest.reshape(-1), jnp.repeat(jnp.arange(n_tok, dtype=jnp.int32), TOP_K))
    row_e = jnp.repeat(be, MOE_SUB)
    local = jnp.arange(n_sub_max * MOE_SUB, dtype=jnp.int32) - sub_start[row_e] * MOE_SUB
    first = jnp.cumsum(cnt) - cnt
    tok = jnp.where(local < cnt[row_e], tok_sorted[jnp.clip(first[row_e] + local, 0, n_tok * TOP_K - 1)], 0)
    return dest, tok.astype(jnp.int32), meta


def _combine_kernel(dest_ref, y_hbm, g4_ref, x_ref, g_ref, b_ref, o_ref, ob_ref, stage, ytile, ysum, sem):
    i = pl.program_id(0)
    n = pl.num_programs(0)
    tm = x_ref.shape[0]
    RT = ROW_TILES
    n_rows = tm * TOP_K

    def gather_rows(step, slot):
        base = step * n_rows

        def issue(r, carry):
            d = dest_ref[base + r]
            pltpu.make_async_copy(y_hbm.at[pl.ds(pl.multiple_of(d * RT, RT), RT), :],
                                  stage.at[slot, pl.ds(pl.multiple_of(r * RT, RT), RT), :],
                                  sem.at[slot]).start()
            return carry

        lax.fori_loop(0, n_rows, issue, 0, unroll=8)

    @pl.when(i == 0)
    def _():
        gather_rows(0, 0)

    @pl.when(i + 1 < n)
    def _():
        gather_rows(i + 1, (i + 1) % 2)

    slot = i % 2
    pltpu.make_async_copy(y_hbm.at[pl.ds(0, n_rows * RT), :], stage.at[slot], sem.at[slot]).wait()

    def token_sum(t, carry):
        r0 = pl.multiple_of(t * (TOP_K * RT), TOP_K * RT)
        y = g4_ref[TOP_K * t] * stage[slot, pl.ds(r0, RT), :]
        for k in range(1, TOP_K):
            y = y + g4_ref[TOP_K * t + k] * stage[slot, pl.ds(r0 + k * RT, RT), :]
        ytile[pl.ds(pl.multiple_of(t * RT, RT), RT), :] = y
        return carry

    lax.fori_loop(0, tm, token_sum, 0, unroll=4)
    for c in range(RT):
        ysum[:, c * LANES:(c + 1) * LANES] = ytile[pl.ds(c, tm, stride=RT), :]
    out = _layer_norm_rows(DN_ALPHA * x_ref[...] + ysum[...], g_ref[...], b_ref[...])
    o_ref[...] = out
    ob_ref[...] = out.astype(BF16)


def _combine_ln(dest_flat, y_rows, g4, x, g, b, tm=128):
    S, D = x.shape
    vec = pl.BlockSpec((1, D), lambda i, d: (0, 0))
    grid_spec = pltpu.PrefetchScalarGridSpec(
        num_scalar_prefetch=1,
        grid=(S // tm,),
        in_specs=[pl.BlockSpec(memory_space=pl.ANY),
                  pl.BlockSpec((tm * TOP_K,), lambda i, d: (i,), memory_space=pltpu.SMEM),
                  pl.BlockSpec((tm, D), lambda i, d: (i, 0)), vec, vec],
        out_specs=[pl.BlockSpec((tm, D), lambda i, d: (i, 0)), pl.BlockSpec((tm, D), lambda i, d: (i, 0))],
        scratch_shapes=[pltpu.VMEM((2, tm * TOP_K * ROW_TILES, LANES), F32),
                        pltpu.VMEM((tm * ROW_TILES, LANES), F32),
                        pltpu.VMEM((tm, D), F32),
                        pltpu.SemaphoreType.DMA((2,))],
    )
    return pl.pallas_call(
        _combine_kernel,
        grid_spec=grid_spec,
        out_shape=[jax.ShapeDtypeStruct((S, D), F32), jax.ShapeDtypeStruct((S, D), BF16)],
        compiler_params=_cparams(("arbitrary",), 48),
        name="moe_combine_ln",
    )(dest_flat, y_rows, g4, x, g.reshape(1, D), b.reshape(1, D))


def _nsa_group(u_bf, gate_logits, cmp_pos, cmp_w1, cmp_w2, rel_bias, g_nsa):
    S = u_bf.shape[0]
    G = NSA_KV_HEADS
    kvc = u_bf[:, NSA_WIDTH:NSA_WIDTH + 2 * KV_WIDTH]
    kv_rows = kvc.reshape(S, 2 * G, HEAD_DIM).transpose(1, 0, 2).reshape(2 * G, S // CMP_STRIDE,
                                                                          CMP_STRIDE * HEAD_DIM)
    kcv = _compress(kv_rows, cmp_pos, cmp_w1, cmp_w2)
    o_c, sel = _cmp_attention(rel_bias, u_bf, kcv)
    kv_pad = jnp.pad(u_bf[:, NSA_WIDTH + 2 * KV_WIDTH:], ((WINDOW, 0), (0, 0)))
    return _sel_attention(rel_bias, u_bf, kv_pad, sel, o_c, gate_logits, g_nsa)


def _moe(x, x_rows, w_router, b_router, w_gu, b_gu, w_dn, b_dn, ln_g, ln_b, layer):
    S, D = x.shape
    E = N_EXPERTS
    w_pad = jnp.pad(w_router, ((0, 0), (0, LANES - E)))
    b_pad = jnp.pad(b_router, (0, LANES - E), constant_values=NEG_INF).reshape(1, LANES)
    e_out, g_out, r_out, cnt_out = _router(x, w_pad, b_pad)
    e4 = e_out[:, :TOP_K]
    r4 = r_out[:, :TOP_K].astype(jnp.int32)
    cnt = cnt_out[0, :E].astype(jnp.int32)
    n_sub_max = (S * TOP_K) // MOE_SUB + E
    dest, tok, meta = _moe_tables(e4, r4, cnt, n_sub_max)
    flat = dest.reshape(-1)
    y_rows = _experts(meta, tok, x_rows, w_gu, b_gu.reshape(DEPTH, E, 1, 2 * D_FF), w_dn,
                      b_dn.reshape(DEPTH, E, 1, D), layer)
    return _combine_ln(flat, y_rows, g_out[:, :TOP_K].reshape(-1), x, ln_g, ln_b)


def kernel(x, w_in, conv_w, conv_b, lru_wa, lru_ba, lru_wi, lru_bi, lru_lambda, cmp_pos, cmp_w1, cmp_w2,
           rel_bias, g_lru, g_nsa, w_out, ln1_g, ln1_b, w_router, b_router, w_gate_up, b_gate_up, w_down,
           b_down, ln2_g, ln2_b):
    B, S, D = x.shape
    assert B == 1
    x = x.reshape(S, D)
    xb = x.astype(BF16)
    n_gate = 3 * NSA_HEADS
    for l in range(DEPTH):
        u_f32 = _project(xb, w_in, l, 0, 2 * LRU_WIDTH // 512, 512, F32)
        u_bf = _project(xb, w_in, l, 2 * LRU_WIDTH // 512, (MAIN_COLS - 2 * LRU_WIDTH) // 512, 512, BF16)
        w_gate = jnp.pad(w_in[l, :, MAIN_COLS:], ((0, 0), (0, LANES - n_gate)))[None]
        gate_logits = _project(xb, w_gate, 0, 0, 1, LANES, F32)
        y_lru = _lru_group(u_f32, conv_w[l], conv_b[l], lru_wa[l], lru_ba[l], lru_wi[l], lru_bi[l],
                           lru_lambda[l], g_lru[l])
        y_nsa = _nsa_group(u_bf, gate_logits, cmp_pos[l], cmp_w1[l], cmp_w2[l], rel_bias, g_nsa[l])
        x, x_rows = _out_proj_ln(y_lru, y_nsa, w_out[l].astype(BF16), x, ln1_g[l], ln1_b[l])
        x, xb = _moe(x, x_rows, w_router[l], b_router[l], w_gate_up, b_gate_up, w_down, b_down,
                     ln2_g[l], ln2_b[l], l)
    return x.reshape(B, S, D)
```

```python
import functools
import math

import numpy as np
import jax
import jax.numpy as jnp
from jax import lax
from jax.experimental import pallas as pl
from jax.experimental.pallas import tpu as pltpu

D_MODEL = 2048
DEPTH = 2
LRU_WIDTH = 1024
LRU_BLOCKS = 8
LRU_BLOCK_W = LRU_WIDTH // LRU_BLOCKS
CONV_WIDTH = 4
LRU_C = 8.0
NSA_HEADS = 8
NSA_KV_HEADS = 2
HEADS_PER_GROUP = NSA_HEADS // NSA_KV_HEADS
HEAD_DIM = 128
NSA_WIDTH = NSA_HEADS * HEAD_DIM
KV_WIDTH = NSA_KV_HEADS * HEAD_DIM
CMP_BLOCK = 32
CMP_STRIDE = 16
SEL_BLOCK = 64
N_SEL = 16
WINDOW = 512
Q_BLOCK = 128
N_BUCKETS = 32
MAX_DISTANCE = 128
N_EXPERTS = 32
TOP_K = 4
D_FF = 2048
SWIGLU_LIMIT = 7.0
SWIGLU_ALPHA = 1.702
DN_ALPHA = (2 * DEPTH) ** 0.25
NORM_EPS = 1e-5
NEG_INF = -1e30
FORCED_SCORE = 1e9
ATTN_SCALE = HEAD_DIM ** -0.5

LOG2E = math.log2(math.e)
PAD_BLK = WINDOW // SEL_BLOCK
M_INIT = -1e9

LANES = 128
MIB = 1024 * 1024
MAIN_COLS = 2 * LRU_WIDTH + NSA_WIDTH + 6 * KV_WIDTH
BF16 = jnp.bfloat16
F32 = jnp.float32

MOE_SUB = 256
MOE_ROWQ = 64
MOE_J = 5
MOE_FT = 512
MOE_NF = D_FF // MOE_FT
ROW_TILES = D_MODEL // LANES


def _bucket_thresholds():
    max_exact = N_BUCKETS // 2
    out = []
    for b in range(1, N_BUCKETS):
        if b <= max_exact:
            out.append(b)
            continue
        d = max_exact
        while True:
            v = math.log(d / max_exact) / math.log(MAX_DISTANCE / max_exact) * (N_BUCKETS - max_exact)
            assert abs(v - round(v)) > 1e-3 or d == max_exact
            if min(max_exact + int(v), N_BUCKETS - 1) >= b:
                break
            d += 1
        out.append(d)
    return tuple(out)


BUCKET_TH = _bucket_thresholds()


def _cparams(semantics, vmem_mib):
    return pltpu.CompilerParams(dimension_semantics=semantics, vmem_limit_bytes=vmem_mib * MIB)


def _nt_dot(a, b):
    return lax.dot_general(a, b, (((1,), (1,)), ((), ())), preferred_element_type=F32)


def _bias_of_dist(rb_ref, head, d):
    b = jnp.full(d.shape, rb_ref[0, head], F32)
    for k in range(1, N_BUCKETS):
        b = jnp.where(d >= BUCKET_TH[k - 1], rb_ref[k, head], b)
    return b


def _mm_kernel(x_ref, w_ref, o_ref, wb_ref):
    @pl.when(pl.program_id(1) == 0)
    def _():
        wb_ref[...] = w_ref[...].astype(BF16)

    o_ref[...] = jnp.dot(x_ref[...], wb_ref[...], preferred_element_type=F32).astype(o_ref.dtype)


def _project(xb, w3, layer, col_blk0, n_blks, tn, out_dtype, tm=1024):
    S, D = xb.shape
    return pl.pallas_call(
        _mm_kernel,
        grid=(n_blks, S // tm),
        in_specs=[pl.BlockSpec((tm, D), lambda n, m: (m, 0)),
                  pl.BlockSpec((None, D, tn), lambda n, m: (layer, 0, col_blk0 + n))],
        out_specs=pl.BlockSpec((tm, tn), lambda n, m: (m, n)),
        out_shape=jax.ShapeDtypeStruct((S, n_blks * tn), out_dtype),
        scratch_shapes=[pltpu.VMEM((D, tn), BF16)],
        compiler_params=_cparams(("arbitrary", "arbitrary"), 40),
        name="in_proj",
    )(xb, w3)


def _gelu_tanh(x):
    return 0.5 * x * (1.0 + jnp.tanh(math.sqrt(2.0 / math.pi) * (x + 0.044715 * (x * x * x))))


def _lru_kernel(xr_ref, xg_ref, cw_ref, cb_ref, wa_ref, ba_ref, wi_ref, bi_ref, lam_ref, g_ref,
                o_ref, xbuf, hc):
    i = pl.program_id(0)
    T = xr_ref.shape[0]
    W = xr_ref.shape[1]
    HALO = 8

    @pl.when(i == 0)
    def _():
        xbuf[0:HALO, :] = jnp.zeros((HALO, W), F32)
        hc[...] = jnp.zeros(hc.shape, F32)

    @pl.when(i > 0)
    def _():
        xbuf[0:HALO, :] = xbuf[T:T + HALO, :]

    xbuf[HALO:HALO + T, :] = xr_ref[...]
    cw = cw_ref[...]
    xc = cb_ref[...] + xbuf[HALO - 3:HALO - 3 + T, :] * cw[0:1]
    for k in range(1, CONV_WIDTH):
        xc = xc + xbuf[HALO - 3 + k:HALO - 3 + k + T, :] * cw[k:k + 1]

    xcb = xc.astype(BF16)
    ra, ia = [], []
    for n in range(LRU_BLOCKS):
        blk = xcb[:, n * LRU_BLOCK_W:(n + 1) * LRU_BLOCK_W]
        ra.append(jnp.dot(blk, wa_ref[n].astype(BF16), preferred_element_type=F32))
        ia.append(jnp.dot(blk, wi_ref[n].astype(BF16), preferred_element_type=F32))
    r = jax.nn.sigmoid(jnp.concatenate(ra, axis=1) + ba_ref[...])
    ig = jax.nn.sigmoid(jnp.concatenate(ia, axis=1) + bi_ref[...])

    z = -lam_ref[...]
    softplus = jnp.maximum(z, 0.0) + jnp.log(1.0 + jnp.exp(-jnp.abs(z)))
    log_a = -LRU_C * r * softplus
    a = jnp.exp(log_a)
    u = jnp.sqrt(1.0 - a * a) * (ig * xc)

    row = lax.broadcasted_iota(jnp.int32, (T, W), 0)
    A, U = a, u
    d = 1
    while d < T:
        keep = row >= d
        a_sh = jnp.where(keep, pltpu.roll(A, d, 0), 1.0)
        u_sh = jnp.where(keep, pltpu.roll(U, d, 0), 0.0)
        U = A * u_sh + U
        A = A * a_sh
        d *= 2
    h = U + A * hc[...]
    hc[...] = h[T - 1:T, :]

    y = h * _gelu_tanh(xg_ref[...])
    ms = jnp.mean(y * y, axis=-1, keepdims=True)
    o_ref[...] = (y * lax.rsqrt(ms + NORM_EPS) * g_ref[...]).astype(o_ref.dtype)


def _lru_group(u_f32, conv_w, conv_b, wa, ba, wi, bi, lam, g_lru, T=256):
    S = u_f32.shape[0]
    W = LRU_WIDTH
    row = lambda v: v.reshape(1, W)
    vec = pl.BlockSpec((1, W), lambda i: (0, 0))
    mat = pl.BlockSpec((LRU_BLOCKS, LRU_BLOCK_W, LRU_BLOCK_W), lambda i: (0, 0, 0))
    return pl.pallas_call(
        _lru_kernel,
        grid=(S // T,),
        in_specs=[pl.BlockSpec((T, W), lambda i: (i, 0)),
                  pl.BlockSpec((T, W), lambda i: (i, 1)),
                  pl.BlockSpec((CONV_WIDTH, W), lambda i: (0, 0)),
                  vec, mat, vec, mat, vec, vec, vec],
        out_specs=pl.BlockSpec((T, W), lambda i: (i, 0)),
        out_shape=jax.ShapeDtypeStruct((S, W), BF16),
        scratch_shapes=[pltpu.VMEM((T + 8, W), F32), pltpu.VMEM((1, W), F32)],
        compiler_params=_cparams(("arbitrary",), 48),
        name="rg_lru",
    )(u_f32, u_f32, conv_w, row(conv_b), wa, row(ba), wi, row(bi), row(lam), row(g_lru))


def _compress_kernel(x_ref, pos_ref, w1_ref, w2_ref, o_ref):
    NJ = x_ref.shape[0]
    half = (CMP_BLOCK // 2) * HEAD_DIM
    x = x_ref[...].astype(F32)
    a = (x + pos_ref[0:1, :]).astype(BF16)
    b = (x + pos_ref[1:2, :]).astype(BF16)
    y1 = jnp.dot(a, w1_ref[0:half, :].astype(BF16), preferred_element_type=F32)
    y2 = jnp.dot(b, w1_ref[half:2 * half, :].astype(BF16), preferred_element_type=F32)
    hmid = y1 + pltpu.roll(y2, NJ - 1, 0)
    o_ref[...] = jnp.dot(_gelu_tanh(hmid).astype(BF16), w2_ref[...].astype(BF16),
                         preferred_element_type=F32).astype(o_ref.dtype)


def _compress(kv_rows, pos, w1, w2):
    _, NJ, RW = kv_rows.shape
    return pl.pallas_call(
        _compress_kernel,
        grid=(4,),
        in_specs=[pl.BlockSpec((None, NJ, RW), lambda a: (a, 0, 0)),
                  pl.BlockSpec((None, 2, RW), lambda a: (a // 2, 0, 0)),
                  pl.BlockSpec((None, CMP_BLOCK * HEAD_DIM, HEAD_DIM), lambda a: (a // 2, 0, 0)),
                  pl.BlockSpec((None, HEAD_DIM, HEAD_DIM), lambda a: (a // 2, 0, 0))],
        out_specs=pl.BlockSpec((None, NJ, HEAD_DIM), lambda a: (a, 0, 0)),
        out_shape=jax.ShapeDtypeStruct((4, NJ, HEAD_DIM), BF16),
        compiler_params=_cparams(("arbitrary",), 32),
        name="kv_compress",
    )(kv_rows, pos.reshape(2, 2, RW), w1, w2)


def _cmp_attn_kernel(rb_ref, q_ref, kcv_ref, oc_ref, sel_ref, tcat):
    qb = pl.program_id(0)
    t0 = qb * Q_BLOCK
    NJ = kcv_ref.shape[1]
    NB = NJ * CMP_STRIDE // SEL_BLOCK
    NBP = sel_ref.shape[2]
    n_sel = min(N_SEL, NB)
    LOCAL = 2 * Q_BLOCK // CMP_STRIDE

    @pl.when(qb == 0)
    def _():
        i = lax.broadcasted_iota(jnp.int32, (Q_BLOCK, LANES), 0)
        j = lax.broadcasted_iota(jnp.int32, (Q_BLOCK, LANES), 1)
        d = jnp.maximum(i - CMP_STRIDE * j + (Q_BLOCK - CMP_STRIDE), 0)
        for h in range(NSA_HEADS):
            corr = jnp.where(j < LOCAL, (_bias_of_dist(rb_ref, h, d) - rb_ref[N_BUCKETS - 1, h]) * LOG2E, 0.0)
            hi = corr.astype(BF16)
            tcat[h, :, 0:LANES] = hi
            tcat[h, :, LANES:2 * LANES] = (corr - hi.astype(F32)).astype(BF16)

    row = lax.broadcasted_iota(jnp.int32, (Q_BLOCK, NJ), 0)
    col = lax.broadcasted_iota(jnp.int32, (Q_BLOCK, NJ), 1)
    mask_add = jnp.where(CMP_STRIDE * col + (CMP_BLOCK - 1) <= t0 + row, 0.0, NEG_INF)
    jj = lax.broadcasted_iota(jnp.int32, (2 * LANES, NJ), 0) % LANES
    cc = lax.broadcasted_iota(jnp.int32, (2 * LANES, NJ), 1)
    place = (cc == (Q_BLOCK // CMP_STRIDE) * (qb - 1) + jj).astype(F32).astype(BF16)
    bj = lax.broadcasted_iota(jnp.int32, (NB, NJ), 0)
    bc = lax.broadcasted_iota(jnp.int32, (NB, NJ), 1)
    ratio = SEL_BLOCK // CMP_STRIDE
    overlap_t = ((bc <= ratio * bj + ratio - 1) & (bc >= ratio * bj - 1)).astype(F32).astype(BF16)

    sj = lax.broadcasted_iota(jnp.int32, (NB, Q_BLOCK), 0)
    st = t0 + lax.broadcasted_iota(jnp.int32, (NB, Q_BLOCK), 1)
    cur = st // SEL_BLOCK
    blk_ok = sj <= cur
    forced = (sj == 0) | (sj == cur) | (sj == cur - 1)

    for g in range(NSA_KV_HEADS):
        kc = kcv_ref[g]
        vc = kcv_ref[NSA_KV_HEADS + g]
        psum = jnp.zeros((Q_BLOCK, NJ), F32)
        for h in range(HEADS_PER_GROUP):
            hh = g * HEADS_PER_GROUP + h
            qh = (q_ref[:, hh * HEAD_DIM:(hh + 1) * HEAD_DIM].astype(F32) * (ATTN_SCALE * LOG2E)).astype(BF16)
            z = _nt_dot(qh, kc) + jnp.dot(tcat[hh], place, preferred_element_type=F32) + mask_add
            m = jnp.maximum(jnp.max(z, axis=-1, keepdims=True), M_INIT)
            e = jnp.exp2(z - m)
            l = jnp.sum(e, axis=-1, keepdims=True)
            p = e * (1.0 / jnp.where(l > 0.0, l, 1.0))
            oc_ref[:, hh * HEAD_DIM:(hh + 1) * HEAD_DIM] = jnp.dot(
                p.astype(BF16), vc, preferred_element_type=F32)
            psum = psum + p
        p_hi = psum.astype(BF16)
        p_lo = (psum - p_hi.astype(F32)).astype(BF16)
        imp_t = _nt_dot(overlap_t, p_hi) + _nt_dot(overlap_t, p_lo)
        score = jnp.where(blk_ok, jnp.where(forced, FORCED_SCORE, imp_t), NEG_INF)
        chosen = jnp.zeros((NB, Q_BLOCK), jnp.bool_)
        sjf = sj.astype(F32)
        for _ in range(n_sel):
            mx = jnp.max(score, axis=0, keepdims=True)
            first = jnp.min(jnp.where(score == mx, sjf, float(NB)), axis=0, keepdims=True)
            pick = sjf == first
            chosen = chosen | pick
            score = jnp.where(pick, -3e38, score)
        pen_t = jnp.where(chosen & blk_ok, 0.0, NEG_INF)
        padded = jnp.concatenate([jnp.full((PAD_BLK, Q_BLOCK), NEG_INF, F32), pen_t,
                                  jnp.zeros((NBP - PAD_BLK - NB, Q_BLOCK), F32)], axis=0)
        pen = jnp.concatenate([padded[r:r + LANES].T for r in range(0, NBP, LANES)], axis=1)
        sel_ref[g] = pen.astype(sel_ref.dtype)


def _cmp_attention(rel_bias, u_bf, kcv):
    S = u_bf.shape[0]
    NJ = kcv.shape[1]
    NB = -(-(S // SEL_BLOCK + PAD_BLK) // LANES) * LANES
    return pl.pallas_call(
        _cmp_attn_kernel,
        grid=(S // Q_BLOCK,),
        in_specs=[pl.BlockSpec(memory_space=pltpu.SMEM),
                  pl.BlockSpec((Q_BLOCK, NSA_WIDTH), lambda i: (i, 0)),
                  pl.BlockSpec((4, NJ, HEAD_DIM), lambda i: (0, 0, 0))],
        out_specs=[pl.BlockSpec((Q_BLOCK, NSA_WIDTH), lambda i: (i, 0)),
                   pl.BlockSpec((NSA_KV_HEADS, Q_BLOCK, NB), lambda i: (0, i, 0))],
        out_shape=[jax.ShapeDtypeStruct((S, NSA_WIDTH), F32),
                   jax.ShapeDtypeStruct((NSA_KV_HEADS, S, NB), BF16)],
        scratch_shapes=[pltpu.VMEM((NSA_HEADS, Q_BLOCK, 2 * LANES), BF16)],
        compiler_params=_cparams(("arbitrary",), 32),
        name="nsa_compressed",
    )(rel_bias, u_bf, kcv)


def _sel_attn_kernel(rb_ref, q_ref, ks_ref, vs_ref, kw_ref, vw_ref, sel_ref, oc_ref, gl_ref, g_ref,
                     o_ref, tbl_s, tbl_w, yacc):
    qb = pl.program_id(0)
    Q = Q_BLOCK
    HPG = HEADS_PER_GROUP
    TW = WINDOW + Q
    FAR = 512
    NBP = sel_ref.shape[2]
    t0 = qb * Q

    @pl.when(qb == 0)
    def _():
        i = lax.broadcasted_iota(jnp.int32, (Q, TW), 0)
        c = lax.broadcasted_iota(jnp.int32, (Q, TW), 1)
        d = i - c + WINDOW
        dd = jnp.maximum(d, 0)
        for h in range(NSA_HEADS):
            corr = (_bias_of_dist(rb_ref, h, dd) - rb_ref[N_BUCKETS - 1, h]) * LOG2E
            tbl_s[h] = jnp.where(d < 0, NEG_INF, corr)
            tbl_w[h] = jnp.where((d < 0) | (d >= WINDOW), NEG_INF, corr)

    gates = jax.nn.sigmoid(gl_ref[...])
    col_w = lax.broadcasted_iota(jnp.int32, (1, TW), 1)
    before_start = jnp.where(col_w < WINDOW - t0, NEG_INF, 0.0)
    p_tail = pl.multiple_of(t0, Q)

    G = NSA_KV_HEADS
    q4 = [jnp.concatenate(
        [(q_ref[:, (g * HPG + h) * HEAD_DIM:(g * HPG + h + 1) * HEAD_DIM].astype(F32)
          * (ATTN_SCALE * LOG2E)).astype(BF16) for h in range(HPG)], axis=0) for g in range(G)]

    def attend(g, k_ref, v_ref, p0, width, add, carry):
        m, l, acc = carry
        cols = slice(g * HEAD_DIM, (g + 1) * HEAD_DIM)
        kk = k_ref[pl.ds(p0, width), cols]
        vv = v_ref[pl.ds(p0, width), cols]
        s = _nt_dot(q4[g], kk)
        z = (s.reshape(HPG, Q, width) + add).reshape(HPG * Q, width)
        m_new = jnp.maximum(m, jnp.max(z, axis=-1, keepdims=True))
        alpha = jnp.exp2(m - m_new)
        e = jnp.exp2(z - m_new)
        l = alpha * l + jnp.sum(e, axis=-1, keepdims=True)
        acc = alpha * acc + jnp.dot(e.astype(BF16), vv, preferred_element_type=F32)
        return m_new, l, acc

    def block_mask(g, p0, width):
        bp = lax.broadcasted_iota(jnp.int32, (NBP, width), 0)
        kb = lax.broadcasted_iota(jnp.int32, (NBP, width), 1) // SEL_BLOCK
        expand = (bp - p0 // SEL_BLOCK == kb).astype(F32).astype(BF16)
        return jnp.dot(sel_ref[g], expand, preferred_element_type=F32)[None]

    init = (jnp.full((HPG * Q, 1), M_INIT, F32), jnp.zeros((HPG * Q, 1), F32),
            jnp.zeros((HPG * Q, HEAD_DIM), F32))

    def far_body(i, carry):
        p0 = pl.multiple_of((qb - (FAR // Q) * (i + 1)) * Q, Q)
        return tuple(attend(g, ks_ref, vs_ref, p0, FAR, block_mask(g, p0, FAR), carry[g]) for g in range(G))

    far = lax.fori_loop(0, jnp.maximum(qb - 1, 0) // (FAR // Q), far_body, (init,) * G)

    for g in range(G):
        heads = slice(g * HPG, (g + 1) * HPG)
        m_s, l_s, acc_s = attend(g, ks_ref, vs_ref, p_tail, TW, tbl_s[heads] + block_mask(g, p_tail, TW), far[g])
        o_s = acc_s / jnp.where(l_s > 0.0, l_s, 1.0)

        m_w, l_w, acc_w = attend(g, kw_ref, vw_ref, p_tail, TW, tbl_w[heads] + before_start[None], init)
        o_w = acc_w / jnp.where(l_w > 0.0, l_w, 1.0)

        for h in range(HPG):
            hh = g * HPG + h
            hc = slice(hh * HEAD_DIM, (hh + 1) * HEAD_DIM)
            rows = slice(h * Q, (h + 1) * Q)
            yacc[:, hc] = (gates[:, 3 * hh:3 * hh + 1] * oc_ref[:, hc]
                           + gates[:, 3 * hh + 1:3 * hh + 2] * o_s[rows]
                           + gates[:, 3 * hh + 2:3 * hh + 3] * o_w[rows])

    y = yacc[...]
    ms = jnp.mean(y * y, axis=-1, keepdims=True)
    o_ref[...] = (y * lax.rsqrt(ms + NORM_EPS) * g_ref[...]).astype(o_ref.dtype)


def _sel_attention(rel_bias, u_bf, kv_pad, sel, o_c, gate_logits, g_nsa):
    S = u_bf.shape[0]
    NB = sel.shape[2]

    def kv_spec(j):
        return pl.BlockSpec((WINDOW + S, KV_WIDTH), lambda i: (0, j))

    return pl.pallas_call(
        _sel_attn_kernel,
        grid=(S // Q_BLOCK,),
        in_specs=[pl.BlockSpec(memory_space=pltpu.SMEM),
                  pl.BlockSpec((Q_BLOCK, NSA_WIDTH), lambda i: (i, 0)),
                  kv_spec(0), kv_spec(1), kv_spec(2), kv_spec(3),
                  pl.BlockSpec((NSA_KV_HEADS, Q_BLOCK, NB), lambda i: (0, i, 0)),
                  pl.BlockSpec((Q_BLOCK, NSA_WIDTH), lambda i: (i, 0)),
                  pl.BlockSpec((Q_BLOCK, LANES), lambda i: (i, 0)),
                  pl.BlockSpec((1, NSA_WIDTH), lambda i: (0, 0))],
        out_specs=pl.BlockSpec((Q_BLOCK, NSA_WIDTH), lambda i: (i, 0)),
        out_shape=jax.ShapeDtypeStruct((S, NSA_WIDTH), BF16),
        scratch_shapes=[pltpu.VMEM((NSA_HEADS, Q_BLOCK, WINDOW + Q_BLOCK), F32),
                        pltpu.VMEM((NSA_HEADS, Q_BLOCK, WINDOW + Q_BLOCK), F32),
                        pltpu.VMEM((Q_BLOCK, NSA_WIDTH), F32)],
        compiler_params=_cparams(("arbitrary",), 60),
        name="nsa_selected_window",
    )(rel_bias, u_bf, kv_pad, kv_pad, kv_pad, kv_pad, sel, o_c, gate_logits, g_nsa.reshape(1, NSA_WIDTH))


def _layer_norm_rows(z, g, b):
    mu = jnp.mean(z, axis=-1, keepdims=True)
    zc = z - mu
    var = jnp.mean(zc * zc, axis=-1, keepdims=True)
    return zc * lax.rsqrt(var + NORM_EPS) * g + b


def _outproj_kernel(yl_ref, yn_ref, w_ref, x_ref, g_ref, b_ref, o_ref, or_ref):
    tm = x_ref.shape[0]
    acc = jnp.dot(yl_ref[...], w_ref[0:LRU_WIDTH, :], preferred_element_type=F32)
    acc = acc + jnp.dot(yn_ref[...], w_ref[LRU_WIDTH:LRU_WIDTH + NSA_WIDTH, :], preferred_element_type=F32)
    y = _layer_norm_rows(DN_ALPHA * x_ref[...] + acc, g_ref[...], b_ref[...])
    o_ref[...] = y
    for c in range(ROW_TILES):
        or_ref[pl.ds(c, tm, stride=ROW_TILES), :] = y[:, c * LANES:(c + 1) * LANES]


def _out_proj_ln(y_lru, y_nsa, w_out_b, x, g, b, tm=256):
    S, D = x.shape
    vec = pl.BlockSpec((1, D), lambda i: (0, 0))
    return pl.pallas_call(
        _outproj_kernel,
        grid=(S // tm,),
        in_specs=[pl.BlockSpec((tm, LRU_WIDTH), lambda i: (i, 0)),
                  pl.BlockSpec((tm, NSA_WIDTH), lambda i: (i, 0)),
                  pl.BlockSpec((LRU_WIDTH + NSA_WIDTH, D), lambda i: (0, 0)),
                  pl.BlockSpec((tm, D), lambda i: (i, 0)), vec, vec],
        out_specs=[pl.BlockSpec((tm, D), lambda i: (i, 0)),
                   pl.BlockSpec((tm * ROW_TILES, LANES), lambda i: (i, 0))],
        out_shape=[jax.ShapeDtypeStruct((S, D), F32), jax.ShapeDtypeStruct((S * ROW_TILES, LANES), F32)],
        compiler_params=_cparams(("arbitrary",), 48),
        name="out_proj_ln",
    )(y_lru, y_nsa, w_out_b, x, g.reshape(1, D), b.reshape(1, D))


def _router_kernel(x_ref, w_ref, b_ref, e_ref, g_ref, r_ref, cnt_ref, carry):
    i = pl.program_id(0)
    tm = x_ref.shape[0]

    @pl.when(i == 0)
    def _():
        carry[...] = jnp.zeros(carry.shape, F32)

    logits = jnp.dot(x_ref[...], w_ref[...], preferred_element_type=F32,
                     precision=lax.Precision.HIGHEST) + b_ref[...]
    lane = lax.broadcasted_iota(jnp.int32, (tm, LANES), 1)
    lanef = lane.astype(F32)
    s = logits
    picks = []
    chosen = jnp.zeros((tm, LANES), jnp.bool_)
    for _ in range(TOP_K):
        mx = jnp.max(s, axis=-1, keepdims=True)
        first = jnp.min(jnp.where(s == mx, lanef, float(LANES)), axis=-1, keepdims=True)
        pick = lanef == first
        picks.append((pick, first))
        chosen = chosen | pick
        s = jnp.where(pick, -3e38, s)
    top = jnp.max(logits, axis=-1, keepdims=True)
    ex = jnp.where(chosen, jnp.exp(logits - top), 0.0)
    gate = ex / jnp.sum(ex, axis=-1, keepdims=True)

    onehot = jnp.where(chosen, 1.0, 0.0)
    rr = lax.broadcasted_iota(jnp.int32, (tm, tm), 0)
    rc = lax.broadcasted_iota(jnp.int32, (tm, tm), 1)
    before = (rr > rc).astype(F32).astype(BF16)
    rank = jnp.dot(before, onehot.astype(BF16), preferred_element_type=F32) + carry[0:1, :]
    carry[0:1, :] = carry[0:1, :] + jnp.sum(onehot, axis=0, keepdims=True)
    cnt_ref[...] = carry[...]

    e_out = jnp.zeros((tm, LANES), jnp.int32)
    g_out = jnp.zeros((tm, LANES), F32)
    r_out = jnp.zeros((tm, LANES), F32)
    for k, (pick, first) in enumerate(picks):
        e_out = jnp.where(lane == k, first.astype(jnp.int32), e_out)
        g_out = jnp.where(lane == k, jnp.sum(jnp.where(pick, gate, 0.0), axis=-1, keepdims=True), g_out)
        r_out = jnp.where(lane == k, jnp.sum(jnp.where(pick, rank, 0.0), axis=-1, keepdims=True), r_out)
    e_ref[...] = e_out
    g_ref[...] = g_out
    r_ref[...] = r_out


def _router(x, w_pad, b_pad, tm=256):
    S, D = x.shape
    out = pl.BlockSpec((tm, LANES), lambda i: (i, 0))
    return pl.pallas_call(
        _router_kernel,
        grid=(S // tm,),
        in_specs=[pl.BlockSpec((tm, D), lambda i: (i, 0)),
                  pl.BlockSpec((D, LANES), lambda i: (0, 0)),
                  pl.BlockSpec((1, LANES), lambda i: (0, 0))],
        out_specs=[out, out, out, pl.BlockSpec((8, LANES), lambda i: (0, 0))],
        out_shape=[jax.ShapeDtypeStruct((S, LANES), jnp.int32), jax.ShapeDtypeStruct((S, LANES), F32),
                   jax.ShapeDtypeStruct((S, LANES), F32), jax.ShapeDtypeStruct((8, LANES), F32)],
        scratch_shapes=[pltpu.VMEM((8, LANES), F32)],
        compiler_params=_cparams(("arbitrary",), 32),
        name="moe_router",
    )(x, w_pad, b_pad)


def _expert_kernel(sb_ref, se_ref, sf_ref, sj_ref, so_ref, sv_ref, tok_ref,
                   x_hbm, wgu_ref, bgu_ref, wdn_ref, bdn_ref, o_ref, wgu_b, wdn_b, acc, stage, xrows, sem):
    s = pl.program_id(0)
    n_steps = pl.num_programs(0)
    nrow = sv_ref[s]
    valid = nrow > 0
    f = sf_ref[s]
    j = sj_ref[s]
    SUB = MOE_SUB
    FT2 = wgu_ref.shape[1]
    RT = ROW_TILES
    UNROLL = 8

    def gather_rows(step):
        base = sb_ref[step] * SUB
        slot = sj_ref[step] % 2

        def issue(i, carry):
            for u in range(UNROLL):
                r = i * UNROLL + u
                t = tok_ref[base + r]
                pltpu.make_async_copy(x_hbm.at[pl.ds(pl.multiple_of(t * RT, RT), RT), :],
                                      stage.at[slot, pl.ds(pl.multiple_of(r * RT, RT), RT), :],
                                      sem.at[slot]).start()
            return carry

        lax.fori_loop(0, sv_ref[step] // UNROLL, issue, 0)

    @pl.when(s == 0)
    def _():
        gather_rows(0)

    nxt = jnp.minimum(s + 1, n_steps - 1)

    @pl.when((s + 1 < n_steps) & (sv_ref[nxt] > 0) & (sf_ref[nxt] == 0))
    def _():
        gather_rows(nxt)

    @pl.when(valid & (j == 0))
    def _():
        wgu_b[...] = wgu_ref[...].astype(BF16)
        wdn_b[...] = wdn_ref[...].astype(BF16)

    @pl.when(jnp.logical_not(valid))
    def _():
        o_ref[...] = jnp.zeros(o_ref.shape, F32)

    def process(m):
        rows = pl.ds(pl.multiple_of(j * SUB, SUB), m)

        @pl.when(f == 0)
        def _():
            slot = j % 2
            pltpu.make_async_copy(x_hbm.at[pl.ds(0, m * RT), :], stage.at[slot, pl.ds(0, m * RT), :],
                                  sem.at[slot]).wait()
            for c in range(RT):
                xrows[rows, c * LANES:(c + 1) * LANES] = stage[slot, pl.ds(c, m, stride=RT), :].astype(BF16)

        gu = jnp.dot(xrows[rows, :], wgu_b[...], preferred_element_type=F32) + bgu_ref[...]
        gate = jnp.minimum(gu, SWIGLU_LIMIT)
        act = gate * jax.nn.sigmoid(SWIGLU_ALPHA * gate)
        up1 = jnp.clip(gu, -SWIGLU_LIMIT, SWIGLU_LIMIT) + 1.0
        pr = lax.broadcasted_iota(jnp.int32, (2 * LANES, LANES), 0)
        pc = lax.broadcasted_iota(jnp.int32, (2 * LANES, LANES), 1)
        pick_even = (pr == 2 * pc).astype(F32).astype(BF16)
        hs = []
        for c in range(FT2 // (2 * LANES)):
            parts = []
            for k in range(2):
                cs = slice((2 * c + k) * LANES, (2 * c + k + 1) * LANES)
                parts.append((act[:, cs] * pltpu.roll(up1[:, cs], LANES - 1, 1)).astype(BF16))
            pair = jnp.concatenate(parts, axis=1)
            hs.append(jnp.dot(pair, pick_even, preferred_element_type=F32).astype(BF16))
        h = jnp.concatenate(hs, axis=1)
        y = jnp.dot(h, wdn_b[...], preferred_element_type=F32)

        @pl.when(f == 0)
        def _():
            acc[rows, :] = y + bdn_ref[...]

        @pl.when((f > 0) & (f < MOE_NF - 1))
        def _():
            acc[rows, :] = acc[rows, :] + y

        @pl.when(f == MOE_NF - 1)
        def _():
            yfin = acc[rows, :] + y
            for c in range(RT):
                o_ref[pl.ds(c, m, stride=RT), :] = yfin[:, c * LANES:(c + 1) * LANES]
            if m < SUB:
                o_ref[pl.ds(m * RT, (SUB - m) * RT), :] = jnp.zeros(((SUB - m) * RT, LANES), F32)

    for m in range(MOE_ROWQ, SUB + 1, MOE_ROWQ):
        pl.when(nrow == m)(functools.partial(process, m))


def _experts(meta, tok, x_rows, w_gu, b_gu3, w_dn, b_dn3, layer):
    sb, se, sf, sj, so, sv = meta
    NS = sb.shape[0]
    R = tok.shape[0]
    D = D_MODEL
    RT = ROW_TILES

    def wmap(fn):
        return lambda s, sb, se, sf, sj, so, sv, tok: fn(s, se, sf, so)

    grid_spec = pltpu.PrefetchScalarGridSpec(
        num_scalar_prefetch=7,
        grid=(NS,),
        in_specs=[
            pl.BlockSpec(memory_space=pl.ANY),
            pl.BlockSpec((None, None, D, 2 * MOE_FT), wmap(lambda s, se, sf, so: (layer, se[s], 0, sf[s]))),
            pl.BlockSpec((None, None, 1, 2 * MOE_FT), wmap(lambda s, se, sf, so: (layer, se[s], 0, sf[s]))),
            pl.BlockSpec((None, None, MOE_FT, D), wmap(lambda s, se, sf, so: (layer, se[s], sf[s], 0))),
            pl.BlockSpec((None, None, 1, D), wmap(lambda s, se, sf, so: (layer, se[s], 0, 0))),
        ],
        out_specs=pl.BlockSpec((MOE_SUB * RT, LANES), wmap(lambda s, se, sf, so: (so[s], 0))),
        scratch_shapes=[pltpu.VMEM((D, 2 * MOE_FT), BF16), pltpu.VMEM((MOE_FT, D), BF16),
                        pltpu.VMEM((MOE_J * MOE_SUB, D), F32),
                        pltpu.VMEM((2, MOE_SUB * RT, LANES), F32),
                        pltpu.VMEM((MOE_J * MOE_SUB, D), BF16),
                        pltpu.SemaphoreType.DMA((2,))],
    )
    return pl.pallas_call(
        _expert_kernel,
        grid_spec=grid_spec,
        out_shape=jax.ShapeDtypeStruct((R * RT, LANES), F32),
        compiler_params=_cparams(("arbitrary",), 60),
        name="moe_experts",
    )(sb, se, sf, sj, so, sv, tok, x_rows, w_gu, b_gu3, w_dn, b_dn3)


def _moe_tables(e4, r4, cnt, n_sub_max):
    E = N_EXPERTS
    nsub = (cnt + MOE_SUB - 1) // MOE_SUB
    sub_end = jnp.cumsum(nsub)
    sub_start = sub_end - nsub
    n_blk = sub_end[-1]
    dest = sub_start[e4] * MOE_SUB + r4

    b = jnp.arange(n_sub_max, dtype=jnp.int32)
    be = jnp.minimum(jnp.sum(b[:, None] >= sub_end[None, :], axis=1), E - 1).astype(jnp.int32)
    lb = b - sub_start[be]
    jb = lb % MOE_J
    b0 = b - jb
    nj = jnp.minimum(MOE_J, nsub[be] - (lb // MOE_J) * MOE_J)
    NS = MOE_NF * n_sub_max
    bb = jnp.repeat(b, MOE_NF)
    ff = jnp.tile(jnp.arange(MOE_NF, dtype=jnp.int32), n_sub_max)
    step = jnp.where(bb < n_blk, MOE_NF * b0[bb] + ff * nj[bb] + jb[bb], NS)
    step_b = jnp.zeros((NS,), jnp.int32).at[step].set(bb, mode='drop')
    step_f = jnp.zeros((NS,), jnp.int32).at[step].set(ff, mode='drop')
    sidx = jnp.arange(NS, dtype=jnp.int32)
    n_steps = MOE_NF * n_blk
    valid = sidx < n_steps
    last = jnp.maximum(n_steps - 1, 0)
    step_b = jnp.where(valid, step_b, step_b[last])
    step_f = jnp.where(valid, step_f, step_f[last])
    out_b = jnp.where(step_f == MOE_NF - 1, step_b, b0[step_b])
    out_b = jnp.where(valid, out_b, jnp.minimum(n_blk + (sidx - n_steps), n_sub_max - 1))
    rem = cnt[be] - lb * MOE_SUB
    rows_b = jnp.clip(-(-rem // MOE_ROWQ) * MOE_ROWQ, MOE_ROWQ, MOE_SUB)
    meta = (step_b, be[step_b], step_f, jb[step_b], out_b, jnp.where(valid, rows_b[step_b], 0).astype(jnp.int32))

    dest = dest.astype(jnp.int32)
    n_tok = e4.shape[0]
    _, tok_sorted = lax.sort_key_val(dest.reshape(-1), jnp.repeat(jnp.arange(n_tok, dtype=jnp.int32), TOP_K))
    row_e = jnp.repeat(be, MOE_SUB)
    local = jnp.arange(n_sub_max * MOE_SUB, dtype=jnp.int32) - sub_start[row_e] * MOE_SUB
    first = jnp.cumsum(cnt) - cnt
    tok = jnp.where(local < cnt[row_e], tok_sorted[jnp.clip(first[row_e] + local, 0, n_tok * TOP_K - 1)], 0)
    return dest, tok.astype(jnp.int32), meta


def _combine_kernel(dest_ref, y_hbm, g4_ref, x_ref, g_ref, b_ref, o_ref, ob_ref, stage, ytile, ysum, sem):
    i = pl.program_id(0)
    n = pl.num_programs(0)
    tm = x_ref.shape[0]
    RT = ROW_TILES
    n_rows = tm * TOP_K

    def gather_rows(step, slot):
        base = step * n_rows

        def issue(r, carry):
            d = dest_ref[base + r]
            pltpu.make_async_copy(y_hbm.at[pl.ds(pl.multiple_of(d * RT, RT), RT), :],
                                  stage.at[slot, pl.ds(pl.multiple_of(r * RT, RT), RT), :],
                                  sem.at[slot]).start()
            return carry

        lax.fori_loop(0, n_rows, issue, 0, unroll=8)

    @pl.when(i == 0)
    def _():
        gather_rows(0, 0)

    @pl.when(i + 1 < n)
    def _():
        gather_rows(i + 1, (i + 1) % 2)

    slot = i % 2
    pltpu.make_async_copy(y_hbm.at[pl.ds(0, n_rows * RT), :], stage.at[slot], sem.at[slot]).wait()

    def token_sum(t, carry):
        r0 = pl.multiple_of(t * (TOP_K * RT), TOP_K * RT)
        y = g4_ref[TOP_K * t] * stage[slot, pl.ds(r0, RT), :]
        for k in range(1, TOP_K):
            y = y + g4_ref[TOP_K * t + k] * stage[slot, pl.ds(r0 + k * RT, RT), :]
        ytile[pl.ds(pl.multiple_of(t * RT, RT), RT), :] = y
        return carry

    lax.fori_loop(0, tm, token_sum, 0, unroll=4)
    for c in range(RT):
        ysum[:, c * LANES:(c + 1) * LANES] = ytile[pl.ds(c, tm, stride=RT), :]
    out = _layer_norm_rows(DN_ALPHA * x_ref[...] + ysum[...], g_ref[...], b_ref[...])
    o_ref[...] = out
    ob_ref[...] = out.astype(BF16)


def _combine_ln(dest_flat, y_rows, g4, x, g, b, tm=128):
    S, D = x.shape
    vec = pl.BlockSpec((1, D), lambda i, d: (0, 0))
    grid_spec = pltpu.PrefetchScalarGridSpec(
        num_scalar_prefetch=1,
        grid=(S // tm,),
        in_specs=[pl.BlockSpec(memory_space=pl.ANY),
                  pl.BlockSpec((tm * TOP_K,), lambda i, d: (i,), memory_space=pltpu.SMEM),
                  pl.BlockSpec((tm, D), lambda i, d: (i, 0)), vec, vec],
        out_specs=[pl.BlockSpec((tm, D), lambda i, d: (i, 0)), pl.BlockSpec((tm, D), lambda i, d: (i, 0))],
        scratch_shapes=[pltpu.VMEM((2, tm * TOP_K * ROW_TILES, LANES), F32),
                        pltpu.VMEM((tm * ROW_TILES, LANES), F32),
                        pltpu.VMEM((tm, D), F32),
                        pltpu.SemaphoreType.DMA((2,))],
    )
    return pl.pallas_call(
        _combine_kernel,
        grid_spec=grid_spec,
        out_shape=[jax.ShapeDtypeStruct((S, D), F32), jax.ShapeDtypeStruct((S, D), BF16)],
        compiler_params=_cparams(("arbitrary",), 48),
        name="moe_combine_ln",
    )(dest_flat, y_rows, g4, x, g.reshape(1, D), b.reshape(1, D))


def _nsa_group(u_bf, gate_logits, cmp_pos, cmp_w1, cmp_w2, rel_bias, g_nsa):
    S = u_bf.shape[0]
    G = NSA_KV_HEADS
    kvc = u_bf[:, NSA_WIDTH:NSA_WIDTH + 2 * KV_WIDTH]
    kv_rows = kvc.reshape(S, 2 * G, HEAD_DIM).transpose(1, 0, 2).reshape(2 * G, S // CMP_STRIDE,
                                                                          CMP_STRIDE * HEAD_DIM)
    kcv = _compress(kv_rows, cmp_pos, cmp_w1, cmp_w2)
    o_c, sel = _cmp_attention(rel_bias, u_bf, kcv)
    kv_pad = jnp.pad(u_bf[:, NSA_WIDTH + 2 * KV_WIDTH:], ((WINDOW, 0), (0, 0)))
    return _sel_attention(rel_bias, u_bf, kv_pad, sel, o_c, gate_logits, g_nsa)


def _moe(x, x_rows, w_router, b_router, w_gu, b_gu, w_dn, b_dn, ln_g, ln_b, layer):
    S, D = x.shape
    E = N_EXPERTS
    w_pad = jnp.pad(w_router, ((0, 0), (0, LANES - E)))
    b_pad = jnp.pad(b_router, (0, LANES - E), constant_values=NEG_INF).reshape(1, LANES)
    e_out, g_out, r_out, cnt_out = _router(x, w_pad, b_pad)
    e4 = e_out[:, :TOP_K]
    r4 = r_out[:, :TOP_K].astype(jnp.int32)
    cnt = cnt_out[0, :E].astype(jnp.int32)
    n_sub_max = (S * TOP_K) // MOE_SUB + E
    dest, tok, meta = _moe_tables(e4, r4, cnt, n_sub_max)
    flat = dest.reshape(-1)
    y_rows = _experts(meta, tok, x_rows, w_gu, b_gu.reshape(DEPTH, E, 1, 2 * D_FF), w_dn,
                      b_dn.reshape(DEPTH, E, 1, D), layer)
    return _combine_ln(flat, y_rows, g_out[:, :TOP_K].reshape(-1), x, ln_g, ln_b)


def kernel(x, w_in, conv_w, conv_b, lru_wa, lru_ba, lru_wi, lru_bi, lru_lambda, cmp_pos, cmp_w1, cmp_w2,
           rel_bias, g_lru, g_nsa, w_out, ln1_g, ln1_b, w_router, b_router, w_gate_up, b_gate_up, w_down,
           b_down, ln2_g, ln2_b):
    B, S, D = x.shape
    assert B == 1
    x = x.reshape(S, D)
    xb = x.astype(BF16)
    n_gate = 3 * NSA_HEADS
    for l in range(DEPTH):
        u_f32 = _project(xb, w_in, l, 0, 2 * LRU_WIDTH // 512, 512, F32)
        u_bf = _project(xb, w_in, l, 2 * LRU_WIDTH // 512, (MAIN_COLS - 2 * LRU_WIDTH) // 512, 512, BF16)
        w_gate = jnp.pad(w_in[l, :, MAIN_COLS:], ((0, 0), (0, LANES - n_gate)))[None]
        gate_logits = _project(xb, w_gate, 0, 0, 1, LANES, F32)
        y_lru = _lru_group(u_f32, conv_w[l], conv_b[l], lru_wa[l], lru_ba[l], lru_wi[l], lru_bi[l],
                           lru_lambda[l], g_lru[l])
        y_nsa = _nsa_group(u_bf, gate_logits, cmp_pos[l], cmp_w1[l], cmp_w2[l], rel_bias, g_nsa[l])
        x, x_rows = _out_proj_ln(y_lru, y_nsa, w_out[l].astype(BF16), x, ln1_g[l], ln1_b[l])
        x, xb = _moe(x, x_rows, w_router[l], b_router[l], w_gate_up, b_gate_up, w_down, b_down,
                     ln2_g[l], ln2_b[l], l)
    return x.reshape(B, S, D)
```

```python
import functools
import math

import numpy as np
import jax
import jax.numpy as jnp
from jax import lax
from jax.experimental import pallas as pl
from jax.experimental.pallas import tpu as pltpu

D_MODEL = 2048
DEPTH = 2
LRU_WIDTH = 1024
LRU_BLOCKS = 8
LRU_BLOCK_W = LRU_WIDTH // LRU_BLOCKS
CONV_WIDTH = 4
LRU_C = 8.0
NSA_HEADS = 8
NSA_KV_HEADS = 2
HEADS_PER_GROUP = NSA_HEADS // NSA_KV_HEADS
HEAD_DIM = 128
NSA_WIDTH = NSA_HEADS * HEAD_DIM
KV_WIDTH = NSA_KV_HEADS * HEAD_DIM
CMP_BLOCK = 32
CMP_STRIDE = 16
SEL_BLOCK = 64
N_SEL = 16
WINDOW = 512
Q_BLOCK = 128
N_BUCKETS = 32
MAX_DISTANCE = 128
N_EXPERTS = 32
TOP_K = 4
D_FF = 2048
SWIGLU_LIMIT = 7.0
SWIGLU_ALPHA = 1.702
DN_ALPHA = (2 * DEPTH) ** 0.25
NORM_EPS = 1e-5
NEG_INF = -1e30
FORCED_SCORE = 1e9
ATTN_SCALE = HEAD_DIM ** -0.5

LOG2E = math.log2(math.e)
FAR_CHUNK = 1024
KV_FRONT = FAR_CHUNK
PAD_BLK = KV_FRONT // SEL_BLOCK
M_INIT = -1e9

LANES = 128
MIB = 1024 * 1024
MAIN_COLS = 2 * LRU_WIDTH + NSA_WIDTH + 6 * KV_WIDTH
BF16 = jnp.bfloat16
F32 = jnp.float32

MOE_SUB = 256
MOE_ROWQ = 64
MOE_J = 5
MOE_FT = 512
MOE_NF = D_FF // MOE_FT
ROW_TILES = D_MODEL // LANES


def _bucket_thresholds():
    max_exact = N_BUCKETS // 2
    out = []
    for b in range(1, N_BUCKETS):
        if b <= max_exact:
            out.append(b)
            continue
        d = max_exact
        while True:
            v = math.log(d / max_exact) / math.log(MAX_DISTANCE / max_exact) * (N_BUCKETS - max_exact)
            assert abs(v - round(v)) > 1e-3 or d == max_exact
            if min(max_exact + int(v), N_BUCKETS - 1) >= b:
                break
            d += 1
        out.append(d)
    return tuple(out)


BUCKET_TH = _bucket_thresholds()


def _cparams(semantics, vmem_mib):
    return pltpu.CompilerParams(dimension_semantics=semantics, vmem_limit_bytes=vmem_mib * MIB)


def _nt_dot(a, b):
    return lax.dot_general(a, b, (((1,), (1,)), ((), ())), preferred_element_type=F32)


def _bias_of_dist(rb_ref, head, d):
    b = jnp.full(d.shape, rb_ref[0, head], F32)
    for k in range(1, N_BUCKETS):
        b = jnp.where(d >= BUCKET_TH[k - 1], rb_ref[k, head], b)
    return b


def _mm_kernel(x_ref, w_ref, o_ref, wb_ref):
    @pl.when(pl.program_id(1) == 0)
    def _():
        wb_ref[...] = w_ref[...].astype(BF16)

    o_ref[...] = jnp.dot(x_ref[...], wb_ref[...], preferred_element_type=F32).astype(o_ref.dtype)


def _project(xb, w3, layer, col_blk0, n_blks, tn, out_dtype, tm=1024):
    S, D = xb.shape
    return pl.pallas_call(
        _mm_kernel,
        grid=(n_blks, S // tm),
        in_specs=[pl.BlockSpec((tm, D), lambda n, m: (m, 0)),
                  pl.BlockSpec((None, D, tn), lambda n, m: (layer, 0, col_blk0 + n))],
        out_specs=pl.BlockSpec((tm, tn), lambda n, m: (m, n)),
        out_shape=jax.ShapeDtypeStruct((S, n_blks * tn), out_dtype),
        scratch_shapes=[pltpu.VMEM((D, tn), BF16)],
        compiler_params=_cparams(("arbitrary", "arbitrary"), 40),
        name="in_proj",
    )(xb, w3)


def _gelu_tanh(x):
    return 0.5 * x * (1.0 + jnp.tanh(math.sqrt(2.0 / math.pi) * (x + 0.044715 * (x * x * x))))


def _lru_kernel(xr_ref, xg_ref, cw_ref, cb_ref, wa_ref, ba_ref, wi_ref, bi_ref, lam_ref, g_ref,
                o_ref, xbuf, hc):
    i = pl.program_id(0)
    T = xr_ref.shape[0]
    W = xr_ref.shape[1]
    HALO = 8

    @pl.when(i == 0)
    def _():
        xbuf[0:HALO, :] = jnp.zeros((HALO, W), F32)
        hc[...] = jnp.zeros(hc.shape, F32)

    @pl.when(i > 0)
    def _():
        xbuf[0:HALO, :] = xbuf[T:T + HALO, :]

    xbuf[HALO:HALO + T, :] = xr_ref[...]
    cw = cw_ref[...]
    xc = cb_ref[...] + xbuf[HALO - 3:HALO - 3 + T, :] * cw[0:1]
    for k in range(1, CONV_WIDTH):
        xc = xc + xbuf[HALO - 3 + k:HALO - 3 + k + T, :] * cw[k:k + 1]

    xcb = xc.astype(BF16)
    ra, ia = [], []
    for n in range(LRU_BLOCKS):
        blk = xcb[:, n * LRU_BLOCK_W:(n + 1) * LRU_BLOCK_W]
        ra.append(jnp.dot(blk, wa_ref[n].astype(BF16), preferred_element_type=F32))
        ia.append(jnp.dot(blk, wi_ref[n].astype(BF16), preferred_element_type=F32))
    r = jax.nn.sigmoid(jnp.concatenate(ra, axis=1) + ba_ref[...])
    ig = jax.nn.sigmoid(jnp.concatenate(ia, axis=1) + bi_ref[...])

    z = -lam_ref[...]
    softplus = jnp.maximum(z, 0.0) + jnp.log(1.0 + jnp.exp(-jnp.abs(z)))
    log_a = -LRU_C * r * softplus
    a = jnp.exp(log_a)
    u = jnp.sqrt(1.0 - a * a) * (ig * xc)

    row = lax.broadcasted_iota(jnp.int32, (T, W), 0)
    A, U = a, u
    d = 1
    while d < T:
        keep = row >= d
        a_sh = jnp.where(keep, pltpu.roll(A, d, 0), 1.0)
        u_sh = jnp.where(keep, pltpu.roll(U, d, 0), 0.0)
        U = A * u_sh + U
        A = A * a_sh
        d *= 2
    h = U + A * hc[...]
    hc[...] = h[T - 1:T, :]

    y = h * _gelu_tanh(xg_ref[...])
    ms = jnp.mean(y * y, axis=-1, keepdims=True)
    o_ref[...] = (y * lax.rsqrt(ms + NORM_EPS) * g_ref[...]).astype(o_ref.dtype)


def _lru_group(u_f32, conv_w, conv_b, wa, ba, wi, bi, lam, g_lru, T=256):
    S = u_f32.shape[0]
    W = LRU_WIDTH
    row = lambda v: v.reshape(1, W)
    vec = pl.BlockSpec((1, W), lambda i: (0, 0))
    mat = pl.BlockSpec((LRU_BLOCKS, LRU_BLOCK_W, LRU_BLOCK_W), lambda i: (0, 0, 0))
    return pl.pallas_call(
        _lru_kernel,
        grid=(S // T,),
        in_specs=[pl.BlockSpec((T, W), lambda i: (i, 0)),
                  pl.BlockSpec((T, W), lambda i: (i, 1)),
                  pl.BlockSpec((CONV_WIDTH, W), lambda i: (0, 0)),
                  vec, mat, vec, mat, vec, vec, vec],
        out_specs=pl.BlockSpec((T, W), lambda i: (i, 0)),
        out_shape=jax.ShapeDtypeStruct((S, W), BF16),
        scratch_shapes=[pltpu.VMEM((T + 8, W), F32), pltpu.VMEM((1, W), F32)],
        compiler_params=_cparams(("arbitrary",), 48),
        name="rg_lru",
    )(u_f32, u_f32, conv_w, row(conv_b), wa, row(ba), wi, row(bi), row(lam), row(g_lru))


def _compress_kernel(x_ref, pos_ref, w1_ref, w2_ref, o_ref):
    NJ = x_ref.shape[0]
    half = (CMP_BLOCK // 2) * HEAD_DIM
    x = x_ref[...].astype(F32)
    a = (x + pos_ref[0:1, :]).astype(BF16)
    b = (x + pos_ref[1:2, :]).astype(BF16)
    y1 = jnp.dot(a, w1_ref[0:half, :].astype(BF16), preferred_element_type=F32)
    y2 = jnp.dot(b, w1_ref[half:2 * half, :].astype(BF16), preferred_element_type=F32)
    hmid = y1 + pltpu.roll(y2, NJ - 1, 0)
    o_ref[...] = jnp.dot(_gelu_tanh(hmid).astype(BF16), w2_ref[...].astype(BF16),
                         preferred_element_type=F32).astype(o_ref.dtype)


def _compress(kv_rows, pos, w1, w2):
    _, NJ, RW = kv_rows.shape
    return pl.pallas_call(
        _compress_kernel,
        grid=(4,),
        in_specs=[pl.BlockSpec((None, NJ, RW), lambda a: (a, 0, 0)),
                  pl.BlockSpec((None, 2, RW), lambda a: (a // 2, 0, 0)),
                  pl.BlockSpec((None, CMP_BLOCK * HEAD_DIM, HEAD_DIM), lambda a: (a // 2, 0, 0)),
                  pl.BlockSpec((None, HEAD_DIM, HEAD_DIM), lambda a: (a // 2, 0, 0))],
        out_specs=pl.BlockSpec((None, NJ, HEAD_DIM), lambda a: (a, 0, 0)),
        out_shape=jax.ShapeDtypeStruct((4, NJ, HEAD_DIM), BF16),
        compiler_params=_cparams(("arbitrary",), 32),
        name="kv_compress",
    )(kv_rows, pos.reshape(2, 2, RW), w1, w2)


def _cmp_attn_kernel(rb_ref, q_ref, kcv_ref, oc_ref, sel_ref, tcat):
    qb = pl.program_id(0)
    t0 = qb * Q_BLOCK
    NJ = kcv_ref.shape[1]
    NB = NJ * CMP_STRIDE // SEL_BLOCK
    NBP = sel_ref.shape[2]
    n_sel = min(N_SEL, NB)
    LOCAL = 2 * Q_BLOCK // CMP_STRIDE

    @pl.when(qb == 0)
    def _():
        i = lax.broadcasted_iota(jnp.int32, (Q_BLOCK, LANES), 0)
        j = lax.broadcasted_iota(jnp.int32, (Q_BLOCK, LANES), 1)
        d = jnp.maximum(i - CMP_STRIDE * j + (Q_BLOCK - CMP_STRIDE), 0)
        for h in range(NSA_HEADS):
            corr = jnp.where(j < LOCAL, (_bias_of_dist(rb_ref, h, d) - rb_ref[N_BUCKETS - 1, h]) * LOG2E, 0.0)
            hi = corr.astype(BF16)
            tcat[h, :, 0:LANES] = hi
            tcat[h, :, LANES:2 * LANES] = (corr - hi.astype(F32)).astype(BF16)

    def tile(W):
        row = lax.broadcasted_iota(jnp.int32, (Q_BLOCK, W), 0)
        col = lax.broadcasted_iota(jnp.int32, (Q_BLOCK, W), 1)
        mask_add = jnp.where(CMP_STRIDE * col + (CMP_BLOCK - 1) <= t0 + row, 0.0, NEG_INF)
        jj = lax.broadcasted_iota(jnp.int32, (2 * LANES, W), 0) % LANES
        cc = lax.broadcasted_iota(jnp.int32, (2 * LANES, W), 1)
        place = (cc == (Q_BLOCK // CMP_STRIDE) * (qb - 1) + jj).astype(F32).astype(BF16)
        bj = lax.broadcasted_iota(jnp.int32, (NB, W), 0)
        bc = lax.broadcasted_iota(jnp.int32, (NB, W), 1)
        ratio = SEL_BLOCK // CMP_STRIDE
        overlap_t = ((bc <= ratio * bj + ratio - 1) & (bc >= ratio * bj - 1)).astype(F32).astype(BF16)

        sj = lax.broadcasted_iota(jnp.int32, (NB, Q_BLOCK), 0)
        st = t0 + lax.broadcasted_iota(jnp.int32, (NB, Q_BLOCK), 1)
        cur = st // SEL_BLOCK
        blk_ok = sj <= cur
        forced = (sj == 0) | (sj == cur) | (sj == cur - 1)

        for g in range(NSA_KV_HEADS):
            kc = kcv_ref[g, 0:W, :]
            vc = kcv_ref[NSA_KV_HEADS + g, 0:W, :]
            psum = jnp.zeros((Q_BLOCK, W), F32)
            for h in range(HEADS_PER_GROUP):
                hh = g * HEADS_PER_GROUP + h
                qh = (q_ref[:, hh * HEAD_DIM:(hh + 1) * HEAD_DIM].astype(F32) * (ATTN_SCALE * LOG2E)).astype(BF16)
                z = _nt_dot(qh, kc) + jnp.dot(tcat[hh], place, preferred_element_type=F32) + mask_add
                m = jnp.maximum(jnp.max(z, axis=-1, keepdims=True), M_INIT)
                e = jnp.exp2(z - m)
                l = jnp.sum(e, axis=-1, keepdims=True)
                p = e * (1.0 / jnp.where(l > 0.0, l, 1.0))
                oc_ref[:, hh * HEAD_DIM:(hh + 1) * HEAD_DIM] = jnp.dot(
                    p.astype(BF16), vc, preferred_element_type=F32)
                psum = psum + p
            p_hi = psum.astype(BF16)
            p_lo = (psum - p_hi.astype(F32)).astype(BF16)
            imp_t = _nt_dot(overlap_t, p_hi) + _nt_dot(overlap_t, p_lo)
            score = jnp.where(blk_ok, jnp.where(forced, FORCED_SCORE, imp_t), NEG_INF)
            chosen = jnp.zeros((NB, Q_BLOCK), jnp.bool_)
            sjf = sj.astype(F32)
            for _ in range(n_sel):
                mx = jnp.max(score, axis=0, keepdims=True)
                first = jnp.min(jnp.where(score == mx, sjf, float(NB)), axis=0, keepdims=True)
                pick = sjf == first
                chosen = chosen | pick
                score = jnp.where(pick, -3e38, score)
            pen_t = jnp.where(chosen & blk_ok, 0.0, NEG_INF)
            padded = jnp.concatenate([jnp.full((PAD_BLK, Q_BLOCK), NEG_INF, F32), pen_t,
                                      jnp.zeros((NBP - PAD_BLK - NB, Q_BLOCK), F32)], axis=0)
            pen = jnp.concatenate([padded[r:r + LANES].T for r in range(0, NBP, LANES)], axis=1)
            sel_ref[g] = pen.astype(sel_ref.dtype)

    step = min(LANES, NJ)
    need = (Q_BLOCK // CMP_STRIDE) * qb + (Q_BLOCK - CMP_BLOCK) // CMP_STRIDE + 1
    width = jnp.minimum((need + step - 1) // step * step, NJ)
    for W in range(step, NJ + 1, step):
        pl.when(width == W)(functools.partial(tile, W))


def _cmp_attention(rel_bias, u_bf, kcv):
    S = u_bf.shape[0]
    NJ = kcv.shape[1]
    NB = -(-(S // SEL_BLOCK + PAD_BLK) // LANES) * LANES
    return pl.pallas_call(
        _cmp_attn_kernel,
        grid=(S // Q_BLOCK,),
        in_specs=[pl.BlockSpec(memory_space=pltpu.SMEM),
                  pl.BlockSpec((Q_BLOCK, NSA_WIDTH), lambda i: (i, 0)),
                  pl.BlockSpec((4, NJ, HEAD_DIM), lambda i: (0, 0, 0))],
        out_specs=[pl.BlockSpec((Q_BLOCK, NSA_WIDTH), lambda i: (i, 0)),
                   pl.BlockSpec((NSA_KV_HEADS, Q_BLOCK, NB), lambda i: (0, i, 0))],
        out_shape=[jax.ShapeDtypeStruct((S, NSA_WIDTH), F32),
                   jax.ShapeDtypeStruct((NSA_KV_HEADS, S, NB), BF16)],
        scratch_shapes=[pltpu.VMEM((NSA_HEADS, Q_BLOCK, 2 * LANES), BF16)],
        compiler_params=_cparams(("arbitrary",), 32),
        name="nsa_compressed",
    )(rel_bias, u_bf, kcv)


def _sel_attn_kernel(rb_ref, q_ref, ks_ref, vs_ref, kw_ref, vw_ref, sel_ref, oc_ref, gl_ref, g_ref,
                     o_ref, tbl_s, tbl_w, yacc):
    qb = pl.program_id(0)
    Q = Q_BLOCK
    HPG = HEADS_PER_GROUP
    TW = WINDOW + Q
    FAR = FAR_CHUNK
    NBP = sel_ref.shape[2]
    t0 = qb * Q

    @pl.when(qb == 0)
    def _():
        i = lax.broadcasted_iota(jnp.int32, (Q, TW), 0)
        c = lax.broadcasted_iota(jnp.int32, (Q, TW), 1)
        d = i - c + WINDOW
        dd = jnp.maximum(d, 0)
        for h in range(NSA_HEADS):
            corr = (_bias_of_dist(rb_ref, h, dd) - rb_ref[N_BUCKETS - 1, h]) * LOG2E
            tbl_s[h] = jnp.where(d < 0, NEG_INF, corr)
            tbl_w[h] = jnp.where((d < 0) | (d >= WINDOW), NEG_INF, corr)

    gates = jax.nn.sigmoid(gl_ref[...])
    col_w = lax.broadcasted_iota(jnp.int32, (1, TW), 1)
    before_start = jnp.where(col_w < WINDOW - t0, NEG_INF, 0.0)
    p_tail = pl.multiple_of(t0 + (KV_FRONT - WINDOW), Q)

    G = NSA_KV_HEADS
    q4 = [jnp.concatenate(
        [(q_ref[:, (g * HPG + h) * HEAD_DIM:(g * HPG + h + 1) * HEAD_DIM].astype(F32)
          * (ATTN_SCALE * LOG2E)).astype(BF16) for h in range(HPG)], axis=0) for g in range(G)]

    def attend(g, k_ref, v_ref, p0, width, add, carry):
        m, l, acc = carry
        cols = slice(g * HEAD_DIM, (g + 1) * HEAD_DIM)
        kk = k_ref[pl.ds(p0, width), cols]
        vv = v_ref[pl.ds(p0, width), cols]
        s = _nt_dot(q4[g], kk)
        z = (s.reshape(HPG, Q, width) + add).reshape(HPG * Q, width)
        m_new = jnp.maximum(m, jnp.max(z, axis=-1, keepdims=True))
        alpha = jnp.exp2(m - m_new)
        e = jnp.exp2(z - m_new)
        l = alpha * l + jnp.sum(e, axis=-1, keepdims=True)
        acc = alpha * acc + jnp.dot(e.astype(BF16), vv, preferred_element_type=F32)
        return m_new, l, acc

    def block_mask(g, p0, width):
        bp = lax.broadcasted_iota(jnp.int32, (NBP, width), 0)
        kb = lax.broadcasted_iota(jnp.int32, (NBP, width), 1) // SEL_BLOCK
        expand = (bp - p0 // SEL_BLOCK == kb).astype(F32).astype(BF16)
        return jnp.dot(sel_ref[g], expand, preferred_element_type=F32)[None]

    init = (jnp.full((HPG * Q, 1), M_INIT, F32), jnp.zeros((HPG * Q, 1), F32),
            jnp.zeros((HPG * Q, HEAD_DIM), F32))

    def far_body(i, carry):
        p0 = pl.multiple_of(p_tail - FAR * (i + 1), Q)
        return tuple(attend(g, ks_ref, vs_ref, p0, FAR, block_mask(g, p0, FAR), carry[g]) for g in range(G))

    n_far = (jnp.maximum(t0 - WINDOW, 0) + FAR - 1) // FAR
    far = lax.fori_loop(0, n_far, far_body, (init,) * G)

    for g in range(G):
        heads = slice(g * HPG, (g + 1) * HPG)
        m_s, l_s, acc_s = attend(g, ks_ref, vs_ref, p_tail, TW, tbl_s[heads] + block_mask(g, p_tail, TW), far[g])
        o_s = acc_s / jnp.where(l_s > 0.0, l_s, 1.0)

        m_w, l_w, acc_w = attend(g, kw_ref, vw_ref, p_tail, TW, tbl_w[heads] + before_start[None], init)
        o_w = acc_w / jnp.where(l_w > 0.0, l_w, 1.0)

        for h in range(HPG):
            hh = g * HPG + h
            hc = slice(hh * HEAD_DIM, (hh + 1) * HEAD_DIM)
            rows = slice(h * Q, (h + 1) * Q)
            yacc[:, hc] = (gates[:, 3 * hh:3 * hh + 1] * oc_ref[:, hc]
                           + gates[:, 3 * hh + 1:3 * hh + 2] * o_s[rows]
                           + gates[:, 3 * hh + 2:3 * hh + 3] * o_w[rows])

    y = yacc[...]
    ms = jnp.mean(y * y, axis=-1, keepdims=True)
    o_ref[...] = (y * lax.rsqrt(ms + NORM_EPS) * g_ref[...]).astype(o_ref.dtype)


def _sel_attention(rel_bias, u_bf, kv_pad, sel, o_c, gate_logits, g_nsa):
    S = u_bf.shape[0]
    NB = sel.shape[2]

    def kv_spec(j):
        return pl.BlockSpec((KV_FRONT + S, KV_WIDTH), lambda i: (0, j))

    return pl.pallas_call(
        _sel_attn_kernel,
        grid=(S // Q_BLOCK,),
        in_specs=[pl.BlockSpec(memory_space=pltpu.SMEM),
                  pl.BlockSpec((Q_BLOCK, NSA_WIDTH), lambda i: (i, 0)),
                  kv_spec(0), kv_spec(1), kv_spec(2), kv_spec(3),
                  pl.BlockSpec((NSA_KV_HEADS, Q_BLOCK, NB), lambda i: (0, i, 0)),
                  pl.BlockSpec((Q_BLOCK, NSA_WIDTH), lambda i: (i, 0)),
                  pl.BlockSpec((Q_BLOCK, LANES), lambda i: (i, 0)),
                  pl.BlockSpec((1, NSA_WIDTH), lambda i: (0, 0))],
        out_specs=pl.BlockSpec((Q_BLOCK, NSA_WIDTH), lambda i: (i, 0)),
        out_shape=jax.ShapeDtypeStruct((S, NSA_WIDTH), BF16),
        scratch_shapes=[pltpu.VMEM((NSA_HEADS, Q_BLOCK, WINDOW + Q_BLOCK), F32),
                        pltpu.VMEM((NSA_HEADS, Q_BLOCK, WINDOW + Q_BLOCK), F32),
                        pltpu.VMEM((Q_BLOCK, NSA_WIDTH), F32)],
        compiler_params=_cparams(("arbitrary",), 60),
        name="nsa_selected_window",
    )(rel_bias, u_bf, kv_pad, kv_pad, kv_pad, kv_pad, sel, o_c, gate_logits, g_nsa.reshape(1, NSA_WIDTH))


def _layer_norm_rows(z, g, b):
    mu = jnp.mean(z, axis=-1, keepdims=True)
    zc = z - mu
    var = jnp.mean(zc * zc, axis=-1, keepdims=True)
    return zc * lax.rsqrt(var + NORM_EPS) * g + b


def _outproj_kernel(yl_ref, yn_ref, w_ref, x_ref, g_ref, b_ref, o_ref, or_ref):
    tm = x_ref.shape[0]
    acc = jnp.dot(yl_ref[...], w_ref[0:LRU_WIDTH, :], preferred_element_type=F32)
    acc = acc + jnp.dot(yn_ref[...], w_ref[LRU_WIDTH:LRU_WIDTH + NSA_WIDTH, :], preferred_element_type=F32)
    y = _layer_norm_rows(DN_ALPHA * x_ref[...] + acc, g_ref[...], b_ref[...])
    o_ref[...] = y
    for c in range(ROW_TILES):
        or_ref[pl.ds(c, tm, stride=ROW_TILES), :] = y[:, c * LANES:(c + 1) * LANES]


def _out_proj_ln(y_lru, y_nsa, w_out_b, x, g, b, tm=256):
    S, D = x.shape
    vec = pl.BlockSpec((1, D), lambda i: (0, 0))
    return pl.pallas_call(
        _outproj_kernel,
        grid=(S // tm,),
        in_specs=[pl.BlockSpec((tm, LRU_WIDTH), lambda i: (i, 0)),
                  pl.BlockSpec((tm, NSA_WIDTH), lambda i: (i, 0)),
                  pl.BlockSpec((LRU_WIDTH + NSA_WIDTH, D), lambda i: (0, 0)),
                  pl.BlockSpec((tm, D), lambda i: (i, 0)), vec, vec],
        out_specs=[pl.BlockSpec((tm, D), lambda i: (i, 0)),
                   pl.BlockSpec((tm * ROW_TILES, LANES), lambda i: (i, 0))],
        out_shape=[jax.ShapeDtypeStruct((S, D), F32), jax.ShapeDtypeStruct((S * ROW_TILES, LANES), F32)],
        compiler_params=_cparams(("arbitrary",), 48),
        name="out_proj_ln",
    )(y_lru, y_nsa, w_out_b, x, g.reshape(1, D), b.reshape(1, D))


def _router_kernel(x_ref, w_ref, b_ref, e_ref, g_ref, r_ref, cnt_ref, carry):
    i = pl.program_id(0)
    tm = x_ref.shape[0]

    @pl.when(i == 0)
    def _():
        carry[...] = jnp.zeros(carry.shape, F32)

    logits = jnp.dot(x_ref[...], w_ref[...], preferred_element_type=F32,
                     precision=lax.Precision.HIGHEST) + b_ref[...]
    lane = lax.broadcasted_iota(jnp.int32, (tm, LANES), 1)
    lanef = lane.astype(F32)
    s = logits
    picks = []
    chosen = jnp.zeros((tm, LANES), jnp.bool_)
    for _ in range(TOP_K):
        mx = jnp.max(s, axis=-1, keepdims=True)
        first = jnp.min(jnp.where(s == mx, lanef, float(LANES)), axis=-1, keepdims=True)
        pick = lanef == first
        picks.append((pick, first))
        chosen = chosen | pick
        s = jnp.where(pick, -3e38, s)
    top = jnp.max(logits, axis=-1, keepdims=True)
    ex = jnp.where(chosen, jnp.exp(logits - top), 0.0)
    gate = ex / jnp.sum(ex, axis=-1, keepdims=True)

    onehot = jnp.where(chosen, 1.0, 0.0)
    rr = lax.broadcasted_iota(jnp.int32, (tm, tm), 0)
    rc = lax.broadcasted_iota(jnp.int32, (tm, tm), 1)
    before = (rr > rc).astype(F32).astype(BF16)
    rank = jnp.dot(before, onehot.astype(BF16), preferred_element_type=F32) + carry[0:1, :]
    carry[0:1, :] = carry[0:1, :] + jnp.sum(onehot, axis=0, keepdims=True)
    cnt_ref[...] = carry[...]

    e_out = jnp.zeros((tm, LANES), jnp.int32)
    g_out = jnp.zeros((tm, LANES), F32)
    r_out = jnp.zeros((tm, LANES), F32)
    for k, (pick, first) in enumerate(picks):
        e_out = jnp.where(lane == k, first.astype(jnp.int32), e_out)
        g_out = jnp.where(lane == k, jnp.sum(jnp.where(pick, gate, 0.0), axis=-1, keepdims=True), g_out)
        r_out = jnp.where(lane == k, jnp.sum(jnp.where(pick, rank, 0.0), axis=-1, keepdims=True), r_out)
    e_ref[...] = e_out
    g_ref[...] = g_out
    r_ref[...] = r_out


def _router(x, w_pad, b_pad, tm=256):
    S, D = x.shape
    out = pl.BlockSpec((tm, LANES), lambda i: (i, 0))
    return pl.pallas_call(
        _router_kernel,
        grid=(S // tm,),
        in_specs=[pl.BlockSpec((tm, D), lambda i: (i, 0)),
                  pl.BlockSpec((D, LANES), lambda i: (0, 0)),
                  pl.BlockSpec((1, LANES), lambda i: (0, 0))],
        out_specs=[out, out, out, pl.BlockSpec((8, LANES), lambda i: (0, 0))],
        out_shape=[jax.ShapeDtypeStruct((S, LANES), jnp.int32), jax.ShapeDtypeStruct((S, LANES), F32),
                   jax.ShapeDtypeStruct((S, LANES), F32), jax.ShapeDtypeStruct((8, LANES), F32)],
        scratch_shapes=[pltpu.VMEM((8, LANES), F32)],
        compiler_params=_cparams(("arbitrary",), 32),
        name="moe_router",
    )(x, w_pad, b_pad)


def _expert_kernel(sb_ref, se_ref, sf_ref, sj_ref, so_ref, sv_ref, tok_ref,
                   x_hbm, wgu_ref, bgu_ref, wdn_ref, bdn_ref, o_ref, wgu_b, wdn_b, acc, stage, xrows, sem):
    s = pl.program_id(0)
    n_steps = pl.num_programs(0)
    nrow = sv_ref[s]
    valid = nrow > 0
    f = sf_ref[s]
    j = sj_ref[s]
    SUB = MOE_SUB
    FT2 = wgu_ref.shape[1]
    RT = ROW_TILES
    UNROLL = 8

    def gather_rows(step):
        base = sb_ref[step] * SUB
        slot = sj_ref[step] % 2

        def issue(i, carry):
            for u in range(UNROLL):
                r = i * UNROLL + u
                t = tok_ref[base + r]
                pltpu.make_async_copy(x_hbm.at[t], stage.at[slot, :, r, :], sem.at[slot]).start()
            return carry

        lax.fori_loop(0, sv_ref[step] // UNROLL, issue, 0)

    @pl.when(s == 0)
    def _():
        gather_rows(0)

    nxt = jnp.minimum(s + 1, n_steps - 1)

    @pl.when((s + 1 < n_steps) & (sv_ref[nxt] > 0) & (sf_ref[nxt] == 0))
    def _():
        gather_rows(nxt)

    @pl.when(valid & (j == 0))
    def _():
        wgu_b[...] = wgu_ref[...].astype(BF16)
        wdn_b[...] = wdn_ref[...].astype(BF16)

    @pl.when(jnp.logical_not(valid))
    def _():
        o_ref[...] = jnp.zeros(o_ref.shape, F32)

    def process(m):
        rows = pl.ds(pl.multiple_of(j * SUB, SUB), m)

        @pl.when(f == 0)
        def _():
            slot = j % 2
            filled = stage.at[slot, :, pl.ds(0, m), :]
            pltpu.make_async_copy(filled, filled, sem.at[slot]).wait()
            for c in range(RT):
                xrows[rows, c * LANES:(c + 1) * LANES] = stage[slot, c, 0:m, :].astype(BF16)

        gu = jnp.dot(xrows[rows, :], wgu_b[...], preferred_element_type=F32) + bgu_ref[...]
        gate = jnp.minimum(gu, SWIGLU_LIMIT)
        act = gate * jax.nn.sigmoid(SWIGLU_ALPHA * gate)
        up1 = jnp.clip(gu, -SWIGLU_LIMIT, SWIGLU_LIMIT) + 1.0
        pr = lax.broadcasted_iota(jnp.int32, (2 * LANES, LANES), 0)
        pc = lax.broadcasted_iota(jnp.int32, (2 * LANES, LANES), 1)
        pick_even = (pr == 2 * pc).astype(F32).astype(BF16)
        hs = []
        for c in range(FT2 // (2 * LANES)):
            parts = []
            for k in range(2):
                cs = slice((2 * c + k) * LANES, (2 * c + k + 1) * LANES)
                parts.append((act[:, cs] * pltpu.roll(up1[:, cs], LANES - 1, 1)).astype(BF16))
            pair = jnp.concatenate(parts, axis=1)
            hs.append(jnp.dot(pair, pick_even, preferred_element_type=F32).astype(BF16))
        h = jnp.concatenate(hs, axis=1)
        y = jnp.dot(h, wdn_b[...], preferred_element_type=F32)

        @pl.when(f == 0)
        def _():
            acc[rows, :] = y + bdn_ref[...]

        @pl.when((f > 0) & (f < MOE_NF - 1))
        def _():
            acc[rows, :] = acc[rows, :] + y

        @pl.when(f == MOE_NF - 1)
        def _():
            yfin = acc[rows, :] + y
            for c in range(RT):
                o_ref[pl.ds(c, m, stride=RT), :] = yfin[:, c * LANES:(c + 1) * LANES]
            if m < SUB:
                o_ref[pl.ds(m * RT, (SUB - m) * RT), :] = jnp.zeros(((SUB - m) * RT, LANES), F32)

    for m in range(MOE_ROWQ, SUB + 1, MOE_ROWQ):
        pl.when(nrow == m)(functools.partial(process, m))


def _experts(meta, tok, x_rows, w_gu, b_gu3, w_dn, b_dn3, layer):
    sb, se, sf, sj, so, sv = meta
    NS = sb.shape[0]
    R = tok.shape[0]
    D = D_MODEL
    RT = ROW_TILES

    def wmap(fn):
        return lambda s, sb, se, sf, sj, so, sv, tok: fn(s, se, sf, so)

    grid_spec = pltpu.PrefetchScalarGridSpec(
        num_scalar_prefetch=7,
        grid=(NS,),
        in_specs=[
            pl.BlockSpec(memory_space=pl.ANY),
            pl.BlockSpec((None, None, D, 2 * MOE_FT), wmap(lambda s, se, sf, so: (layer, se[s], 0, sf[s]))),
            pl.BlockSpec((None, None, 1, 2 * MOE_FT), wmap(lambda s, se, sf, so: (layer, se[s], 0, sf[s]))),
            pl.BlockSpec((None, None, MOE_FT, D), wmap(lambda s, se, sf, so: (layer, se[s], sf[s], 0))),
            pl.BlockSpec((None, None, 1, D), wmap(lambda s, se, sf, so: (layer, se[s], 0, 0))),
        ],
        out_specs=pl.BlockSpec((MOE_SUB * RT, LANES), wmap(lambda s, se, sf, so: (so[s], 0))),
        scratch_shapes=[pltpu.VMEM((D, 2 * MOE_FT), BF16), pltpu.VMEM((MOE_FT, D), BF16),
                        pltpu.VMEM((MOE_J * MOE_SUB, D), F32),
                        pltpu.VMEM((2, RT, MOE_SUB, LANES), F32),
                        pltpu.VMEM((MOE_J * MOE_SUB, D), BF16),
                        pltpu.SemaphoreType.DMA((2,))],
    )
    return pl.pallas_call(
        _expert_kernel,
        grid_spec=grid_spec,
        out_shape=jax.ShapeDtypeStruct((R * RT, LANES), F32),
        compiler_params=_cparams(("arbitrary",), 60),
        name="moe_experts",
    )(sb, se, sf, sj, so, sv, tok, x_rows, w_gu, b_gu3, w_dn, b_dn3)


def _moe_tables(e4, r4, cnt, n_sub_max):
    E = N_EXPERTS
    nsub = (cnt + MOE_SUB - 1) // MOE_SUB
    sub_end = jnp.cumsum(nsub)
    sub_start = sub_end - nsub
    n_blk = sub_end[-1]
    dest = sub_start[e4] * MOE_SUB + r4

    b = jnp.arange(n_sub_max, dtype=jnp.int32)
    be = jnp.minimum(jnp.sum(b[:, None] >= sub_end[None, :], axis=1), E - 1).astype(jnp.int32)
    lb = b - sub_start[be]
    jb = lb % MOE_J
    b0 = b - jb
    nj = jnp.minimum(MOE_J, nsub[be] - (lb // MOE_J) * MOE_J)
    NS = MOE_NF * n_sub_max
    bb = jnp.repeat(b, MOE_NF)
    ff = jnp.tile(jnp.arange(MOE_NF, dtype=jnp.int32), n_sub_max)
    step = jnp.where(bb < n_blk, MOE_NF * b0[bb] + ff * nj[bb] + jb[bb], NS)
    step_b = jnp.zeros((NS,), jnp.int32).at[step].set(bb, mode='drop')
    step_f = jnp.zeros((NS,), jnp.int32).at[step].set(ff, mode='drop')
    sidx = jnp.arange(NS, dtype=jnp.int32)
    n_steps = MOE_NF * n_blk
    valid = sidx < n_steps
    last = jnp.maximum(n_steps - 1, 0)
    step_b = jnp.where(valid, step_b, step_b[last])
    step_f = jnp.where(valid, step_f, step_f[last])
    out_b = jnp.where(step_f == MOE_NF - 1, step_b, b0[step_b])
    out_b = jnp.where(valid, out_b, jnp.minimum(n_blk + (sidx - n_steps), n_sub_max - 1))
    rem = cnt[be] - lb * MOE_SUB
    rows_b = jnp.clip(-(-rem // MOE_ROWQ) * MOE_ROWQ, MOE_ROWQ, MOE_SUB)
    meta = (step_b, be[step_b], step_f, jb[step_b], out_b, jnp.where(valid, rows_b[step_b], 0).astype(jnp.int32))

    dest = dest.astype(jnp.int32)
    n_tok = e4.shape[0]
    _, tok_sorted = lax.sort_key_val(dest.reshape(-1), jnp.repeat(jnp.arange(n_tok, dtype=jnp.int32), TOP_K))
    row_e = jnp.repeat(be, MOE_SUB)
    local = jnp.arange(n_sub_max * MOE_SUB, dtype=jnp.int32) - sub_start[row_e] * MOE_SUB
    first = jnp.cumsum(cnt) - cnt
    tok = jnp.where(local < cnt[row_e], tok_sorted[jnp.clip(first[row_e] + local, 0, n_tok * TOP_K - 1)], 0)
    return dest, tok.astype(jnp.int32), meta


def _combine_kernel(dest_ref, y_hbm, g4_ref, x_ref, g_ref, b_ref, o_ref, ob_ref, stage, ytile, ysum, sem):
    i = pl.program_id(0)
    n = pl.num_programs(0)
    tm = x_ref.shape[0]
    RT = ROW_TILES
    n_rows = tm * TOP_K

    def gather_rows(step, slot):
        base = step * n_rows

        def issue(r, carry):
            d = dest_ref[base + r]
            pltpu.make_async_copy(y_hbm.at[pl.ds(pl.multiple_of(d * RT, RT), RT), :],
                                  stage.at[slot, pl.ds(pl.multiple_of(r * RT, RT), RT), :],
                                  sem.at[slot]).start()
            return carry

        lax.fori_loop(0, n_rows, issue, 0, unroll=8)

    @pl.when(i == 0)
    def _():
        gather_rows(0, 0)

    @pl.when(i + 1 < n)
    def _():
        gather_rows(i + 1, (i + 1) % 2)

    slot = i % 2
    pltpu.make_async_copy(y_hbm.at[pl.ds(0, n_rows * RT), :], stage.at[slot], sem.at[slot]).wait()

    def token_sum(t, carry):
        r0 = pl.multiple_of(t * (TOP_K * RT), TOP_K * RT)
        y = g4_ref[TOP_K * t] * stage[slot, pl.ds(r0, RT), :]
        for k in range(1, TOP_K):
            y = y + g4_ref[TOP_K * t + k] * stage[slot, pl.ds(r0 + k * RT, RT), :]
        ytile[pl.ds(pl.multiple_of(t * RT, RT), RT), :] = y
        return carry

    lax.fori_loop(0, tm, token_sum, 0, unroll=4)
    for c in range(RT):
        ysum[:, c * LANES:(c + 1) * LANES] = ytile[pl.ds(c, tm, stride=RT), :]
    out = _layer_norm_rows(DN_ALPHA * x_ref[...] + ysum[...], g_ref[...], b_ref[...])
    o_ref[...] = out
    ob_ref[...] = out.astype(BF16)


def _combine_ln(dest_flat, y_rows, g4, x, g, b, tm=128):
    S, D = x.shape
    vec = pl.BlockSpec((1, D), lambda i, d: (0, 0))
    grid_spec = pltpu.PrefetchScalarGridSpec(
        num_scalar_prefetch=1,
        grid=(S // tm,),
        in_specs=[pl.BlockSpec(memory_space=pl.ANY),
                  pl.BlockSpec((tm * TOP_K,), lambda i, d: (i,), memory_space=pltpu.SMEM),
                  pl.BlockSpec((tm, D), lambda i, d: (i, 0)), vec, vec],
        out_specs=[pl.BlockSpec((tm, D), lambda i, d: (i, 0)), pl.BlockSpec((tm, D), lambda i, d: (i, 0))],
        scratch_shapes=[pltpu.VMEM((2, tm * TOP_K * ROW_TILES, LANES), F32),
                        pltpu.VMEM((tm * ROW_TILES, LANES), F32),
                        pltpu.VMEM((tm, D), F32),
                        pltpu.SemaphoreType.DMA((2,))],
    )
    return pl.pallas_call(
        _combine_kernel,
        grid_spec=grid_spec,
        out_shape=[jax.ShapeDtypeStruct((S, D), F32), jax.ShapeDtypeStruct((S, D), BF16)],
        compiler_params=_cparams(("arbitrary",), 48),
        name="moe_combine_ln",
    )(dest_flat, y_rows, g4, x, g.reshape(1, D), b.reshape(1, D))


def _nsa_group(u_bf, gate_logits, cmp_pos, cmp_w1, cmp_w2, rel_bias, g_nsa):
    S = u_bf.shape[0]
    G = NSA_KV_HEADS
    kvc = u_bf[:, NSA_WIDTH:NSA_WIDTH + 2 * KV_WIDTH]
    kv_rows = kvc.reshape(S, 2 * G, HEAD_DIM).transpose(1, 0, 2).reshape(2 * G, S // CMP_STRIDE,
                                                                          CMP_STRIDE * HEAD_DIM)
    kcv = _compress(kv_rows, cmp_pos, cmp_w1, cmp_w2)
    o_c, sel = _cmp_attention(rel_bias, u_bf, kcv)
    kv_pad = jnp.pad(u_bf[:, NSA_WIDTH + 2 * KV_WIDTH:], ((KV_FRONT, 0), (0, 0)))
    return _sel_attention(rel_bias, u_bf, kv_pad, sel, o_c, gate_logits, g_nsa)


def _moe(x, x_rows, w_router, b_router, w_gu, b_gu, w_dn, b_dn, ln_g, ln_b, layer):
    S, D = x.shape
    E = N_EXPERTS
    w_pad = jnp.pad(w_router, ((0, 0), (0, LANES - E)))
    b_pad = jnp.pad(b_router, (0, LANES - E), constant_values=NEG_INF).reshape(1, LANES)
    e_out, g_out, r_out, cnt_out = _router(x, w_pad, b_pad)
    e4 = e_out[:, :TOP_K]
    r4 = r_out[:, :TOP_K].astype(jnp.int32)
    cnt = cnt_out[0, :E].astype(jnp.int32)
    n_sub_max = (S * TOP_K) // MOE_SUB + E
    dest, tok, meta = _moe_tables(e4, r4, cnt, n_sub_max)
    flat = dest.reshape(-1)
    y_rows = _experts(meta, tok, x_rows.reshape(S, ROW_TILES, LANES), w_gu, b_gu.reshape(DEPTH, E, 1, 2 * D_FF), w_dn,
                      b_dn.reshape(DEPTH, E, 1, D), layer)
    return _combine_ln(flat, y_rows, g_out[:, :TOP_K].reshape(-1), x, ln_g, ln_b)


def kernel(x, w_in, conv_w, conv_b, lru_wa, lru_ba, lru_wi, lru_bi, lru_lambda, cmp_pos, cmp_w1, cmp_w2,
           rel_bias, g_lru, g_nsa, w_out, ln1_g, ln1_b, w_router, b_router, w_gate_up, b_gate_up, w_down,
           b_down, ln2_g, ln2_b):
    B, S, D = x.shape
    assert B == 1
    x = x.reshape(S, D)
    xb = x.astype(BF16)
    n_gate = 3 * NSA_HEADS
    for l in range(DEPTH):
        u_f32 = _project(xb, w_in, l, 0, 2 * LRU_WIDTH // 512, 512, F32)
        u_bf = _project(xb, w_in, l, 2 * LRU_WIDTH // 512, (MAIN_COLS - 2 * LRU_WIDTH) // 512, 512, BF16)
        w_gate = jnp.pad(w_in[l, :, MAIN_COLS:], ((0, 0), (0, LANES - n_gate)))[None]
        gate_logits = _project(xb, w_gate, 0, 0, 1, LANES, F32)
        y_lru = _lru_group(u_f32, conv_w[l], conv_b[l], lru_wa[l], lru_ba[l], lru_wi[l], lru_bi[l],
                           lru_lambda[l], g_lru[l])
        y_nsa = _nsa_group(u_bf, gate_logits, cmp_pos[l], cmp_w1[l], cmp_w2[l], rel_bias, g_nsa[l])
        x, x_rows = _out_proj_ln(y_lru, y_nsa, w_out[l].astype(BF16), x, ln1_g[l], ln1_b[l])
        x, xb = _moe(x, x_rows, w_router[l], b_router[l], w_gate_up, b_gate_up, w_down, b_down,
                     ln2_g[l], ln2_b[l], l)
    return x.reshape(B, S, D)
```

```python
import functools
import math

import numpy as np
import jax
import jax.numpy as jnp
from jax import lax
from jax.experimental import pallas as pl
from jax.experimental.pallas import tpu as pltpu

D_MODEL = 2048
DEPTH = 2
LRU_WIDTH = 1024
LRU_BLOCKS = 8
LRU_BLOCK_W = LRU_WIDTH // LRU_BLOCKS
CONV_WIDTH = 4
LRU_C = 8.0
NSA_HEADS = 8
NSA_KV_HEADS = 2
HEADS_PER_GROUP = NSA_HEADS // NSA_KV_HEADS
HEAD_DIM = 128
NSA_WIDTH = NSA_HEADS * HEAD_DIM
KV_WIDTH = NSA_KV_HEADS * HEAD_DIM
CMP_BLOCK = 32
CMP_STRIDE = 16
SEL_BLOCK = 64
N_SEL = 16
WINDOW = 512
Q_BLOCK = 128
N_BUCKETS = 32
MAX_DISTANCE = 128
N_EXPERTS = 32
TOP_K = 4
D_FF = 2048
SWIGLU_LIMIT = 7.0
SWIGLU_ALPHA = 1.702
DN_ALPHA = (2 * DEPTH) ** 0.25
NORM_EPS = 1e-5
NEG_INF = -1e30
FORCED_SCORE = 1e9
ATTN_SCALE = HEAD_DIM ** -0.5

LOG2E = math.log2(math.e)
FAR_CHUNK = 1024
KV_FRONT = FAR_CHUNK
PAD_BLK = KV_FRONT // SEL_BLOCK
M_INIT = -1e9

LANES = 128
MIB = 1024 * 1024
MAIN_COLS = 2 * LRU_WIDTH + NSA_WIDTH + 6 * KV_WIDTH
BF16 = jnp.bfloat16
F32 = jnp.float32

MOE_SUB = 256
MOE_ROWQ = 64
MOE_J = 5
MOE_FT = 512
MOE_NF = D_FF // MOE_FT
ROW_TILES = D_MODEL // LANES


def _bucket_thresholds():
    max_exact = N_BUCKETS // 2
    out = []
    for b in range(1, N_BUCKETS):
        if b <= max_exact:
            out.append(b)
            continue
        d = max_exact
        while True:
            v = math.log(d / max_exact) / math.log(MAX_DISTANCE / max_exact) * (N_BUCKETS - max_exact)
            assert abs(v - round(v)) > 1e-3 or d == max_exact
            if min(max_exact + int(v), N_BUCKETS - 1) >= b:
                break
            d += 1
        out.append(d)
    return tuple(out)


BUCKET_TH = _bucket_thresholds()


def _cparams(semantics, vmem_mib):
    return pltpu.CompilerParams(dimension_semantics=semantics, vmem_limit_bytes=vmem_mib * MIB)


def _nt_dot(a, b):
    return lax.dot_general(a, b, (((1,), (1,)), ((), ())), preferred_element_type=F32)


def _bias_of_dist(rb_ref, head, d):
    b = jnp.full(d.shape, rb_ref[0, head], F32)
    for k in range(1, N_BUCKETS):
        b = jnp.where(d >= BUCKET_TH[k - 1], rb_ref[k, head], b)
    return b


def _mm_kernel(x_ref, w_ref, o_ref, wb_ref):
    @pl.when(pl.program_id(1) == 0)
    def _():
        wb_ref[...] = w_ref[...].astype(BF16)

    o_ref[...] = jnp.dot(x_ref[...], wb_ref[...], preferred_element_type=F32).astype(o_ref.dtype)


def _project(xb, w3, layer, col_blk0, n_blks, tn, out_dtype, tm=1024):
    S, D = xb.shape
    return pl.pallas_call(
        _mm_kernel,
        grid=(n_blks, S // tm),
        in_specs=[pl.BlockSpec((tm, D), lambda n, m: (m, 0)),
                  pl.BlockSpec((None, D, tn), lambda n, m: (layer, 0, col_blk0 + n))],
        out_specs=pl.BlockSpec((tm, tn), lambda n, m: (m, n)),
        out_shape=jax.ShapeDtypeStruct((S, n_blks * tn), out_dtype),
        scratch_shapes=[pltpu.VMEM((D, tn), BF16)],
        compiler_params=_cparams(("arbitrary", "arbitrary"), 40),
        name="in_proj",
    )(xb, w3)


def _gelu_tanh(x):
    return 0.5 * x * (1.0 + jnp.tanh(math.sqrt(2.0 / math.pi) * (x + 0.044715 * (x * x * x))))


def _lru_kernel(xr_ref, xg_ref, cw_ref, cb_ref, wa_ref, ba_ref, wi_ref, bi_ref, lam_ref, g_ref,
                o_ref, xbuf, hc):
    i = pl.program_id(0)
    T = xr_ref.shape[0]
    W = xr_ref.shape[1]
    HALO = 8

    @pl.when(i == 0)
    def _():
        xbuf[0:HALO, :] = jnp.zeros((HALO, W), F32)
        hc[...] = jnp.zeros(hc.shape, F32)

    @pl.when(i > 0)
    def _():
        xbuf[0:HALO, :] = xbuf[T:T + HALO, :]

    xbuf[HALO:HALO + T, :] = xr_ref[...]
    cw = cw_ref[...]
    xc = cb_ref[...] + xbuf[HALO - 3:HALO - 3 + T, :] * cw[0:1]
    for k in range(1, CONV_WIDTH):
        xc = xc + xbuf[HALO - 3 + k:HALO - 3 + k + T, :] * cw[k:k + 1]

    xcb = xc.astype(BF16)
    ra, ia = [], []
    for n in range(LRU_BLOCKS):
        blk = xcb[:, n * LRU_BLOCK_W:(n + 1) * LRU_BLOCK_W]
        ra.append(jnp.dot(blk, wa_ref[n].astype(BF16), preferred_element_type=F32))
        ia.append(jnp.dot(blk, wi_ref[n].astype(BF16), preferred_element_type=F32))
    r = jax.nn.sigmoid(jnp.concatenate(ra, axis=1) + ba_ref[...])
    ig = jax.nn.sigmoid(jnp.concatenate(ia, axis=1) + bi_ref[...])

    z = -lam_ref[...]
    softplus = jnp.maximum(z, 0.0) + jnp.log(1.0 + jnp.exp(-jnp.abs(z)))
    log_a = -LRU_C * r * softplus
    a = jnp.exp(log_a)
    u = jnp.sqrt(1.0 - a * a) * (ig * xc)

    row = lax.broadcasted_iota(jnp.int32, (T, W), 0)
    A, U = a, u
    d = 1
    while d < T:
        keep = row >= d
        a_sh = jnp.where(keep, pltpu.roll(A, d, 0), 1.0)
        u_sh = jnp.where(keep, pltpu.roll(U, d, 0), 0.0)
        U = A * u_sh + U
        A = A * a_sh
        d *= 2
    h = U + A * hc[...]
    hc[...] = h[T - 1:T, :]

    y = h * _gelu_tanh(xg_ref[...])
    ms = jnp.mean(y * y, axis=-1, keepdims=True)
    o_ref[...] = (y * lax.rsqrt(ms + NORM_EPS) * g_ref[...]).astype(o_ref.dtype)


def _lru_group(u_f32, conv_w, conv_b, wa, ba, wi, bi, lam, g_lru, T=256):
    S = u_f32.shape[0]
    W = LRU_WIDTH
    row = lambda v: v.reshape(1, W)
    vec = pl.BlockSpec((1, W), lambda i: (0, 0))
    mat = pl.BlockSpec((LRU_BLOCKS, LRU_BLOCK_W, LRU_BLOCK_W), lambda i: (0, 0, 0))
    return pl.pallas_call(
        _lru_kernel,
        grid=(S // T,),
        in_specs=[pl.BlockSpec((T, W), lambda i: (i, 0)),
                  pl.BlockSpec((T, W), lambda i: (i, 1)),
                  pl.BlockSpec((CONV_WIDTH, W), lambda i: (0, 0)),
                  vec, mat, vec, mat, vec, vec, vec],
        out_specs=pl.BlockSpec((T, W), lambda i: (i, 0)),
        out_shape=jax.ShapeDtypeStruct((S, W), BF16),
        scratch_shapes=[pltpu.VMEM((T + 8, W), F32), pltpu.VMEM((1, W), F32)],
        compiler_params=_cparams(("arbitrary",), 48),
        name="rg_lru",
    )(u_f32, u_f32, conv_w, row(conv_b), wa, row(ba), wi, row(bi), row(lam), row(g_lru))


def _compress_kernel(x_ref, pos_ref, w1_ref, w2_ref, o_ref):
    NJ = x_ref.shape[0]
    half = (CMP_BLOCK // 2) * HEAD_DIM
    x = x_ref[...].astype(F32)
    a = (x + pos_ref[0:1, :]).astype(BF16)
    b = (x + pos_ref[1:2, :]).astype(BF16)
    y1 = jnp.dot(a, w1_ref[0:half, :].astype(BF16), preferred_element_type=F32)
    y2 = jnp.dot(b, w1_ref[half:2 * half, :].astype(BF16), preferred_element_type=F32)
    hmid = y1 + pltpu.roll(y2, NJ - 1, 0)
    o_ref[...] = jnp.dot(_gelu_tanh(hmid).astype(BF16), w2_ref[...].astype(BF16),
                         preferred_element_type=F32).astype(o_ref.dtype)


def _compress(kv_rows, pos, w1, w2):
    _, NJ, RW = kv_rows.shape
    return pl.pallas_call(
        _compress_kernel,
        grid=(4,),
        in_specs=[pl.BlockSpec((None, NJ, RW), lambda a: (a, 0, 0)),
                  pl.BlockSpec((None, 2, RW), lambda a: (a // 2, 0, 0)),
                  pl.BlockSpec((None, CMP_BLOCK * HEAD_DIM, HEAD_DIM), lambda a: (a // 2, 0, 0)),
                  pl.BlockSpec((None, HEAD_DIM, HEAD_DIM), lambda a: (a // 2, 0, 0))],
        out_specs=pl.BlockSpec((None, NJ, HEAD_DIM), lambda a: (a, 0, 0)),
        out_shape=jax.ShapeDtypeStruct((4, NJ, HEAD_DIM), BF16),
        compiler_params=_cparams(("arbitrary",), 32),
        name="kv_compress",
    )(kv_rows, pos.reshape(2, 2, RW), w1, w2)


def _cmp_attn_kernel(rb_ref, q_ref, kcv_ref, oc_ref, sel_ref, tcat):
    qb = pl.program_id(0)
    t0 = qb * Q_BLOCK
    NJ = kcv_ref.shape[1]
    NB = NJ * CMP_STRIDE // SEL_BLOCK
    NBP = sel_ref.shape[2]
    n_sel = min(N_SEL, NB)
    LOCAL = 2 * Q_BLOCK // CMP_STRIDE

    @pl.when(qb == 0)
    def _():
        i = lax.broadcasted_iota(jnp.int32, (Q_BLOCK, LANES), 0)
        j = lax.broadcasted_iota(jnp.int32, (Q_BLOCK, LANES), 1)
        d = jnp.maximum(i - CMP_STRIDE * j + (Q_BLOCK - CMP_STRIDE), 0)
        for h in range(NSA_HEADS):
            corr = jnp.where(j < LOCAL, (_bias_of_dist(rb_ref, h, d) - rb_ref[N_BUCKETS - 1, h]) * LOG2E, 0.0)
            hi = corr.astype(BF16)
            tcat[h, :, 0:LANES] = hi
            tcat[h, :, LANES:2 * LANES] = (corr - hi.astype(F32)).astype(BF16)

    def tile(W):
        row = lax.broadcasted_iota(jnp.int32, (Q_BLOCK, W), 0)
        col = lax.broadcasted_iota(jnp.int32, (Q_BLOCK, W), 1)
        mask_add = jnp.where(CMP_STRIDE * col + (CMP_BLOCK - 1) <= t0 + row, 0.0, NEG_INF)
        jj = lax.broadcasted_iota(jnp.int32, (2 * LANES, W), 0) % LANES
        cc = lax.broadcasted_iota(jnp.int32, (2 * LANES, W), 1)
        place = (cc == (Q_BLOCK // CMP_STRIDE) * (qb - 1) + jj).astype(F32).astype(BF16)
        bj = lax.broadcasted_iota(jnp.int32, (NB, W), 0)
        bc = lax.broadcasted_iota(jnp.int32, (NB, W), 1)
        ratio = SEL_BLOCK // CMP_STRIDE
        overlap_t = ((bc <= ratio * bj + ratio - 1) & (bc >= ratio * bj - 1)).astype(F32).astype(BF16)

        sj = lax.broadcasted_iota(jnp.int32, (NB, Q_BLOCK), 0)
        st = t0 + lax.broadcasted_iota(jnp.int32, (NB, Q_BLOCK), 1)
        cur = st // SEL_BLOCK
        blk_ok = sj <= cur
        forced = (sj == 0) | (sj == cur) | (sj == cur - 1)

        scores = []
        for g in range(NSA_KV_HEADS):
            kc = kcv_ref[g, 0:W, :]
            vc = kcv_ref[NSA_KV_HEADS + g, 0:W, :]
            psum = jnp.zeros((Q_BLOCK, W), F32)
            for h in range(HEADS_PER_GROUP):
                hh = g * HEADS_PER_GROUP + h
                qh = (q_ref[:, hh * HEAD_DIM:(hh + 1) * HEAD_DIM].astype(F32) * (ATTN_SCALE * LOG2E)).astype(BF16)
                z = _nt_dot(qh, kc) + jnp.dot(tcat[hh], place, preferred_element_type=F32) + mask_add
                m = jnp.maximum(jnp.max(z, axis=-1, keepdims=True), M_INIT)
                e = jnp.exp2(z - m)
                l = jnp.sum(e, axis=-1, keepdims=True)
                p = e * (1.0 / jnp.where(l > 0.0, l, 1.0))
                oc_ref[:, hh * HEAD_DIM:(hh + 1) * HEAD_DIM] = jnp.dot(
                    p.astype(BF16), vc, preferred_element_type=F32)
                psum = psum + p
            p_hi = psum.astype(BF16)
            p_lo = (psum - p_hi.astype(F32)).astype(BF16)
            imp_t = _nt_dot(overlap_t, p_hi) + _nt_dot(overlap_t, p_lo)
            scores.append(jnp.where(blk_ok, jnp.where(forced, FORCED_SCORE, imp_t), NEG_INF))

        score = jnp.concatenate(scores, axis=1)
        sjf = jnp.concatenate([sj.astype(F32)] * NSA_KV_HEADS, axis=1)
        chosen = jnp.zeros(score.shape, jnp.bool_)
        for _ in range(n_sel):
            mx = jnp.max(score, axis=0, keepdims=True)
            first = jnp.min(jnp.where(score == mx, sjf, float(NB)), axis=0, keepdims=True)
            pick = sjf == first
            chosen = chosen | pick
            score = jnp.where(pick, -3e38, score)
        picked = jnp.where(chosen, 1.0, 0.0)
        for g in range(NSA_KV_HEADS):
            pen_t = jnp.where((picked[:, g * Q_BLOCK:(g + 1) * Q_BLOCK] > 0.5) & blk_ok, 0.0, NEG_INF)
            padded = jnp.concatenate([jnp.full((PAD_BLK, Q_BLOCK), NEG_INF, F32), pen_t,
                                      jnp.zeros((NBP - PAD_BLK - NB, Q_BLOCK), F32)], axis=0)
            pen = jnp.concatenate([padded[r:r + LANES].T for r in range(0, NBP, LANES)], axis=1)
            sel_ref[g] = pen.astype(sel_ref.dtype)

    step = min(LANES, NJ)
    need = (Q_BLOCK // CMP_STRIDE) * qb + (Q_BLOCK - CMP_BLOCK) // CMP_STRIDE + 1
    width = jnp.minimum((need + step - 1) // step * step, NJ)
    for W in range(step, NJ + 1, step):
        pl.when(width == W)(functools.partial(tile, W))


def _cmp_attention(rel_bias, u_bf, kcv):
    S = u_bf.shape[0]
    NJ = kcv.shape[1]
    NB = -(-(S // SEL_BLOCK + PAD_BLK) // LANES) * LANES
    return pl.pallas_call(
        _cmp_attn_kernel,
        grid=(S // Q_BLOCK,),
        in_specs=[pl.BlockSpec(memory_space=pltpu.SMEM),
                  pl.BlockSpec((Q_BLOCK, NSA_WIDTH), lambda i: (i, 0)),
                  pl.BlockSpec((4, NJ, HEAD_DIM), lambda i: (0, 0, 0))],
        out_specs=[pl.BlockSpec((Q_BLOCK, NSA_WIDTH), lambda i: (i, 0)),
                   pl.BlockSpec((NSA_KV_HEADS, Q_BLOCK, NB), lambda i: (0, i, 0))],
        out_shape=[jax.ShapeDtypeStruct((S, NSA_WIDTH), F32),
                   jax.ShapeDtypeStruct((NSA_KV_HEADS, S, NB), BF16)],
        scratch_shapes=[pltpu.VMEM((NSA_HEADS, Q_BLOCK, 2 * LANES), BF16)],
        compiler_params=_cparams(("arbitrary",), 32),
        name="nsa_compressed",
    )(rel_bias, u_bf, kcv)


def _sel_attn_kernel(rb_ref, q_ref, ks_ref, vs_ref, kw_ref, vw_ref, sel_ref, oc_ref, gl_ref, g_ref,
                     o_ref, tbl_s, tbl_w, yacc):
    qb = pl.program_id(0)
    Q = Q_BLOCK
    HPG = HEADS_PER_GROUP
    TW = WINDOW + Q
    FAR = FAR_CHUNK
    NBP = sel_ref.shape[2]
    t0 = qb * Q

    @pl.when(qb == 0)
    def _():
        i = lax.broadcasted_iota(jnp.int32, (Q, TW), 0)
        c = lax.broadcasted_iota(jnp.int32, (Q, TW), 1)
        d = i - c + WINDOW
        dd = jnp.maximum(d, 0)
        for h in range(NSA_HEADS):
            corr = (_bias_of_dist(rb_ref, h, dd) - rb_ref[N_BUCKETS - 1, h]) * LOG2E
            tbl_s[h] = jnp.where(d < 0, NEG_INF, corr)
            tbl_w[h] = jnp.where((d < 0) | (d >= WINDOW), NEG_INF, corr)

    gates = jax.nn.sigmoid(gl_ref[...])
    col_w = lax.broadcasted_iota(jnp.int32, (1, TW), 1)
    before_start = jnp.where(col_w < WINDOW - t0, NEG_INF, 0.0)
    p_tail = pl.multiple_of(t0 + (KV_FRONT - WINDOW), Q)

    G = NSA_KV_HEADS
    q4 = [jnp.concatenate(
        [(q_ref[:, (g * HPG + h) * HEAD_DIM:(g * HPG + h + 1) * HEAD_DIM].astype(F32)
          * (ATTN_SCALE * LOG2E)).astype(BF16) for h in range(HPG)], axis=0) for g in range(G)]

    def attend(g, k_ref, v_ref, p0, width, add, carry):
        m, l, acc = carry
        cols = slice(g * HEAD_DIM, (g + 1) * HEAD_DIM)
        kk = k_ref[pl.ds(p0, width), cols]
        vv = v_ref[pl.ds(p0, width), cols]
        s = _nt_dot(q4[g], kk)
        z = (s.reshape(HPG, Q, width) + add).reshape(HPG * Q, width)
        m_new = jnp.maximum(m, jnp.max(z, axis=-1, keepdims=True))
        alpha = jnp.exp2(m - m_new)
        e = jnp.exp2(z - m_new)
        l = alpha * l + jnp.sum(e, axis=-1, keepdims=True)
        acc = alpha * acc + jnp.dot(e.astype(BF16), vv, preferred_element_type=F32)
        return m_new, l, acc

    def block_mask(g, p0, width):
        bp = lax.broadcasted_iota(jnp.int32, (NBP, width), 0)
        kb = lax.broadcasted_iota(jnp.int32, (NBP, width), 1) // SEL_BLOCK
        expand = (bp - p0 // SEL_BLOCK == kb).astype(F32).astype(BF16)
        return jnp.dot(sel_ref[g], expand, preferred_element_type=F32)[None]

    init = (jnp.full((HPG * Q, 1), M_INIT, F32), jnp.zeros((HPG * Q, 1), F32),
            jnp.zeros((HPG * Q, HEAD_DIM), F32))

    def far_body(i, carry):
        p0 = pl.multiple_of(p_tail - FAR * (i + 1), Q)
        return tuple(attend(g, ks_ref, vs_ref, p0, FAR, block_mask(g, p0, FAR), carry[g]) for g in range(G))

    n_far = (jnp.maximum(t0 - WINDOW, 0) + FAR - 1) // FAR
    far = lax.fori_loop(0, n_far, far_body, (init,) * G)

    for g in range(G):
        heads = slice(g * HPG, (g + 1) * HPG)
        m_s, l_s, acc_s = attend(g, ks_ref, vs_ref, p_tail, TW, tbl_s[heads] + block_mask(g, p_tail, TW), far[g])
        o_s = acc_s / jnp.where(l_s > 0.0, l_s, 1.0)

        m_w, l_w, acc_w = attend(g, kw_ref, vw_ref, p_tail, TW, tbl_w[heads] + before_start[None], init)
        o_w = acc_w / jnp.where(l_w > 0.0, l_w, 1.0)

        for h in range(HPG):
            hh = g * HPG + h
            hc = slice(hh * HEAD_DIM, (hh + 1) * HEAD_DIM)
            rows = slice(h * Q, (h + 1) * Q)
            yacc[:, hc] = (gates[:, 3 * hh:3 * hh + 1] * oc_ref[:, hc]
                           + gates[:, 3 * hh + 1:3 * hh + 2] * o_s[rows]
                           + gates[:, 3 * hh + 2:3 * hh + 3] * o_w[rows])

    y = yacc[...]
    ms = jnp.mean(y * y, axis=-1, keepdims=True)
    o_ref[...] = (y * lax.rsqrt(ms + NORM_EPS) * g_ref[...]).astype(o_ref.dtype)


def _sel_attention(rel_bias, u_bf, kv_pad, sel, o_c, gate_logits, g_nsa):
    S = u_bf.shape[0]
    NB = sel.shape[2]

    def kv_spec(j):
        return pl.BlockSpec((KV_FRONT + S, KV_WIDTH), lambda i: (0, j))

    return pl.pallas_call(
        _sel_attn_kernel,
        grid=(S // Q_BLOCK,),
        in_specs=[pl.BlockSpec(memory_space=pltpu.SMEM),
                  pl.BlockSpec((Q_BLOCK, NSA_WIDTH), lambda i: (i, 0)),
                  kv_spec(0), kv_spec(1), kv_spec(2), kv_spec(3),
                  pl.BlockSpec((NSA_KV_HEADS, Q_BLOCK, NB), lambda i: (0, i, 0)),
                  pl.BlockSpec((Q_BLOCK, NSA_WIDTH), lambda i: (i, 0)),
                  pl.BlockSpec((Q_BLOCK, LANES), lambda i: (i, 0)),
                  pl.BlockSpec((1, NSA_WIDTH), lambda i: (0, 0))],
        out_specs=pl.BlockSpec((Q_BLOCK, NSA_WIDTH), lambda i: (i, 0)),
        out_shape=jax.ShapeDtypeStruct((S, NSA_WIDTH), BF16),
        scratch_shapes=[pltpu.VMEM((NSA_HEADS, Q_BLOCK, WINDOW + Q_BLOCK), F32),
                        pltpu.VMEM((NSA_HEADS, Q_BLOCK, WINDOW + Q_BLOCK), F32),
                        pltpu.VMEM((Q_BLOCK, NSA_WIDTH), F32)],
        compiler_params=_cparams(("arbitrary",), 60),
        name="nsa_selected_window",
    )(rel_bias, u_bf, kv_pad, kv_pad, kv_pad, kv_pad, sel, o_c, gate_logits, g_nsa.reshape(1, NSA_WIDTH))


def _layer_norm_rows(z, g, b):
    mu = jnp.mean(z, axis=-1, keepdims=True)
    zc = z - mu
    var = jnp.mean(zc * zc, axis=-1, keepdims=True)
    return zc * lax.rsqrt(var + NORM_EPS) * g + b


def _outproj_kernel(yl_ref, yn_ref, w_ref, x_ref, g_ref, b_ref, o_ref, or_ref):
    tm = x_ref.shape[0]
    acc = jnp.dot(yl_ref[...], w_ref[0:LRU_WIDTH, :], preferred_element_type=F32)
    acc = acc + jnp.dot(yn_ref[...], w_ref[LRU_WIDTH:LRU_WIDTH + NSA_WIDTH, :], preferred_element_type=F32)
    y = _layer_norm_rows(DN_ALPHA * x_ref[...] + acc, g_ref[...], b_ref[...])
    o_ref[...] = y
    for c in range(ROW_TILES):
        or_ref[pl.ds(c, tm, stride=ROW_TILES), :] = y[:, c * LANES:(c + 1) * LANES]


def _out_proj_ln(y_lru, y_nsa, w_out_b, x, g, b, tm=256):
    S, D = x.shape
    vec = pl.BlockSpec((1, D), lambda i: (0, 0))
    return pl.pallas_call(
        _outproj_kernel,
        grid=(S // tm,),
        in_specs=[pl.BlockSpec((tm, LRU_WIDTH), lambda i: (i, 0)),
                  pl.BlockSpec((tm, NSA_WIDTH), lambda i: (i, 0)),
                  pl.BlockSpec((LRU_WIDTH + NSA_WIDTH, D), lambda i: (0, 0)),
                  pl.BlockSpec((tm, D), lambda i: (i, 0)), vec, vec],
        out_specs=[pl.BlockSpec((tm, D), lambda i: (i, 0)),
                   pl.BlockSpec((tm * ROW_TILES, LANES), lambda i: (i, 0))],
        out_shape=[jax.ShapeDtypeStruct((S, D), F32), jax.ShapeDtypeStruct((S * ROW_TILES, LANES), F32)],
        compiler_params=_cparams(("arbitrary",), 48),
        name="out_proj_ln",
    )(y_lru, y_nsa, w_out_b, x, g.reshape(1, D), b.reshape(1, D))


def _router_kernel(x_ref, w_ref, b_ref, e_ref, g_ref, r_ref, cnt_ref, carry):
    i = pl.program_id(0)
    tm = x_ref.shape[0]

    @pl.when(i == 0)
    def _():
        carry[...] = jnp.zeros(carry.shape, F32)

    logits = jnp.dot(x_ref[...], w_ref[...], preferred_element_type=F32,
                     precision=lax.Precision.HIGHEST) + b_ref[...]
    lane = lax.broadcasted_iota(jnp.int32, (tm, LANES), 1)
    lanef = lane.astype(F32)
    s = logits
    picks = []
    chosen = jnp.zeros((tm, LANES), jnp.bool_)
    for _ in range(TOP_K):
        mx = jnp.max(s, axis=-1, keepdims=True)
        first = jnp.min(jnp.where(s == mx, lanef, float(LANES)), axis=-1, keepdims=True)
        pick = lanef == first
        picks.append((pick, first))
        chosen = chosen | pick
        s = jnp.where(pick, -3e38, s)
    top = jnp.max(logits, axis=-1, keepdims=True)
    ex = jnp.where(chosen, jnp.exp(logits - top), 0.0)
    gate = ex / jnp.sum(ex, axis=-1, keepdims=True)

    onehot = jnp.where(chosen, 1.0, 0.0)
    rr = lax.broadcasted_iota(jnp.int32, (tm, tm), 0)
    rc = lax.broadcasted_iota(jnp.int32, (tm, tm), 1)
    before = (rr > rc).astype(F32).astype(BF16)
    rank = jnp.dot(before, onehot.astype(BF16), preferred_element_type=F32) + carry[0:1, :]
    carry[0:1, :] = carry[0:1, :] + jnp.sum(onehot, axis=0, keepdims=True)
    cnt_ref[...] = carry[...]

    e_out = jnp.zeros((tm, LANES), jnp.int32)
    g_out = jnp.zeros((tm, LANES), F32)
    r_out = jnp.zeros((tm, LANES), F32)
    for k, (pick, first) in enumerate(picks):
        e_out = jnp.where(lane == k, first.astype(jnp.int32), e_out)
        g_out = jnp.where(lane == k, jnp.sum(jnp.where(pick, gate, 0.0), axis=-1, keepdims=True), g_out)
        r_out = jnp.where(lane == k, jnp.sum(jnp.where(pick, rank, 0.0), axis=-1, keepdims=True), r_out)
    e_ref[...] = e_out
    g_ref[...] = g_out
    r_ref[...] = r_out


def _router(x, w_pad, b_pad, tm=256):
    S, D = x.shape
    out = pl.BlockSpec((tm, LANES), lambda i: (i, 0))
    return pl.pallas_call(
        _router_kernel,
        grid=(S // tm,),
        in_specs=[pl.BlockSpec((tm, D), lambda i: (i, 0)),
                  pl.BlockSpec((D, LANES), lambda i: (0, 0)),
                  pl.BlockSpec((1, LANES), lambda i: (0, 0))],
        out_specs=[out, out, out, pl.BlockSpec((8, LANES), lambda i: (0, 0))],
        out_shape=[jax.ShapeDtypeStruct((S, LANES), jnp.int32), jax.ShapeDtypeStruct((S, LANES), F32),
                   jax.ShapeDtypeStruct((S, LANES), F32), jax.ShapeDtypeStruct((8, LANES), F32)],
        scratch_shapes=[pltpu.VMEM((8, LANES), F32)],
        compiler_params=_cparams(("arbitrary",), 32),
        name="moe_router",
    )(x, w_pad, b_pad)


def _expert_kernel(sb_ref, se_ref, sf_ref, sj_ref, so_ref, sv_ref, tok_ref,
                   x_hbm, wgu_ref, bgu_ref, wdn_ref, bdn_ref, o_ref, wgu_b, wdn_b, acc, stage, xrows, sem):
    s = pl.program_id(0)
    n_steps = pl.num_programs(0)
    nrow = sv_ref[s]
    valid = nrow > 0
    f = sf_ref[s]
    j = sj_ref[s]
    SUB = MOE_SUB
    FT2 = wgu_ref.shape[1]
    RT = ROW_TILES
    UNROLL = 8

    def gather_rows(step):
        base = sb_ref[step] * SUB
        slot = sj_ref[step] % 2

        def issue(i, carry):
            for u in range(UNROLL):
                r = i * UNROLL + u
                t = tok_ref[base + r]
                pltpu.make_async_copy(x_hbm.at[t], stage.at[slot, :, r, :], sem.at[slot]).start()
            return carry

        lax.fori_loop(0, sv_ref[step] // UNROLL, issue, 0)

    @pl.when(s == 0)
    def _():
        gather_rows(0)

    nxt = jnp.minimum(s + 1, n_steps - 1)

    @pl.when((s + 1 < n_steps) & (sv_ref[nxt] > 0) & (sf_ref[nxt] == 0))
    def _():
        gather_rows(nxt)

    @pl.when(valid & (j == 0))
    def _():
        wgu_b[...] = wgu_ref[...].astype(BF16)
        wdn_b[...] = wdn_ref[...].astype(BF16)

    @pl.when(jnp.logical_not(valid))
    def _():
        o_ref[...] = jnp.zeros(o_ref.shape, F32)

    def process(m):
        rows = pl.ds(pl.multiple_of(j * SUB, SUB), m)

        @pl.when(f == 0)
        def _():
            slot = j % 2
            filled = stage.at[slot, :, pl.ds(0, m), :]
            pltpu.make_async_copy(filled, filled, sem.at[slot]).wait()
            for c in range(RT):
                xrows[rows, c * LANES:(c + 1) * LANES] = stage[slot, c, 0:m, :].astype(BF16)

        gu = jnp.dot(xrows[rows, :], wgu_b[...], preferred_element_type=F32) + bgu_ref[...]
        gate = jnp.minimum(gu, SWIGLU_LIMIT)
        act = gate * jax.nn.sigmoid(SWIGLU_ALPHA * gate)
        up1 = jnp.clip(gu, -SWIGLU_LIMIT, SWIGLU_LIMIT) + 1.0
        pr = lax.broadcasted_iota(jnp.int32, (2 * LANES, LANES), 0)
        pc = lax.broadcasted_iota(jnp.int32, (2 * LANES, LANES), 1)
        pick_even = (pr == 2 * pc).astype(F32).astype(BF16)
        hs = []
        for c in range(FT2 // (2 * LANES)):
            parts = []
            for k in range(2):
                cs = slice((2 * c + k) * LANES, (2 * c + k + 1) * LANES)
                parts.append((act[:, cs] * pltpu.roll(up1[:, cs], LANES - 1, 1)).astype(BF16))
            pair = jnp.concatenate(parts, axis=1)
            hs.append(jnp.dot(pair, pick_even, preferred_element_type=F32).astype(BF16))
        h = jnp.concatenate(hs, axis=1)
        y = jnp.dot(h, wdn_b[...], preferred_element_type=F32)

        @pl.when(f == 0)
        def _():
            acc[rows, :] = y + bdn_ref[...]

        @pl.when((f > 0) & (f < MOE_NF - 1))
        def _():
            acc[rows, :] = acc[rows, :] + y

        @pl.when(f == MOE_NF - 1)
        def _():
            yfin = acc[rows, :] + y
            for c in range(RT):
                o_ref[pl.ds(c, m, stride=RT), :] = yfin[:, c * LANES:(c + 1) * LANES]
            if m < SUB:
                o_ref[pl.ds(m * RT, (SUB - m) * RT), :] = jnp.zeros(((SUB - m) * RT, LANES), F32)

    for m in range(MOE_ROWQ, SUB + 1, MOE_ROWQ):
        pl.when(nrow == m)(functools.partial(process, m))


def _experts(meta, tok, x_rows, w_gu, b_gu3, w_dn, b_dn3, layer):
    sb, se, sf, sj, so, sv = meta
    NS = sb.shape[0]
    R = tok.shape[0]
    D = D_MODEL
    RT = ROW_TILES

    def wmap(fn):
        return lambda s, sb, se, sf, sj, so, sv, tok: fn(s, se, sf, so)

    grid_spec = pltpu.PrefetchScalarGridSpec(
        num_scalar_prefetch=7,
        grid=(NS,),
        in_specs=[
            pl.BlockSpec(memory_space=pl.ANY),
            pl.BlockSpec((None, None, D, 2 * MOE_FT), wmap(lambda s, se, sf, so: (layer, se[s], 0, sf[s]))),
            pl.BlockSpec((None, None, 1, 2 * MOE_FT), wmap(lambda s, se, sf, so: (layer, se[s], 0, sf[s]))),
            pl.BlockSpec((None, None, MOE_FT, D), wmap(lambda s, se, sf, so: (layer, se[s], sf[s], 0))),
            pl.BlockSpec((None, None, 1, D), wmap(lambda s, se, sf, so: (layer, se[s], 0, 0))),
        ],
        out_specs=pl.BlockSpec((MOE_SUB * RT, LANES), wmap(lambda s, se, sf, so: (so[s], 0))),
        scratch_shapes=[pltpu.VMEM((D, 2 * MOE_FT), BF16), pltpu.VMEM((MOE_FT, D), BF16),
                        pltpu.VMEM((MOE_J * MOE_SUB, D), F32),
                        pltpu.VMEM((2, RT, MOE_SUB, LANES), F32),
                        pltpu.VMEM((MOE_J * MOE_SUB, D), BF16),
                        pltpu.SemaphoreType.DMA((2,))],
    )
    return pl.pallas_call(
        _expert_kernel,
        grid_spec=grid_spec,
        out_shape=jax.ShapeDtypeStruct((R * RT, LANES), F32),
        compiler_params=_cparams(("arbitrary",), 60),
        name="moe_experts",
    )(sb, se, sf, sj, so, sv, tok, x_rows, w_gu, b_gu3, w_dn, b_dn3)


def _plan_kernel(cnt_ref, sb, se, sf, sj, so, sv, sub_start, src_start, n_used):
    n_sub_max = src_start.shape[0]
    n_steps_max = sb.shape[0]

    def expert(e, carry):
        b0, g0, s = carry
        c = cnt_ref[e]
        ns = (c + MOE_SUB - 1) // MOE_SUB
        sub_start[e] = b0

        def sub_tile(lb, _):
            src_start[b0 + lb] = g0 + lb * MOE_SUB
            n_used[b0 + lb] = jnp.minimum(MOE_SUB, c - lb * MOE_SUB)
            return 0

        lax.fori_loop(0, ns, sub_tile, 0)

        def chunk(ci, s):
            cb0 = b0 + ci * MOE_J
            nj = jnp.minimum(MOE_J, ns - ci * MOE_J)

            def ff_tile(f, s):
                def slot(j, s):
                    rem = c - (ci * MOE_J + j) * MOE_SUB
                    sb[s] = cb0 + j
                    se[s] = e
                    sf[s] = f
                    sj[s] = j
                    so[s] = jnp.where(f == MOE_NF - 1, cb0 + j, cb0)
                    sv[s] = jnp.clip((rem + MOE_ROWQ - 1) // MOE_ROWQ * MOE_ROWQ, MOE_ROWQ, MOE_SUB)
                    return s + 1

                return lax.fori_loop(0, nj, slot, s)

            return lax.fori_loop(0, MOE_NF, ff_tile, s)

        s = lax.fori_loop(0, (ns + MOE_J - 1) // MOE_J, chunk, s)
        return b0 + ns, g0 + c, s

    b_end, _, s_end = lax.fori_loop(0, N_EXPERTS, expert, (jnp.int32(0), jnp.int32(0), jnp.int32(0)))

    def unused_sub_tile(b, _):
        src_start[b] = 0
        n_used[b] = 0
        return 0

    lax.fori_loop(b_end, n_sub_max, unused_sub_tile, 0)

    last = jnp.maximum(s_end - 1, 0)

    def unused_step(s, _):
        sb[s] = sb[last]
        se[s] = se[last]
        sf[s] = sf[last]
        sj[s] = sj[last]
        so[s] = jnp.minimum(b_end + (s - s_end), n_sub_max - 1)
        sv[s] = 0
        return 0

    lax.fori_loop(s_end, n_steps_max, unused_step, 0)


def _moe_tables(e_flat, r_flat, cnt, n_sub_max):
    NS = MOE_NF * n_sub_max
    smem = pl.BlockSpec(memory_space=pltpu.SMEM)
    i32 = lambda n: jax.ShapeDtypeStruct((n,), jnp.int32)
    *meta, sub_start, src_start, n_used = pl.pallas_call(
        _plan_kernel,
        in_specs=[smem],
        out_specs=[smem] * 9,
        out_shape=[i32(NS)] * 6 + [i32(N_EXPERTS), i32(n_sub_max), i32(n_sub_max)],
        name="moe_plan",
    )(cnt)
    dest = sub_start[e_flat] * MOE_SUB + r_flat

    n_asg = e_flat.shape[0]
    _, tok_sorted = lax.sort_key_val(dest, jnp.arange(n_asg, dtype=jnp.int32) // TOP_K)
    lane = jnp.arange(MOE_SUB, dtype=jnp.int32)[None, :]
    src = jnp.minimum(src_start[:, None] + lane, n_asg - 1)
    tok = jnp.where(lane < n_used[:, None], tok_sorted[src], 0).reshape(-1)
    return dest, tok, tuple(meta)


def _combine_kernel(dest_ref, y_hbm, g4_ref, x_ref, g_ref, b_ref, o_ref, ob_ref, stage, ytile, ysum, sem):
    i = pl.program_id(0)
    n = pl.num_programs(0)
    tm = x_ref.shape[0]
    RT = ROW_TILES
    n_rows = tm * TOP_K

    def gather_rows(step, slot):
        base = step * n_rows

        def issue(r, carry):
            d = dest_ref[base + r]
            pltpu.make_async_copy(y_hbm.at[pl.ds(pl.multiple_of(d * RT, RT), RT), :],
                                  stage.at[slot, pl.ds(pl.multiple_of(r * RT, RT), RT), :],
                                  sem.at[slot]).start()
            return carry

        lax.fori_loop(0, n_rows, issue, 0, unroll=8)

    @pl.when(i == 0)
    def _():
        gather_rows(0, 0)

    @pl.when(i + 1 < n)
    def _():
        gather_rows(i + 1, (i + 1) % 2)

    slot = i % 2
    pltpu.make_async_copy(y_hbm.at[pl.ds(0, n_rows * RT), :], stage.at[slot], sem.at[slot]).wait()

    def token_sum(t, carry):
        r0 = pl.multiple_of(t * (TOP_K * RT), TOP_K * RT)
        y = g4_ref[TOP_K * t] * stage[slot, pl.ds(r0, RT), :]
        for k in range(1, TOP_K):
            y = y + g4_ref[TOP_K * t + k] * stage[slot, pl.ds(r0 + k * RT, RT), :]
        ytile[pl.ds(pl.multiple_of(t * RT, RT), RT), :] = y
        return carry

    lax.fori_loop(0, tm, token_sum, 0, unroll=4)
    for c in range(RT):
        ysum[:, c * LANES:(c + 1) * LANES] = ytile[pl.ds(c, tm, stride=RT), :]
    out = _layer_norm_rows(DN_ALPHA * x_ref[...] + ysum[...], g_ref[...], b_ref[...])
    o_ref[...] = out
    ob_ref[...] = out.astype(BF16)


def _combine_ln(dest_flat, y_rows, g4, x, g, b, tm=128):
    S, D = x.shape
    vec = pl.BlockSpec((1, D), lambda i, d: (0, 0))
    grid_spec = pltpu.PrefetchScalarGridSpec(
        num_scalar_prefetch=1,
        grid=(S // tm,),
        in_specs=[pl.BlockSpec(memory_space=pl.ANY),
                  pl.BlockSpec((tm * TOP_K,), lambda i, d: (i,), memory_space=pltpu.SMEM),
                  pl.BlockSpec((tm, D), lambda i, d: (i, 0)), vec, vec],
        out_specs=[pl.BlockSpec((tm, D), lambda i, d: (i, 0)), pl.BlockSpec((tm, D), lambda i, d: (i, 0))],
        scratch_shapes=[pltpu.VMEM((2, tm * TOP_K * ROW_TILES, LANES), F32),
                        pltpu.VMEM((tm * ROW_TILES, LANES), F32),
                        pltpu.VMEM((tm, D), F32),
                        pltpu.SemaphoreType.DMA((2,))],
    )
    return pl.pallas_call(
        _combine_kernel,
        grid_spec=grid_spec,
        out_shape=[jax.ShapeDtypeStruct((S, D), F32), jax.ShapeDtypeStruct((S, D), BF16)],
        compiler_params=_cparams(("arbitrary",), 48),
        name="moe_combine_ln",
    )(dest_flat, y_rows, g4, x, g.reshape(1, D), b.reshape(1, D))


def _nsa_group(u_bf, gate_logits, cmp_pos, cmp_w1, cmp_w2, rel_bias, g_nsa):
    S = u_bf.shape[0]
    G = NSA_KV_HEADS
    kvc = u_bf[:, NSA_WIDTH:NSA_WIDTH + 2 * KV_WIDTH]
    kv_rows = kvc.reshape(S, 2 * G, HEAD_DIM).transpose(1, 0, 2).reshape(2 * G, S // CMP_STRIDE,
                                                                          CMP_STRIDE * HEAD_DIM)
    kcv = _compress(kv_rows, cmp_pos, cmp_w1, cmp_w2)
    o_c, sel = _cmp_attention(rel_bias, u_bf, kcv)
    kv_pad = jnp.pad(u_bf[:, NSA_WIDTH + 2 * KV_WIDTH:], ((KV_FRONT, 0), (0, 0)))
    return _sel_attention(rel_bias, u_bf, kv_pad, sel, o_c, gate_logits, g_nsa)


def _moe(x, x_rows, w_router, b_router, w_gu, b_gu, w_dn, b_dn, ln_g, ln_b, layer):
    S, D = x.shape
    E = N_EXPERTS
    w_pad = jnp.pad(w_router, ((0, 0), (0, LANES - E)))
    b_pad = jnp.pad(b_router, (0, LANES - E), constant_values=NEG_INF).reshape(1, LANES)
    e_out, g_out, r_out, cnt_out = _router(x, w_pad, b_pad)
    e_flat = e_out[:, :TOP_K].reshape(-1)
    r_flat = r_out[:, :TOP_K].reshape(-1).astype(jnp.int32)
    cnt = cnt_out[0, :E].astype(jnp.int32)
    n_sub_max = (S * TOP_K) // MOE_SUB + E
    flat, tok, meta = _moe_tables(e_flat, r_flat, cnt, n_sub_max)
    y_rows = _experts(meta, tok, x_rows.reshape(S, ROW_TILES, LANES), w_gu, b_gu.reshape(DEPTH, E, 1, 2 * D_FF), w_dn,
                      b_dn.reshape(DEPTH, E, 1, D), layer)
    return _combine_ln(flat, y_rows, g_out[:, :TOP_K].reshape(-1), x, ln_g, ln_b)


def kernel(x, w_in, conv_w, conv_b, lru_wa, lru_ba, lru_wi, lru_bi, lru_lambda, cmp_pos, cmp_w1, cmp_w2,
           rel_bias, g_lru, g_nsa, w_out, ln1_g, ln1_b, w_router, b_router, w_gate_up, b_gate_up, w_down,
           b_down, ln2_g, ln2_b):
    B, S, D = x.shape
    assert B == 1
    x = x.reshape(S, D)
    xb = x.astype(BF16)
    n_gate = 3 * NSA_HEADS
    for l in range(DEPTH):
        u_f32 = _project(xb, w_in, l, 0, 2 * LRU_WIDTH // 512, 512, F32)
        u_bf = _project(xb, w_in, l, 2 * LRU_WIDTH // 512, (MAIN_COLS - 2 * LRU_WIDTH) // 512, 512, BF16)
        w_gate = jnp.pad(w_in[l, :, MAIN_COLS:], ((0, 0), (0, LANES - n_gate)))[None]
        gate_logits = _project(xb, w_gate, 0, 0, 1, LANES, F32)
        y_lru = _lru_group(u_f32, conv_w[l], conv_b[l], lru_wa[l], lru_ba[l], lru_wi[l], lru_bi[l],
                           lru_lambda[l], g_lru[l])
        y_nsa = _nsa_group(u_bf, gate_logits, cmp_pos[l], cmp_w1[l], cmp_w2[l], rel_bias, g_nsa[l])
        x, x_rows = _out_proj_ln(y_lru, y_nsa, w_out[l].astype(BF16), x, ln1_g[l], ln1_b[l])
        x, xb = _moe(x, x_rows, w_router[l], b_router[l], w_gate_up, b_gate_up, w_down, b_down,
                     ln2_g[l], ln2_b[l], l)
    return x.reshape(B, S, D)
```

```python
import functools
import math

import numpy as np
import jax
import jax.numpy as jnp
from jax import lax
from jax.experimental import pallas as pl
from jax.experimental.pallas import tpu as pltpu

D_MODEL = 2048
DEPTH = 2
LRU_WIDTH = 1024
LRU_BLOCKS = 8
LRU_BLOCK_W = LRU_WIDTH // LRU_BLOCKS
CONV_WIDTH = 4
LRU_C = 8.0
NSA_HEADS = 8
NSA_KV_HEADS = 2
HEADS_PER_GROUP = NSA_HEADS // NSA_KV_HEADS
HEAD_DIM = 128
NSA_WIDTH = NSA_HEADS * HEAD_DIM
KV_WIDTH = NSA_KV_HEADS * HEAD_DIM
CMP_BLOCK = 32
CMP_STRIDE = 16
SEL_BLOCK = 64
N_SEL = 16
WINDOW = 512
Q_BLOCK = 128
N_BUCKETS = 32
MAX_DISTANCE = 128
N_EXPERTS = 32
TOP_K = 4
D_FF = 2048
SWIGLU_LIMIT = 7.0
SWIGLU_ALPHA = 1.702
DN_ALPHA = (2 * DEPTH) ** 0.25
NORM_EPS = 1e-5
NEG_INF = -1e30
FORCED_SCORE = 1e9
ATTN_SCALE = HEAD_DIM ** -0.5

LOG2E = math.log2(math.e)
FAR_CHUNK = 1024
KV_FRONT = FAR_CHUNK
PAD_BLK = KV_FRONT // SEL_BLOCK
M_INIT = -1e9

LANES = 128
MIB = 1024 * 1024
MAIN_COLS = 2 * LRU_WIDTH + NSA_WIDTH + 6 * KV_WIDTH
BF16 = jnp.bfloat16
F32 = jnp.float32

MOE_SUB = 256
MOE_ROWQ = 64
MOE_J = 5
MOE_FT = 512
MOE_NF = D_FF // MOE_FT
ROW_TILES = D_MODEL // LANES


def _bucket_thresholds():
    max_exact = N_BUCKETS // 2
    out = []
    for b in range(1, N_BUCKETS):
        if b <= max_exact:
            out.append(b)
            continue
        d = max_exact
        while True:
            v = math.log(d / max_exact) / math.log(MAX_DISTANCE / max_exact) * (N_BUCKETS - max_exact)
            assert abs(v - round(v)) > 1e-3 or d == max_exact
            if min(max_exact + int(v), N_BUCKETS - 1) >= b:
                break
            d += 1
        out.append(d)
    return tuple(out)


BUCKET_TH = _bucket_thresholds()


def _cparams(semantics, vmem_mib):
    return pltpu.CompilerParams(dimension_semantics=semantics, vmem_limit_bytes=vmem_mib * MIB)


def _nt_dot(a, b):
    return lax.dot_general(a, b, (((1,), (1,)), ((), ())), preferred_element_type=F32)


def _bias_of_dist(rb_ref, head, d):
    b = jnp.full(d.shape, rb_ref[0, head], F32)
    for k in range(1, N_BUCKETS):
        b = jnp.where(d >= BUCKET_TH[k - 1], rb_ref[k, head], b)
    return b


def _mm_kernel(x_ref, w_ref, o_ref, wb_ref):
    @pl.when(pl.program_id(1) == 0)
    def _():
        wb_ref[...] = w_ref[...].astype(BF16)

    o_ref[...] = jnp.dot(x_ref[...], wb_ref[...], preferred_element_type=F32).astype(o_ref.dtype)


def _project(xb, w3, layer, col_blk0, n_blks, tn, out_dtype, tm=1024):
    S, D = xb.shape
    return pl.pallas_call(
        _mm_kernel,
        grid=(n_blks, S // tm),
        in_specs=[pl.BlockSpec((tm, D), lambda n, m: (m, 0)),
                  pl.BlockSpec((None, D, tn), lambda n, m: (layer, 0, col_blk0 + n))],
        out_specs=pl.BlockSpec((tm, tn), lambda n, m: (m, n)),
        out_shape=jax.ShapeDtypeStruct((S, n_blks * tn), out_dtype),
        scratch_shapes=[pltpu.VMEM((D, tn), BF16)],
        compiler_params=_cparams(("arbitrary", "arbitrary"), 40),
        name="in_proj",
    )(xb, w3)


def _gelu_tanh(x):
    return 0.5 * x * (1.0 + jnp.tanh(math.sqrt(2.0 / math.pi) * (x + 0.044715 * (x * x * x))))


def _lru_kernel(xr_ref, xg_ref, cw_ref, cb_ref, wa_ref, ba_ref, wi_ref, bi_ref, lam_ref, g_ref,
                o_ref, xbuf, hc):
    i = pl.program_id(0)
    T = xr_ref.shape[0]
    W = xr_ref.shape[1]
    HALO = 8

    @pl.when(i == 0)
    def _():
        xbuf[0:HALO, :] = jnp.zeros((HALO, W), F32)
        hc[...] = jnp.zeros(hc.shape, F32)

    @pl.when(i > 0)
    def _():
        xbuf[0:HALO, :] = xbuf[T:T + HALO, :]

    xbuf[HALO:HALO + T, :] = xr_ref[...]
    cw = cw_ref[...]
    xc = cb_ref[...] + xbuf[HALO - 3:HALO - 3 + T, :] * cw[0:1]
    for k in range(1, CONV_WIDTH):
        xc = xc + xbuf[HALO - 3 + k:HALO - 3 + k + T, :] * cw[k:k + 1]

    xcb = xc.astype(BF16)
    ra, ia = [], []
    for n in range(LRU_BLOCKS):
        blk = xcb[:, n * LRU_BLOCK_W:(n + 1) * LRU_BLOCK_W]
        ra.append(jnp.dot(blk, wa_ref[n].astype(BF16), preferred_element_type=F32))
        ia.append(jnp.dot(blk, wi_ref[n].astype(BF16), preferred_element_type=F32))
    r = jax.nn.sigmoid(jnp.concatenate(ra, axis=1) + ba_ref[...])
    ig = jax.nn.sigmoid(jnp.concatenate(ia, axis=1) + bi_ref[...])

    z = -lam_ref[...]
    softplus = jnp.maximum(z, 0.0) + jnp.log(1.0 + jnp.exp(-jnp.abs(z)))
    log_a = -LRU_C * r * softplus
    a = jnp.exp(log_a)
    u = jnp.sqrt(1.0 - a * a) * (ig * xc)

    SUBL = 8
    sub = lax.broadcasted_iota(jnp.int32, (T, W), 0) % SUBL
    A, U = a, u
    d = 1
    while d < SUBL:
        keep = sub >= d
        a_sh = jnp.where(keep, pltpu.roll(A, d, 0), 1.0)
        u_sh = jnp.where(keep, pltpu.roll(U, d, 0), 0.0)
        U = A * u_sh + U
        A = A * a_sh
        d *= 2
    state = hc[...]
    groups = []
    for k in range(T // SUBL):
        hg = U[k * SUBL:(k + 1) * SUBL] + A[k * SUBL:(k + 1) * SUBL] * state
        state = hg[SUBL - 1:SUBL]
        groups.append(hg)
    h = jnp.concatenate(groups, axis=0)
    hc[...] = state

    y = h * _gelu_tanh(xg_ref[...])
    ms = jnp.mean(y * y, axis=-1, keepdims=True)
    o_ref[...] = (y * lax.rsqrt(ms + NORM_EPS) * g_ref[...]).astype(o_ref.dtype)


def _lru_group(u_f32, conv_w, conv_b, wa, ba, wi, bi, lam, g_lru, T=256):
    S = u_f32.shape[0]
    W = LRU_WIDTH
    row = lambda v: v.reshape(1, W)
    vec = pl.BlockSpec((1, W), lambda i: (0, 0))
    mat = pl.BlockSpec((LRU_BLOCKS, LRU_BLOCK_W, LRU_BLOCK_W), lambda i: (0, 0, 0))
    return pl.pallas_call(
        _lru_kernel,
        grid=(S // T,),
        in_specs=[pl.BlockSpec((T, W), lambda i: (i, 0)),
                  pl.BlockSpec((T, W), lambda i: (i, 1)),
                  pl.BlockSpec((CONV_WIDTH, W), lambda i: (0, 0)),
                  vec, mat, vec, mat, vec, vec, vec],
        out_specs=pl.BlockSpec((T, W), lambda i: (i, 0)),
        out_shape=jax.ShapeDtypeStruct((S, W), BF16),
        scratch_shapes=[pltpu.VMEM((T + 8, W), F32), pltpu.VMEM((1, W), F32)],
        compiler_params=_cparams(("arbitrary",), 48),
        name="rg_lru",
    )(u_f32, u_f32, conv_w, row(conv_b), wa, row(ba), wi, row(bi), row(lam), row(g_lru))


def _compress_kernel(x_ref, pos_ref, w1_ref, w2_ref, o_ref):
    NJ = x_ref.shape[0]
    half = (CMP_BLOCK // 2) * HEAD_DIM
    x = x_ref[...].astype(F32)
    a = (x + pos_ref[0:1, :]).astype(BF16)
    b = (x + pos_ref[1:2, :]).astype(BF16)
    y1 = jnp.dot(a, w1_ref[0:half, :].astype(BF16), preferred_element_type=F32)
    y2 = jnp.dot(b, w1_ref[half:2 * half, :].astype(BF16), preferred_element_type=F32)
    hmid = y1 + pltpu.roll(y2, NJ - 1, 0)
    o_ref[...] = jnp.dot(_gelu_tanh(hmid).astype(BF16), w2_ref[...].astype(BF16),
                         preferred_element_type=F32).astype(o_ref.dtype)


def _compress(kv_rows, pos, w1, w2):
    _, NJ, RW = kv_rows.shape
    return pl.pallas_call(
        _compress_kernel,
        grid=(4,),
        in_specs=[pl.BlockSpec((None, NJ, RW), lambda a: (a, 0, 0)),
                  pl.BlockSpec((None, 2, RW), lambda a: (a // 2, 0, 0)),
                  pl.BlockSpec((None, CMP_BLOCK * HEAD_DIM, HEAD_DIM), lambda a: (a // 2, 0, 0)),
                  pl.BlockSpec((None, HEAD_DIM, HEAD_DIM), lambda a: (a // 2, 0, 0))],
        out_specs=pl.BlockSpec((None, NJ, HEAD_DIM), lambda a: (a, 0, 0)),
        out_shape=jax.ShapeDtypeStruct((4, NJ, HEAD_DIM), BF16),
        compiler_params=_cparams(("arbitrary",), 32),
        name="kv_compress",
    )(kv_rows, pos.reshape(2, 2, RW), w1, w2)


def _cmp_attn_kernel(rb_ref, q_ref, kcv_ref, oc_ref, sel_ref, tcat):
    qb = pl.program_id(0)
    t0 = qb * Q_BLOCK
    NJ = kcv_ref.shape[1]
    NB = NJ * CMP_STRIDE // SEL_BLOCK
    NBP = sel_ref.shape[2]
    n_sel = min(N_SEL, NB)
    LOCAL = 2 * Q_BLOCK // CMP_STRIDE

    @pl.when(qb == 0)
    def _():
        i = lax.broadcasted_iota(jnp.int32, (Q_BLOCK, LANES), 0)
        j = lax.broadcasted_iota(jnp.int32, (Q_BLOCK, LANES), 1)
        d = jnp.maximum(i - CMP_STRIDE * j + (Q_BLOCK - CMP_STRIDE), 0)
        for h in range(NSA_HEADS):
            corr = jnp.where(j < LOCAL, (_bias_of_dist(rb_ref, h, d) - rb_ref[N_BUCKETS - 1, h]) * LOG2E, 0.0)
            hi = corr.astype(BF16)
            tcat[h, :, 0:LANES] = hi
            tcat[h, :, LANES:2 * LANES] = (corr - hi.astype(F32)).astype(BF16)

    def tile(W):
        row = lax.broadcasted_iota(jnp.int32, (Q_BLOCK, W), 0)
        col = lax.broadcasted_iota(jnp.int32, (Q_BLOCK, W), 1)
        mask_add = jnp.where(CMP_STRIDE * col + (CMP_BLOCK - 1) <= t0 + row, 0.0, NEG_INF)
        jj = lax.broadcasted_iota(jnp.int32, (2 * LANES, W), 0) % LANES
        cc = lax.broadcasted_iota(jnp.int32, (2 * LANES, W), 1)
        place = (cc == (Q_BLOCK // CMP_STRIDE) * (qb - 1) + jj).astype(F32).astype(BF16)
        bj = lax.broadcasted_iota(jnp.int32, (NB, W), 0)
        bc = lax.broadcasted_iota(jnp.int32, (NB, W), 1)
        ratio = SEL_BLOCK // CMP_STRIDE
        overlap_t = ((bc <= ratio * bj + ratio - 1) & (bc >= ratio * bj - 1)).astype(F32).astype(BF16)

        sj = lax.broadcasted_iota(jnp.int32, (NB, Q_BLOCK), 0)
        st = t0 + lax.broadcasted_iota(jnp.int32, (NB, Q_BLOCK), 1)
        cur = st // SEL_BLOCK
        blk_ok = sj <= cur
        forced = (sj == 0) | (sj == cur) | (sj == cur - 1)

        scores = []
        for g in range(NSA_KV_HEADS):
            kc = kcv_ref[g, 0:W, :]
            vc = kcv_ref[NSA_KV_HEADS + g, 0:W, :]
            psum = jnp.zeros((Q_BLOCK, W), F32)
            for h in range(HEADS_PER_GROUP):
                hh = g * HEADS_PER_GROUP + h
                qh = (q_ref[:, hh * HEAD_DIM:(hh + 1) * HEAD_DIM].astype(F32) * (ATTN_SCALE * LOG2E)).astype(BF16)
                z = _nt_dot(qh, kc) + jnp.dot(tcat[hh], place, preferred_element_type=F32) + mask_add
                m = jnp.maximum(jnp.max(z, axis=-1, keepdims=True), M_INIT)
                e = jnp.exp2(z - m)
                l = jnp.sum(e, axis=-1, keepdims=True)
                p = e * (1.0 / jnp.where(l > 0.0, l, 1.0))
                oc_ref[:, hh * HEAD_DIM:(hh + 1) * HEAD_DIM] = jnp.dot(
                    p.astype(BF16), vc, preferred_element_type=F32)
                psum = psum + p
            p_hi = psum.astype(BF16)
            p_lo = (psum - p_hi.astype(F32)).astype(BF16)
            imp_t = _nt_dot(overlap_t, p_hi) + _nt_dot(overlap_t, p_lo)
            scores.append(jnp.where(blk_ok, jnp.where(forced, FORCED_SCORE, imp_t), NEG_INF))

        score = jnp.concatenate(scores, axis=1)
        sjf = jnp.concatenate([sj.astype(F32)] * NSA_KV_HEADS, axis=1)
        chosen = jnp.zeros(score.shape, jnp.bool_)
        for _ in range(n_sel):
            mx = jnp.max(score, axis=0, keepdims=True)
            first = jnp.min(jnp.where(score == mx, sjf, float(NB)), axis=0, keepdims=True)
            pick = sjf == first
            chosen = chosen | pick
            score = jnp.where(pick, -3e38, score)
        picked = jnp.where(chosen, 1.0, 0.0)
        for g in range(NSA_KV_HEADS):
            pen_t = jnp.where((picked[:, g * Q_BLOCK:(g + 1) * Q_BLOCK] > 0.5) & blk_ok, 0.0, NEG_INF)
            padded = jnp.concatenate([jnp.full((PAD_BLK, Q_BLOCK), NEG_INF, F32), pen_t,
                                      jnp.zeros((NBP - PAD_BLK - NB, Q_BLOCK), F32)], axis=0)
            pen = jnp.concatenate([padded[r:r + LANES].T for r in range(0, NBP, LANES)], axis=1)
            sel_ref[g] = pen.astype(sel_ref.dtype)

    step = min(LANES, NJ)
    need = (Q_BLOCK // CMP_STRIDE) * qb + (Q_BLOCK - CMP_BLOCK) // CMP_STRIDE + 1
    width = jnp.minimum((need + step - 1) // step * step, NJ)
    for W in range(step, NJ + 1, step):
        pl.when(width == W)(functools.partial(tile, W))


def _cmp_attention(rel_bias, u_bf, kcv):
    S = u_bf.shape[0]
    NJ = kcv.shape[1]
    NB = -(-(S // SEL_BLOCK + PAD_BLK) // LANES) * LANES
    return pl.pallas_call(
        _cmp_attn_kernel,
        grid=(S // Q_BLOCK,),
        in_specs=[pl.BlockSpec(memory_space=pltpu.SMEM),
                  pl.BlockSpec((Q_BLOCK, NSA_WIDTH), lambda i: (i, 0)),
                  pl.BlockSpec((4, NJ, HEAD_DIM), lambda i: (0, 0, 0))],
        out_specs=[pl.BlockSpec((Q_BLOCK, NSA_WIDTH), lambda i: (i, 0)),
                   pl.BlockSpec((NSA_KV_HEADS, Q_BLOCK, NB), lambda i: (0, i, 0))],
        out_shape=[jax.ShapeDtypeStruct((S, NSA_WIDTH), F32),
                   jax.ShapeDtypeStruct((NSA_KV_HEADS, S, NB), BF16)],
        scratch_shapes=[pltpu.VMEM((NSA_HEADS, Q_BLOCK, 2 * LANES), BF16)],
        compiler_params=_cparams(("arbitrary",), 32),
        name="nsa_compressed",
    )(rel_bias, u_bf, kcv)


def _sel_attn_kernel(rb_ref, q_ref, ks_ref, vs_ref, kw_ref, vw_ref, sel_ref, oc_ref, gl_ref, g_ref,
                     o_ref, tbl_s, tbl_w, yacc):
    qb = pl.program_id(0)
    Q = Q_BLOCK
    HPG = HEADS_PER_GROUP
    TW = WINDOW + Q
    FAR = FAR_CHUNK
    NBP = sel_ref.shape[2]
    t0 = qb * Q

    @pl.when(qb == 0)
    def _():
        i = lax.broadcasted_iota(jnp.int32, (Q, TW), 0)
        c = lax.broadcasted_iota(jnp.int32, (Q, TW), 1)
        d = i - c + WINDOW
        dd = jnp.maximum(d, 0)
        for h in range(NSA_HEADS):
            corr = (_bias_of_dist(rb_ref, h, dd) - rb_ref[N_BUCKETS - 1, h]) * LOG2E
            tbl_s[h] = jnp.where(d < 0, NEG_INF, corr)
            tbl_w[h] = jnp.where((d < 0) | (d >= WINDOW), NEG_INF, corr)

    gates = jax.nn.sigmoid(gl_ref[...])
    col_w = lax.broadcasted_iota(jnp.int32, (1, TW), 1)
    before_start = jnp.where(col_w < WINDOW - t0, NEG_INF, 0.0)
    p_tail = pl.multiple_of(t0 + (KV_FRONT - WINDOW), Q)

    G = NSA_KV_HEADS
    q4 = [jnp.concatenate(
        [(q_ref[:, (g * HPG + h) * HEAD_DIM:(g * HPG + h + 1) * HEAD_DIM].astype(F32)
          * (ATTN_SCALE * LOG2E)).astype(BF16) for h in range(HPG)], axis=0) for g in range(G)]

    def attend(g, k_ref, v_ref, p0, width, add, carry):
        m, l, acc = carry
        cols = slice(g * HEAD_DIM, (g + 1) * HEAD_DIM)
        kk = k_ref[pl.ds(p0, width), cols]
        vv = v_ref[pl.ds(p0, width), cols]
        s = _nt_dot(q4[g], kk)
        z = (s.reshape(HPG, Q, width) + add).reshape(HPG * Q, width)
        m_new = jnp.maximum(m, jnp.max(z, axis=-1, keepdims=True))
        alpha = jnp.exp2(m - m_new)
        e = jnp.exp2(z - m_new)
        l = alpha * l + jnp.sum(e, axis=-1, keepdims=True)
        acc = alpha * acc + jnp.dot(e.astype(BF16), vv, preferred_element_type=F32)
        return m_new, l, acc

    def block_mask(g, p0, width):
        bp = lax.broadcasted_iota(jnp.int32, (NBP, width), 0)
        kb = lax.broadcasted_iota(jnp.int32, (NBP, width), 1) // SEL_BLOCK
        expand = (bp - p0 // SEL_BLOCK == kb).astype(F32).astype(BF16)
        return jnp.dot(sel_ref[g], expand, preferred_element_type=F32)[None]

    init = (jnp.full((HPG * Q, 1), M_INIT, F32), jnp.zeros((HPG * Q, 1), F32),
            jnp.zeros((HPG * Q, HEAD_DIM), F32))

    def far_body(i, carry):
        p0 = pl.multiple_of(p_tail - FAR * (i + 1), Q)
        return tuple(attend(g, ks_ref, vs_ref, p0, FAR, block_mask(g, p0, FAR), carry[g]) for g in range(G))

    n_far = (jnp.maximum(t0 - WINDOW, 0) + FAR - 1) // FAR
    far = lax.fori_loop(0, n_far, far_body, (init,) * G)

    for g in range(G):
        heads = slice(g * HPG, (g + 1) * HPG)
        m_s, l_s, acc_s = attend(g, ks_ref, vs_ref, p_tail, TW, tbl_s[heads] + block_mask(g, p_tail, TW), far[g])
        o_s = acc_s / jnp.where(l_s > 0.0, l_s, 1.0)

        m_w, l_w, acc_w = attend(g, kw_ref, vw_ref, p_tail, TW, tbl_w[heads] + before_start[None], init)
        o_w = acc_w / jnp.where(l_w > 0.0, l_w, 1.0)

        for h in range(HPG):
            hh = g * HPG + h
            hc = slice(hh * HEAD_DIM, (hh + 1) * HEAD_DIM)
            rows = slice(h * Q, (h + 1) * Q)
            yacc[:, hc] = (gates[:, 3 * hh:3 * hh + 1] * oc_ref[:, hc]
                           + gates[:, 3 * hh + 1:3 * hh + 2] * o_s[rows]
                           + gates[:, 3 * hh + 2:3 * hh + 3] * o_w[rows])

    y = yacc[...]
    ms = jnp.mean(y * y, axis=-1, keepdims=True)
    o_ref[...] = (y * lax.rsqrt(ms + NORM_EPS) * g_ref[...]).astype(o_ref.dtype)


def _sel_attention(rel_bias, u_bf, kv_pad, sel, o_c, gate_logits, g_nsa):
    S = u_bf.shape[0]
    NB = sel.shape[2]

    def kv_spec(j):
        return pl.BlockSpec((KV_FRONT + S, KV_WIDTH), lambda i: (0, j))

    return pl.pallas_call(
        _sel_attn_kernel,
        grid=(S // Q_BLOCK,),
        in_specs=[pl.BlockSpec(memory_space=pltpu.SMEM),
                  pl.BlockSpec((Q_BLOCK, NSA_WIDTH), lambda i: (i, 0)),
                  kv_spec(0), kv_spec(1), kv_spec(2), kv_spec(3),
                  pl.BlockSpec((NSA_KV_HEADS, Q_BLOCK, NB), lambda i: (0, i, 0)),
                  pl.BlockSpec((Q_BLOCK, NSA_WIDTH), lambda i: (i, 0)),
                  pl.BlockSpec((Q_BLOCK, LANES), lambda i: (i, 0)),
                  pl.BlockSpec((1, NSA_WIDTH), lambda i: (0, 0))],
        out_specs=pl.BlockSpec((Q_BLOCK, NSA_WIDTH), lambda i: (i, 0)),
        out_shape=jax.ShapeDtypeStruct((S, NSA_WIDTH), BF16),
        scratch_shapes=[pltpu.VMEM((NSA_HEADS, Q_BLOCK, WINDOW + Q_BLOCK), F32),
                        pltpu.VMEM((NSA_HEADS, Q_BLOCK, WINDOW + Q_BLOCK), F32),
                        pltpu.VMEM((Q_BLOCK, NSA_WIDTH), F32)],
        compiler_params=_cparams(("arbitrary",), 60),
        name="nsa_selected_window",
    )(rel_bias, u_bf, kv_pad, kv_pad, kv_pad, kv_pad, sel, o_c, gate_logits, g_nsa.reshape(1, NSA_WIDTH))


def _layer_norm_rows(z, g, b):
    mu = jnp.mean(z, axis=-1, keepdims=True)
    zc = z - mu
    var = jnp.mean(zc * zc, axis=-1, keepdims=True)
    return zc * lax.rsqrt(var + NORM_EPS) * g + b


def _outproj_kernel(yl_ref, yn_ref, w_ref, x_ref, g_ref, b_ref, o_ref, or_ref):
    tm = x_ref.shape[0]
    acc = jnp.dot(yl_ref[...], w_ref[0:LRU_WIDTH, :], preferred_element_type=F32)
    acc = acc + jnp.dot(yn_ref[...], w_ref[LRU_WIDTH:LRU_WIDTH + NSA_WIDTH, :], preferred_element_type=F32)
    y = _layer_norm_rows(DN_ALPHA * x_ref[...] + acc, g_ref[...], b_ref[...])
    o_ref[...] = y
    for c in range(ROW_TILES):
        or_ref[pl.ds(c, tm, stride=ROW_TILES), :] = y[:, c * LANES:(c + 1) * LANES]


def _out_proj_ln(y_lru, y_nsa, w_out_b, x, g, b, tm=256):
    S, D = x.shape
    vec = pl.BlockSpec((1, D), lambda i: (0, 0))
    return pl.pallas_call(
        _outproj_kernel,
        grid=(S // tm,),
        in_specs=[pl.BlockSpec((tm, LRU_WIDTH), lambda i: (i, 0)),
                  pl.BlockSpec((tm, NSA_WIDTH), lambda i: (i, 0)),
                  pl.BlockSpec((LRU_WIDTH + NSA_WIDTH, D), lambda i: (0, 0)),
                  pl.BlockSpec((tm, D), lambda i: (i, 0)), vec, vec],
        out_specs=[pl.BlockSpec((tm, D), lambda i: (i, 0)),
                   pl.BlockSpec((tm * ROW_TILES, LANES), lambda i: (i, 0))],
        out_shape=[jax.ShapeDtypeStruct((S, D), F32), jax.ShapeDtypeStruct((S * ROW_TILES, LANES), F32)],
        compiler_params=_cparams(("arbitrary",), 48),
        name="out_proj_ln",
    )(y_lru, y_nsa, w_out_b, x, g.reshape(1, D), b.reshape(1, D))


def _router_kernel(x_ref, w_ref, b_ref, e_ref, g_ref, r_ref, cnt_ref, carry):
    i = pl.program_id(0)
    tm = x_ref.shape[0]

    @pl.when(i == 0)
    def _():
        carry[...] = jnp.zeros(carry.shape, F32)

    logits = jnp.dot(x_ref[...], w_ref[...], preferred_element_type=F32,
                     precision=lax.Precision.HIGHEST) + b_ref[...]
    lane = lax.broadcasted_iota(jnp.int32, (tm, LANES), 1)
    lanef = lane.astype(F32)
    s = logits
    picks = []
    chosen = jnp.zeros((tm, LANES), jnp.bool_)
    for _ in range(TOP_K):
        mx = jnp.max(s, axis=-1, keepdims=True)
        first = jnp.min(jnp.where(s == mx, lanef, float(LANES)), axis=-1, keepdims=True)
        pick = lanef == first
        picks.append((pick, first))
        chosen = chosen | pick
        s = jnp.where(pick, -3e38, s)
    top = jnp.max(logits, axis=-1, keepdims=True)
    ex = jnp.where(chosen, jnp.exp(logits - top), 0.0)
    gate = ex / jnp.sum(ex, axis=-1, keepdims=True)

    onehot = jnp.where(chosen, 1.0, 0.0)
    rr = lax.broadcasted_iota(jnp.int32, (tm, tm), 0)
    rc = lax.broadcasted_iota(jnp.int32, (tm, tm), 1)
    before = (rr > rc).astype(F32).astype(BF16)
    rank = jnp.dot(before, onehot.astype(BF16), preferred_element_type=F32) + carry[0:1, :]
    carry[0:1, :] = carry[0:1, :] + jnp.sum(onehot, axis=0, keepdims=True)
    cnt_ref[...] = carry[...]

    e_out = jnp.zeros((tm, LANES), jnp.int32)
    g_out = jnp.zeros((tm, LANES), F32)
    r_out = jnp.zeros((tm, LANES), F32)
    for k, (pick, first) in enumerate(picks):
        e_out = jnp.where(lane == k, first.astype(jnp.int32), e_out)
        g_out = jnp.where(lane == k, jnp.sum(jnp.where(pick, gate, 0.0), axis=-1, keepdims=True), g_out)
        r_out = jnp.where(lane == k, jnp.sum(jnp.where(pick, rank, 0.0), axis=-1, keepdims=True), r_out)
    e_ref[...] = e_out
    g_ref[...] = g_out
    r_ref[...] = r_out


def _router(x, w_pad, b_pad, tm=256):
    S, D = x.shape
    out = pl.BlockSpec((tm, LANES), lambda i: (i, 0))
    return pl.pallas_call(
        _router_kernel,
        grid=(S // tm,),
        in_specs=[pl.BlockSpec((tm, D), lambda i: (i, 0)),
                  pl.BlockSpec((D, LANES), lambda i: (0, 0)),
                  pl.BlockSpec((1, LANES), lambda i: (0, 0))],
        out_specs=[out, out, out, pl.BlockSpec((8, LANES), lambda i: (0, 0))],
        out_shape=[jax.ShapeDtypeStruct((S, LANES), jnp.int32), jax.ShapeDtypeStruct((S, LANES), F32),
                   jax.ShapeDtypeStruct((S, LANES), F32), jax.ShapeDtypeStruct((8, LANES), F32)],
        scratch_shapes=[pltpu.VMEM((8, LANES), F32)],
        compiler_params=_cparams(("arbitrary",), 32),
        name="moe_router",
    )(x, w_pad, b_pad)


def _expert_kernel(sb_ref, se_ref, sf_ref, sj_ref, so_ref, sv_ref, sn_ref, tok_ref,
                   x_hbm, wgu_ref, bgu_ref, wdn_ref, bdn_ref, o_ref, wgu_b, wdn_b, acc, stage, xrows, sem):
    s = pl.program_id(0)
    n_steps = pl.num_programs(0)
    nrow = sv_ref[s]
    valid = nrow > 0
    f = sf_ref[s]
    j = sj_ref[s]
    SUB = MOE_SUB
    FT2 = wgu_ref.shape[1]
    RT = ROW_TILES
    UNROLL = 8

    def gather_rows(step):
        base = sb_ref[step] * SUB
        slot = sj_ref[step] % 2

        def issue(i, carry):
            for u in range(UNROLL):
                r = i * UNROLL + u
                t = tok_ref[base + r]
                pltpu.make_async_copy(x_hbm.at[t], stage.at[slot, :, r, :], sem.at[slot]).start()
            return carry

        lax.fori_loop(0, sv_ref[step] // UNROLL, issue, 0)

    def gather_if_first_tile(step, cond):
        u = jnp.clip(step, 0, n_steps - 1)

        @pl.when(cond & (step < n_steps) & (sv_ref[u] > 0) & (sf_ref[u] == 0))
        def _():
            gather_rows(u)

    @pl.when(s == 0)
    def _():
        gather_rows(0)

    gather_if_first_tile(1, s == 0)
    next_chunk = sn_ref[s]
    starts_tile1 = valid & (f == 1) & (j == 0)
    gather_if_first_tile(next_chunk, starts_tile1)
    gather_if_first_tile(next_chunk + 1, starts_tile1)

    @pl.when(valid & (j == 0))
    def _():
        wgu_b[...] = wgu_ref[...].astype(BF16)
        wdn_b[...] = wdn_ref[...].astype(BF16)

    @pl.when(jnp.logical_not(valid))
    def _():
        o_ref[...] = jnp.zeros(o_ref.shape, F32)

    def process(m):
        rows = pl.ds(pl.multiple_of(j * SUB, SUB), m)

        @pl.when(f == 0)
        def _():
            slot = j % 2
            filled = stage.at[slot, :, pl.ds(0, m), :]
            pltpu.make_async_copy(filled, filled, sem.at[slot]).wait()
            for c in range(RT):
                xrows[rows, c * LANES:(c + 1) * LANES] = stage[slot, c, 0:m, :].astype(BF16)
            after = jnp.minimum(s + 1, n_steps - 1)
            gather_if_first_tile(s + 2, (sf_ref[after] == 0) & (sv_ref[after] > 0))

        gu = jnp.dot(xrows[rows, :], wgu_b[...], preferred_element_type=F32) + bgu_ref[...]
        gate = jnp.minimum(gu, SWIGLU_LIMIT)
        act = gate * jax.nn.sigmoid(SWIGLU_ALPHA * gate)
        up1 = jnp.clip(gu, -SWIGLU_LIMIT, SWIGLU_LIMIT) + 1.0
        pr = lax.broadcasted_iota(jnp.int32, (2 * LANES, LANES), 0)
        pc = lax.broadcasted_iota(jnp.int32, (2 * LANES, LANES), 1)
        pick_even = (pr == 2 * pc).astype(F32).astype(BF16)
        hs = []
        for c in range(FT2 // (2 * LANES)):
            parts = []
            for k in range(2):
                cs = slice((2 * c + k) * LANES, (2 * c + k + 1) * LANES)
                parts.append((act[:, cs] * pltpu.roll(up1[:, cs], LANES - 1, 1)).astype(BF16))
            pair = jnp.concatenate(parts, axis=1)
            hs.append(jnp.dot(pair, pick_even, preferred_element_type=F32).astype(BF16))
        h = jnp.concatenate(hs, axis=1)
        y = jnp.dot(h, wdn_b[...], preferred_element_type=F32)

        @pl.when(f == 0)
        def _():
            acc[rows, :] = y + bdn_ref[...]

        @pl.when((f > 0) & (f < MOE_NF - 1))
        def _():
            acc[rows, :] = acc[rows, :] + y

        @pl.when(f == MOE_NF - 1)
        def _():
            yfin = acc[rows, :] + y
            for c in range(RT):
                o_ref[pl.ds(c, m, stride=RT), :] = yfin[:, c * LANES:(c + 1) * LANES]
            if m < SUB:
                o_ref[pl.ds(m * RT, (SUB - m) * RT), :] = jnp.zeros(((SUB - m) * RT, LANES), F32)

    for m in range(MOE_ROWQ, SUB + 1, MOE_ROWQ):
        pl.when(nrow == m)(functools.partial(process, m))


def _experts(meta, tok, x_rows, w_gu, b_gu3, w_dn, b_dn3, layer):
    sb, se, sf, sj, so, sv, sn = meta
    NS = sb.shape[0]
    R = tok.shape[0]
    D = D_MODEL
    RT = ROW_TILES

    def wmap(fn):
        return lambda s, sb, se, sf, sj, so, sv, sn, tok: fn(s, se, sf, so)

    grid_spec = pltpu.PrefetchScalarGridSpec(
        num_scalar_prefetch=8,
        grid=(NS,),
        in_specs=[
            pl.BlockSpec(memory_space=pl.ANY),
            pl.BlockSpec((None, None, D, 2 * MOE_FT), wmap(lambda s, se, sf, so: (layer, se[s], 0, sf[s]))),
            pl.BlockSpec((None, None, 1, 2 * MOE_FT), wmap(lambda s, se, sf, so: (layer, se[s], 0, sf[s]))),
            pl.BlockSpec((None, None, MOE_FT, D), wmap(lambda s, se, sf, so: (layer, se[s], sf[s], 0))),
            pl.BlockSpec((None, None, 1, D), wmap(lambda s, se, sf, so: (layer, se[s], 0, 0))),
        ],
        out_specs=pl.BlockSpec((MOE_SUB * RT, LANES), wmap(lambda s, se, sf, so: (so[s], 0))),
        scratch_shapes=[pltpu.VMEM((D, 2 * MOE_FT), BF16), pltpu.VMEM((MOE_FT, D), BF16),
                        pltpu.VMEM((MOE_J * MOE_SUB, D), F32),
                        pltpu.VMEM((2, RT, MOE_SUB, LANES), F32),
                        pltpu.VMEM((MOE_J * MOE_SUB, D), BF16),
                        pltpu.SemaphoreType.DMA((2,))],
    )
    return pl.pallas_call(
        _expert_kernel,
        grid_spec=grid_spec,
        out_shape=jax.ShapeDtypeStruct((R * RT, LANES), F32),
        compiler_params=_cparams(("arbitrary",), 60),
        name="moe_experts",
    )(sb, se, sf, sj, so, sv, sn, tok, x_rows, w_gu, b_gu3, w_dn, b_dn3)


def _plan_kernel(cnt_ref, sb, se, sf, sj, so, sv, sn, sub_start, src_start, n_used):
    n_sub_max = src_start.shape[0]
    n_steps_max = sb.shape[0]

    def expert(e, carry):
        b0, g0, s = carry
        c = cnt_ref[e]
        ns = (c + MOE_SUB - 1) // MOE_SUB
        sub_start[e] = b0

        def sub_tile(lb, _):
            src_start[b0 + lb] = g0 + lb * MOE_SUB
            n_used[b0 + lb] = jnp.minimum(MOE_SUB, c - lb * MOE_SUB)
            return 0

        lax.fori_loop(0, ns, sub_tile, 0)

        def chunk(ci, s):
            cb0 = b0 + ci * MOE_J
            nj = jnp.minimum(MOE_J, ns - ci * MOE_J)

            chunk_end = s + MOE_NF * nj

            def ff_tile(f, s):
                def slot(j, s):
                    rem = c - (ci * MOE_J + j) * MOE_SUB
                    sn[s] = chunk_end
                    sb[s] = cb0 + j
                    se[s] = e
                    sf[s] = f
                    sj[s] = j
                    so[s] = jnp.where(f == MOE_NF - 1, cb0 + j, cb0)
                    sv[s] = jnp.clip((rem + MOE_ROWQ - 1) // MOE_ROWQ * MOE_ROWQ, MOE_ROWQ, MOE_SUB)
                    return s + 1

                return lax.fori_loop(0, nj, slot, s)

            return lax.fori_loop(0, MOE_NF, ff_tile, s)

        s = lax.fori_loop(0, (ns + MOE_J - 1) // MOE_J, chunk, s)
        return b0 + ns, g0 + c, s

    b_end, _, s_end = lax.fori_loop(0, N_EXPERTS, expert, (jnp.int32(0), jnp.int32(0), jnp.int32(0)))

    def unused_sub_tile(b, _):
        src_start[b] = 0
        n_used[b] = 0
        return 0

    lax.fori_loop(b_end, n_sub_max, unused_sub_tile, 0)

    last = jnp.maximum(s_end - 1, 0)

    def unused_step(s, _):
        sb[s] = sb[last]
        se[s] = se[last]
        sf[s] = sf[last]
        sj[s] = sj[last]
        so[s] = jnp.minimum(b_end + (s - s_end), n_sub_max - 1)
        sv[s] = 0
        sn[s] = n_steps_max
        return 0

    lax.fori_loop(s_end, n_steps_max, unused_step, 0)


def _moe_tables(e_flat, r_flat, cnt, n_sub_max):
    NS = MOE_NF * n_sub_max
    smem = pl.BlockSpec(memory_space=pltpu.SMEM)
    i32 = lambda n: jax.ShapeDtypeStruct((n,), jnp.int32)
    *meta, sub_start, src_start, n_used = pl.pallas_call(
        _plan_kernel,
        in_specs=[smem],
        out_specs=[smem] * 10,
        out_shape=[i32(NS)] * 7 +[i32(N_EXPERTS), i32(n_sub_max), i32(n_sub_max)],
        name="moe_plan",
    )(cnt)
    dest = sub_start[e_flat] * MOE_SUB + r_flat

    n_asg = e_flat.shape[0]
    _, tok_sorted = lax.sort_key_val(dest, jnp.arange(n_asg, dtype=jnp.int32) // TOP_K)
    lane = jnp.arange(MOE_SUB, dtype=jnp.int32)[None, :]
    src = jnp.minimum(src_start[:, None] + lane, n_asg - 1)
    tok = jnp.where(lane < n_used[:, None], tok_sorted[src], 0).reshape(-1)
    return dest, tok, tuple(meta)


def _combine_kernel(dest_ref, y_hbm, g4_ref, x_ref, g_ref, b_ref, o_ref, ob_ref, stage, ytile, ysum, sem):
    i = pl.program_id(0)
    n = pl.num_programs(0)
    tm = x_ref.shape[0]
    RT = ROW_TILES
    n_rows = tm * TOP_K

    def start_row(step, to_slot, r):
        d = dest_ref[step * n_rows + r]
        pltpu.make_async_copy(y_hbm.at[pl.ds(pl.multiple_of(d * RT, RT), RT), :],
                              stage.at[to_slot, pl.ds(pl.multiple_of(r * RT, RT), RT), :],
                              sem.at[to_slot]).start()

    def gather_rows(step, to_slot):
        def issue(r, carry):
            start_row(step, to_slot, r)
            return carry

        lax.fori_loop(0, n_rows, issue, 0, unroll=8)

    @pl.when(i == 0)
    def _():
        gather_rows(0, 0)

    @pl.when(i + 1 < n)
    def _():
        gather_rows(i + 1, (i + 1) % 2)

    slot = i % 2
    pltpu.make_async_copy(y_hbm.at[pl.ds(0, n_rows * RT), :], stage.at[slot], sem.at[slot]).wait()

    def token_sum(t, carry):
        r0 = pl.multiple_of(t * (TOP_K * RT), TOP_K * RT)
        y = g4_ref[TOP_K * t] * stage[slot, pl.ds(r0, RT), :]
        for k in range(1, TOP_K):
            y = y + g4_ref[TOP_K * t + k] * stage[slot, pl.ds(r0 + k * RT, RT), :]
        ytile[pl.ds(pl.multiple_of(t * RT, RT), RT), :] = y
        return carry

    lax.fori_loop(0, tm, token_sum, 0, unroll=4)
    for c in range(RT):
        ysum[:, c * LANES:(c + 1) * LANES] = ytile[pl.ds(c, tm, stride=RT), :]
    out = _layer_norm_rows(DN_ALPHA * x_ref[...] + ysum[...], g_ref[...], b_ref[...])
    o_ref[...] = out
    ob_ref[...] = out.astype(BF16)


def _combine_ln(dest_flat, y_rows, g4, x, g, b, tm=128):
    S, D = x.shape
    vec = pl.BlockSpec((1, D), lambda i, d: (0, 0))
    grid_spec = pltpu.PrefetchScalarGridSpec(
        num_scalar_prefetch=1,
        grid=(S // tm,),
        in_specs=[pl.BlockSpec(memory_space=pl.ANY),
                  pl.BlockSpec((tm * TOP_K,), lambda i, d: (i,), memory_space=pltpu.SMEM),
                  pl.BlockSpec((tm, D), lambda i, d: (i, 0)), vec, vec],
        out_specs=[pl.BlockSpec((tm, D), lambda i, d: (i, 0)), pl.BlockSpec((tm, D), lambda i, d: (i, 0))],
        scratch_shapes=[pltpu.VMEM((2, tm * TOP_K * ROW_TILES, LANES), F32),
                        pltpu.VMEM((tm * ROW_TILES, LANES), F32),
                        pltpu.VMEM((tm, D), F32),
                        pltpu.SemaphoreType.DMA((2,))],
    )
    return pl.pallas_call(
        _combine_kernel,
        grid_spec=grid_spec,
        out_shape=[jax.ShapeDtypeStruct((S, D), F32), jax.ShapeDtypeStruct((S, D), BF16)],
        compiler_params=_cparams(("arbitrary",), 48),
        name="moe_combine_ln",
    )(dest_flat, y_rows, g4, x, g.reshape(1, D), b.reshape(1, D))


def _nsa_group(u_bf, gate_logits, cmp_pos, cmp_w1, cmp_w2, rel_bias, g_nsa):
    S = u_bf.shape[0]
    G = NSA_KV_HEADS
    kvc = u_bf[:, NSA_WIDTH:NSA_WIDTH + 2 * KV_WIDTH]
    kv_rows = kvc.reshape(S, 2 * G, HEAD_DIM).transpose(1, 0, 2).reshape(2 * G, S // CMP_STRIDE,
                                                                          CMP_STRIDE * HEAD_DIM)
    kcv = _compress(kv_rows, cmp_pos, cmp_w1, cmp_w2)
    o_c, sel = _cmp_attention(rel_bias, u_bf, kcv)
    kv_pad = jnp.pad(u_bf[:, NSA_WIDTH + 2 * KV_WIDTH:], ((KV_FRONT, 0), (0, 0)))
    return _sel_attention(rel_bias, u_bf, kv_pad, sel, o_c, gate_logits, g_nsa)


def _moe(x, x_rows, w_router, b_router, w_gu, b_gu, w_dn, b_dn, ln_g, ln_b, layer):
    S, D = x.shape
    E = N_EXPERTS
    w_pad = jnp.pad(w_router, ((0, 0), (0, LANES - E)))
    b_pad = jnp.pad(b_router, (0, LANES - E), constant_values=NEG_INF).reshape(1, LANES)
    e_out, g_out, r_out, cnt_out = _router(x, w_pad, b_pad)
    e_flat = e_out[:, :TOP_K].reshape(-1)
    r_flat = r_out[:, :TOP_K].reshape(-1).astype(jnp.int32)
    cnt = cnt_out[0, :E].astype(jnp.int32)
    n_sub_max = (S * TOP_K) // MOE_SUB + E
    flat, tok, meta = _moe_tables(e_flat, r_flat, cnt, n_sub_max)
    y_rows = _experts(meta, tok, x_rows.reshape(S, ROW_TILES, LANES), w_gu, b_gu.reshape(DEPTH, E, 1, 2 * D_FF), w_dn,
                      b_dn.reshape(DEPTH, E, 1, D), layer)
    return _combine_ln(flat, y_rows, g_out[:, :TOP_K].reshape(-1), x, ln_g, ln_b)


def kernel(x, w_in, conv_w, conv_b, lru_wa, lru_ba, lru_wi, lru_bi, lru_lambda, cmp_pos, cmp_w1, cmp_w2,
           rel_bias, g_lru, g_nsa, w_out, ln1_g, ln1_b, w_router, b_router, w_gate_up, b_gate_up, w_down,
           b_down, ln2_g, ln2_b):
    B, S, D = x.shape
    assert B == 1
    x = x.reshape(S, D)
    xb = x.astype(BF16)
    n_gate = 3 * NSA_HEADS
    for l in range(DEPTH):
        u_f32 = _project(xb, w_in, l, 0, 2 * LRU_WIDTH // 512, 512, F32)
        u_bf = _project(xb, w_in, l, 2 * LRU_WIDTH // 512, (MAIN_COLS - 2 * LRU_WIDTH) // 512, 512, BF16)
        w_gate = jnp.pad(w_in[l, :, MAIN_COLS:], ((0, 0), (0, LANES - n_gate)))[None]
        gate_logits = _project(xb, w_gate, 0, 0, 1, LANES, F32)
        y_lru = _lru_group(u_f32, conv_w[l], conv_b[l], lru_wa[l], lru_ba[l], lru_wi[l], lru_bi[l],
                           lru_lambda[l], g_lru[l])
        y_nsa = _nsa_group(u_bf, gate_logits, cmp_pos[l], cmp_w1[l], cmp_w2[l], rel_bias, g_nsa[l])
        x, x_rows = _out_proj_ln(y_lru, y_nsa, w_out[l].astype(BF16), x, ln1_g[l], ln1_b[l])
        x, xb = _moe(x, x_rows, w_router[l], b_router[l], w_gate_up, b_gate_up, w_down, b_down,
                     ln2_g[l], ln2_b[l], l)
    return x.reshape(B, S, D)
```

```python
import functools
import math

import numpy as np
import jax
import jax.numpy as jnp
from jax import lax
from jax.experimental import pallas as pl
from jax.experimental.pallas import tpu as pltpu

D_MODEL = 2048
DEPTH = 2
LRU_WIDTH = 1024
LRU_BLOCKS = 8
LRU_BLOCK_W = LRU_WIDTH // LRU_BLOCKS
CONV_WIDTH = 4
LRU_C = 8.0
NSA_HEADS = 8
NSA_KV_HEADS = 2
HEADS_PER_GROUP = NSA_HEADS // NSA_KV_HEADS
HEAD_DIM = 128
NSA_WIDTH = NSA_HEADS * HEAD_DIM
KV_WIDTH = NSA_KV_HEADS * HEAD_DIM
CMP_BLOCK = 32
CMP_STRIDE = 16
SEL_BLOCK = 64
N_SEL = 16
WINDOW = 512
Q_BLOCK = 128
N_BUCKETS = 32
MAX_DISTANCE = 128
N_EXPERTS = 32
TOP_K = 4
D_FF = 2048
SWIGLU_LIMIT = 7.0
SWIGLU_ALPHA = 1.702
DN_ALPHA = (2 * DEPTH) ** 0.25
NORM_EPS = 1e-5
NEG_INF = -1e30
FORCED_SCORE = 1e9
ATTN_SCALE = HEAD_DIM ** -0.5

LOG2E = math.log2(math.e)
N_FORCED = 3
FAR_CHUNK = 1024
KV_FRONT = FAR_CHUNK
PAD_BLK = KV_FRONT // SEL_BLOCK
M_INIT = -1e9

LANES = 128
MIB = 1024 * 1024
MAIN_COLS = 2 * LRU_WIDTH + NSA_WIDTH + 6 * KV_WIDTH
BF16 = jnp.bfloat16
F32 = jnp.float32

MOE_SUB = 256
MOE_ROWQ = 64
MOE_J = 5
MOE_FT = 512
MOE_NF = D_FF // MOE_FT
ROW_TILES = D_MODEL // LANES


def _bucket_thresholds():
    max_exact = N_BUCKETS // 2
    out = []
    for b in range(1, N_BUCKETS):
        if b <= max_exact:
            out.append(b)
            continue
        d = max_exact
        while True:
            v = math.log(d / max_exact) / math.log(MAX_DISTANCE / max_exact) * (N_BUCKETS - max_exact)
            assert abs(v - round(v)) > 1e-3 or d == max_exact
            if min(max_exact + int(v), N_BUCKETS - 1) >= b:
                break
            d += 1
        out.append(d)
    return tuple(out)


BUCKET_TH = _bucket_thresholds()


def _cparams(semantics, vmem_mib):
    return pltpu.CompilerParams(dimension_semantics=semantics, vmem_limit_bytes=vmem_mib * MIB)


def _nt_dot(a, b):
    return lax.dot_general(a, b, (((1,), (1,)), ((), ())), preferred_element_type=F32)


def _bias_of_dist(rb_ref, head, d):
    b = jnp.full(d.shape, rb_ref[0, head], F32)
    for k in range(1, N_BUCKETS):
        b = jnp.where(d >= BUCKET_TH[k - 1], rb_ref[k, head], b)
    return b


def _mm_kernel(x_ref, w_ref, o_ref, wb_ref):
    @pl.when(pl.program_id(1) == 0)
    def _():
        wb_ref[...] = w_ref[...].astype(BF16)

    o_ref[...] = jnp.dot(x_ref[...], wb_ref[...], preferred_element_type=F32).astype(o_ref.dtype)


def _project(xb, w3, layer, col_blk0, n_blks, tn, out_dtype, tm=1024):
    S, D = xb.shape
    return pl.pallas_call(
        _mm_kernel,
        grid=(n_blks, S // tm),
        in_specs=[pl.BlockSpec((tm, D), lambda n, m: (m, 0)),
                  pl.BlockSpec((None, D, tn), lambda n, m: (layer, 0, col_blk0 + n))],
        out_specs=pl.BlockSpec((tm, tn), lambda n, m: (m, n)),
        out_shape=jax.ShapeDtypeStruct((S, n_blks * tn), out_dtype),
        scratch_shapes=[pltpu.VMEM((D, tn), BF16)],
        compiler_params=_cparams(("arbitrary", "arbitrary"), 40),
        name="in_proj",
    )(xb, w3)


def _gelu_tanh(x):
    return 0.5 * x * (1.0 + jnp.tanh(math.sqrt(2.0 / math.pi) * (x + 0.044715 * (x * x * x))))


def _lru_kernel(xr_ref, xg_ref, cw_ref, cb_ref, wa_ref, ba_ref, wi_ref, bi_ref, lam_ref, g_ref,
                o_ref, xbuf, hc):
    i = pl.program_id(0)
    T = xr_ref.shape[0]
    W = xr_ref.shape[1]
    HALO = 8

    @pl.when(i == 0)
    def _():
        xbuf[0:HALO, :] = jnp.zeros((HALO, W), F32)
        hc[...] = jnp.zeros(hc.shape, F32)

    @pl.when(i > 0)
    def _():
        xbuf[0:HALO, :] = xbuf[T:T + HALO, :]

    xbuf[HALO:HALO + T, :] = xr_ref[...]
    cw = cw_ref[...]
    xc = cb_ref[...] + xbuf[HALO - 3:HALO - 3 + T, :] * cw[0:1]
    for k in range(1, CONV_WIDTH):
        xc = xc + xbuf[HALO - 3 + k:HALO - 3 + k + T, :] * cw[k:k + 1]

    xcb = xc.astype(BF16)
    ra, ia = [], []
    for n in range(LRU_BLOCKS):
        blk = xcb[:, n * LRU_BLOCK_W:(n + 1) * LRU_BLOCK_W]
        ra.append(jnp.dot(blk, wa_ref[n].astype(BF16), preferred_element_type=F32))
        ia.append(jnp.dot(blk, wi_ref[n].astype(BF16), preferred_element_type=F32))
    r = jax.nn.sigmoid(jnp.concatenate(ra, axis=1) + ba_ref[...])
    ig = jax.nn.sigmoid(jnp.concatenate(ia, axis=1) + bi_ref[...])

    z = -lam_ref[...]
    softplus = jnp.maximum(z, 0.0) + jnp.log(1.0 + jnp.exp(-jnp.abs(z)))
    log_a = -LRU_C * r * softplus
    a = jnp.exp(log_a)
    u = jnp.sqrt(1.0 - a * a) * (ig * xc)

    SUBL = 8
    sub = lax.broadcasted_iota(jnp.int32, (T, W), 0) % SUBL
    A, U = a, u
    d = 1
    while d < SUBL:
        keep = sub >= d
        a_sh = jnp.where(keep, pltpu.roll(A, d, 0), 1.0)
        u_sh = jnp.where(keep, pltpu.roll(U, d, 0), 0.0)
        U = A * u_sh + U
        A = A * a_sh
        d *= 2
    state = hc[...]
    groups = []
    for k in range(T // SUBL):
        hg = U[k * SUBL:(k + 1) * SUBL] + A[k * SUBL:(k + 1) * SUBL] * state
        state = hg[SUBL - 1:SUBL]
        groups.append(hg)
    h = jnp.concatenate(groups, axis=0)
    hc[...] = state

    y = h * _gelu_tanh(xg_ref[...])
    ms = jnp.mean(y * y, axis=-1, keepdims=True)
    o_ref[...] = (y * lax.rsqrt(ms + NORM_EPS) * g_ref[...]).astype(o_ref.dtype)


def _lru_group(u_f32, conv_w, conv_b, wa, ba, wi, bi, lam, g_lru, T=256):
    S = u_f32.shape[0]
    W = LRU_WIDTH
    row = lambda v: v.reshape(1, W)
    vec = pl.BlockSpec((1, W), lambda i: (0, 0))
    mat = pl.BlockSpec((LRU_BLOCKS, LRU_BLOCK_W, LRU_BLOCK_W), lambda i: (0, 0, 0))
    return pl.pallas_call(
        _lru_kernel,
        grid=(S // T,),
        in_specs=[pl.BlockSpec((T, W), lambda i: (i, 0)),
                  pl.BlockSpec((T, W), lambda i: (i, 1)),
                  pl.BlockSpec((CONV_WIDTH, W), lambda i: (0, 0)),
                  vec, mat, vec, mat, vec, vec, vec],
        out_specs=pl.BlockSpec((T, W), lambda i: (i, 0)),
        out_shape=jax.ShapeDtypeStruct((S, W), BF16),
        scratch_shapes=[pltpu.VMEM((T + 8, W), F32), pltpu.VMEM((1, W), F32)],
        compiler_params=_cparams(("arbitrary",), 48),
        name="rg_lru",
    )(u_f32, u_f32, conv_w, row(conv_b), wa, row(ba), wi, row(bi), row(lam), row(g_lru))


def _compress_kernel(x_ref, pos_ref, w1_ref, w2_ref, o_ref):
    NJ = x_ref.shape[0]
    half = (CMP_BLOCK // 2) * HEAD_DIM
    x = x_ref[...].astype(F32)
    a = (x + pos_ref[0:1, :]).astype(BF16)
    b = (x + pos_ref[1:2, :]).astype(BF16)
    y1 = jnp.dot(a, w1_ref[0:half, :].astype(BF16), preferred_element_type=F32)
    y2 = jnp.dot(b, w1_ref[half:2 * half, :].astype(BF16), preferred_element_type=F32)
    hmid = y1 + pltpu.roll(y2, NJ - 1, 0)
    o_ref[...] = jnp.dot(_gelu_tanh(hmid).astype(BF16), w2_ref[...].astype(BF16),
                         preferred_element_type=F32).astype(o_ref.dtype)


def _compress(kv_rows, pos, w1, w2):
    _, NJ, RW = kv_rows.shape
    return pl.pallas_call(
        _compress_kernel,
        grid=(4,),
        in_specs=[pl.BlockSpec((None, NJ, RW), lambda a: (a, 0, 0)),
                  pl.BlockSpec((None, 2, RW), lambda a: (a // 2, 0, 0)),
                  pl.BlockSpec((None, CMP_BLOCK * HEAD_DIM, HEAD_DIM), lambda a: (a // 2, 0, 0)),
                  pl.BlockSpec((None, HEAD_DIM, HEAD_DIM), lambda a: (a // 2, 0, 0))],
        out_specs=pl.BlockSpec((None, NJ, HEAD_DIM), lambda a: (a, 0, 0)),
        out_shape=jax.ShapeDtypeStruct((4, NJ, HEAD_DIM), BF16),
        compiler_params=_cparams(("arbitrary",), 32),
        name="kv_compress",
    )(kv_rows, pos.reshape(2, 2, RW), w1, w2)


def _cmp_attn_kernel(rb_ref, q_ref, kcv_ref, oc_ref, sel_ref, tcat):
    qb = pl.program_id(0)
    t0 = qb * Q_BLOCK
    NJ = kcv_ref.shape[1]
    NB = NJ * CMP_STRIDE // SEL_BLOCK
    NBP = sel_ref.shape[2]
    n_sel = min(N_SEL, NB)
    LOCAL = 2 * Q_BLOCK // CMP_STRIDE

    @pl.when(qb == 0)
    def _():
        i = lax.broadcasted_iota(jnp.int32, (Q_BLOCK, LANES), 0)
        j = lax.broadcasted_iota(jnp.int32, (Q_BLOCK, LANES), 1)
        d = jnp.maximum(i - CMP_STRIDE * j + (Q_BLOCK - CMP_STRIDE), 0)
        for h in range(NSA_HEADS):
            corr = jnp.where(j < LOCAL, (_bias_of_dist(rb_ref, h, d) - rb_ref[N_BUCKETS - 1, h]) * LOG2E, 0.0)
            hi = corr.astype(BF16)
            tcat[h, :, 0:LANES] = hi
            tcat[h, :, LANES:2 * LANES] = (corr - hi.astype(F32)).astype(BF16)

    def tile(W):
        row = lax.broadcasted_iota(jnp.int32, (Q_BLOCK, W), 0)
        col = lax.broadcasted_iota(jnp.int32, (Q_BLOCK, W), 1)
        mask_add = jnp.where(CMP_STRIDE * col + (CMP_BLOCK - 1) <= t0 + row, 0.0, NEG_INF)
        jj = lax.broadcasted_iota(jnp.int32, (2 * LANES, W), 0) % LANES
        cc = lax.broadcasted_iota(jnp.int32, (2 * LANES, W), 1)
        place = (cc == (Q_BLOCK // CMP_STRIDE) * (qb - 1) + jj).astype(F32).astype(BF16)
        bj = lax.broadcasted_iota(jnp.int32, (NB, W), 0)
        bc = lax.broadcasted_iota(jnp.int32, (NB, W), 1)
        ratio = SEL_BLOCK // CMP_STRIDE
        overlap_t = ((bc <= ratio * bj + ratio - 1) & (bc >= ratio * bj - 1)).astype(F32).astype(BF16)

        sj = lax.broadcasted_iota(jnp.int32, (NB, Q_BLOCK), 0)
        st = t0 + lax.broadcasted_iota(jnp.int32, (NB, Q_BLOCK), 1)
        cur = st // SEL_BLOCK
        blk_ok = sj <= cur
        forced = (sj == 0) | (sj == cur) | (sj == cur - 1)

        scores = []
        for g in range(NSA_KV_HEADS):
            kc = kcv_ref[g, 0:W, :]
            vc = kcv_ref[NSA_KV_HEADS + g, 0:W, :]
            psum = jnp.zeros((Q_BLOCK, W), F32)
            for h in range(HEADS_PER_GROUP):
                hh = g * HEADS_PER_GROUP + h
                qh = (q_ref[:, hh * HEAD_DIM:(hh + 1) * HEAD_DIM].astype(F32) * (ATTN_SCALE * LOG2E)).astype(BF16)
                z = _nt_dot(qh, kc) + jnp.dot(tcat[hh], place, preferred_element_type=F32) + mask_add
                m = jnp.maximum(jnp.max(z, axis=-1, keepdims=True), M_INIT)
                e = jnp.exp2(z - m)
                l = jnp.sum(e, axis=-1, keepdims=True)
                p = e * (1.0 / jnp.where(l > 0.0, l, 1.0))
                oc_ref[:, hh * HEAD_DIM:(hh + 1) * HEAD_DIM] = jnp.dot(
                    p.astype(BF16), vc, preferred_element_type=F32)
                psum = psum + p
            p_hi = psum.astype(BF16)
            p_lo = (psum - p_hi.astype(F32)).astype(BF16)
            imp_t = _nt_dot(overlap_t, p_hi) + _nt_dot(overlap_t, p_lo)
            scores.append(jnp.where(blk_ok, jnp.where(forced, FORCED_SCORE, imp_t), NEG_INF))

        score = jnp.concatenate(scores, axis=1)
        sjf = jnp.concatenate([sj.astype(F32)] * NSA_KV_HEADS, axis=1)
        chosen = jnp.concatenate([jnp.where(forced, 1.0, 0.0)] * NSA_KV_HEADS, axis=1) > 0.5
        score = jnp.where(chosen, -3e38, score)
        for _ in range(max(n_sel - N_FORCED, 0)):
            mx = jnp.max(score, axis=0, keepdims=True)
            first = jnp.min(jnp.where(score == mx, sjf, float(NB)), axis=0, keepdims=True)
            pick = sjf == first
            chosen = chosen | pick
            score = jnp.where(pick, -3e38, score)
        picked = jnp.where(chosen, 1.0, 0.0)
        for g in range(NSA_KV_HEADS):
            pen_t = jnp.where((picked[:, g * Q_BLOCK:(g + 1) * Q_BLOCK] > 0.5) & blk_ok, 0.0, NEG_INF)
            padded = jnp.concatenate([jnp.full((PAD_BLK, Q_BLOCK), NEG_INF, F32), pen_t,
                                      jnp.zeros((NBP - PAD_BLK - NB, Q_BLOCK), F32)], axis=0)
            pen = jnp.concatenate([padded[r:r + LANES].T for r in range(0, NBP, LANES)], axis=1)
            sel_ref[g] = pen.astype(sel_ref.dtype)

    step = min(LANES, NJ)
    need = (Q_BLOCK // CMP_STRIDE) * qb + (Q_BLOCK - CMP_BLOCK) // CMP_STRIDE + 1
    width = jnp.clip((need + step - 1) // step * step, min(2 * step, NJ), NJ)
    for W in range(min(2 * step, NJ), NJ + 1, step):
        pl.when(width == W)(functools.partial(tile, W))


def _cmp_attention(rel_bias, u_bf, kcv):
    S = u_bf.shape[0]
    NJ = kcv.shape[1]
    NB = -(-(S // SEL_BLOCK + PAD_BLK) // LANES) * LANES
    return pl.pallas_call(
        _cmp_attn_kernel,
        grid=(S // Q_BLOCK,),
        in_specs=[pl.BlockSpec(memory_space=pltpu.SMEM),
                  pl.BlockSpec((Q_BLOCK, NSA_WIDTH), lambda i: (i, 0)),
                  pl.BlockSpec((4, NJ, HEAD_DIM), lambda i: (0, 0, 0))],
        out_specs=[pl.BlockSpec((Q_BLOCK, NSA_WIDTH), lambda i: (i, 0)),
                   pl.BlockSpec((NSA_KV_HEADS, Q_BLOCK, NB), lambda i: (0, i, 0))],
        out_shape=[jax.ShapeDtypeStruct((S, NSA_WIDTH), F32),
                   jax.ShapeDtypeStruct((NSA_KV_HEADS, S, NB), BF16)],
        scratch_shapes=[pltpu.VMEM((NSA_HEADS, Q_BLOCK, 2 * LANES), BF16)],
        compiler_params=_cparams(("arbitrary",), 32),
        name="nsa_compressed",
    )(rel_bias, u_bf, kcv)


def _sel_attn_kernel(rb_ref, q_ref, ks_ref, vs_ref, kw_ref, vw_ref, sel_ref, oc_ref, gl_ref, g_ref,
                     o_ref, tbl_s, tbl_w, yacc):
    qb = pl.program_id(0)
    Q = Q_BLOCK
    HPG = HEADS_PER_GROUP
    TW = WINDOW + Q
    FAR = FAR_CHUNK
    NBP = sel_ref.shape[2]
    t0 = qb * Q

    @pl.when(qb == 0)
    def _():
        i = lax.broadcasted_iota(jnp.int32, (Q, TW), 0)
        c = lax.broadcasted_iota(jnp.int32, (Q, TW), 1)
        d = i - c + WINDOW
        dd = jnp.maximum(d, 0)
        for h in range(NSA_HEADS):
            corr = (_bias_of_dist(rb_ref, h, dd) - rb_ref[N_BUCKETS - 1, h]) * LOG2E
            tbl_s[h] = jnp.where(d < 0, NEG_INF, corr)
            tbl_w[h] = jnp.where((d < 0) | (d >= WINDOW), NEG_INF, corr)

    gates = jax.nn.sigmoid(gl_ref[...])
    col_w = lax.broadcasted_iota(jnp.int32, (1, TW), 1)
    before_start = jnp.where(col_w < WINDOW - t0, NEG_INF, 0.0)
    p_tail = pl.multiple_of(t0 + (KV_FRONT - WINDOW), Q)

    G = NSA_KV_HEADS
    q4 = [jnp.concatenate(
        [(q_ref[:, (g * HPG + h) * HEAD_DIM:(g * HPG + h + 1) * HEAD_DIM].astype(F32)
          * (ATTN_SCALE * LOG2E)).astype(BF16) for h in range(HPG)], axis=0) for g in range(G)]

    def attend(g, k_ref, v_ref, p0, width, add, carry):
        m, l, acc = carry
        cols = slice(g * HEAD_DIM, (g + 1) * HEAD_DIM)
        kk = k_ref[pl.ds(p0, width), cols]
        vv = v_ref[pl.ds(p0, width), cols]
        s = _nt_dot(q4[g], kk)
        z = (s.reshape(HPG, Q, width) + add).reshape(HPG * Q, width)
        m_new = jnp.maximum(m, jnp.max(z, axis=-1, keepdims=True))
        alpha = jnp.exp2(m - m_new)
        e = jnp.exp2(z - m_new)
        l = alpha * l + jnp.sum(e, axis=-1, keepdims=True)
        acc = alpha * acc + jnp.dot(e.astype(BF16), vv, preferred_element_type=F32)
        return m_new, l, acc

    def block_mask(g, p0, width):
        bp = lax.broadcasted_iota(jnp.int32, (NBP, width), 0)
        kb = lax.broadcasted_iota(jnp.int32, (NBP, width), 1) // SEL_BLOCK
        expand = (bp - p0 // SEL_BLOCK == kb).astype(F32).astype(BF16)
        return jnp.dot(sel_ref[g], expand, preferred_element_type=F32)[None]

    init = (jnp.full((HPG * Q, 1), M_INIT, F32), jnp.zeros((HPG * Q, 1), F32),
            jnp.zeros((HPG * Q, HEAD_DIM), F32))

    def far_body(i, carry):
        p0 = pl.multiple_of(p_tail - FAR * (i + 1), Q)
        return tuple(attend(g, ks_ref, vs_ref, p0, FAR, block_mask(g, p0, FAR), carry[g]) for g in range(G))

    n_far = (jnp.maximum(t0 - WINDOW, 0) + FAR - 1) // FAR
    far = lax.fori_loop(0, n_far, far_body, (init,) * G)

    for g in range(G):
        heads = slice(g * HPG, (g + 1) * HPG)
        m_s, l_s, acc_s = attend(g, ks_ref, vs_ref, p_tail, TW, tbl_s[heads] + block_mask(g, p_tail, TW), far[g])
        o_s = acc_s / jnp.where(l_s > 0.0, l_s, 1.0)

        m_w, l_w, acc_w = attend(g, kw_ref, vw_ref, p_tail, TW, tbl_w[heads] + before_start[None], init)
        o_w = acc_w / jnp.where(l_w > 0.0, l_w, 1.0)

        for h in range(HPG):
            hh = g * HPG + h
            hc = slice(hh * HEAD_DIM, (hh + 1) * HEAD_DIM)
            rows = slice(h * Q, (h + 1) * Q)
            yacc[:, hc] = (gates[:, 3 * hh:3 * hh + 1] * oc_ref[:, hc]
                           + gates[:, 3 * hh + 1:3 * hh + 2] * o_s[rows]
                           + gates[:, 3 * hh + 2:3 * hh + 3] * o_w[rows])

    y = yacc[...]
    ms = jnp.mean(y * y, axis=-1, keepdims=True)
    o_ref[...] = (y * lax.rsqrt(ms + NORM_EPS) * g_ref[...]).astype(o_ref.dtype)


def _sel_attention(rel_bias, u_bf, kv_pad, sel, o_c, gate_logits, g_nsa):
    S = u_bf.shape[0]
    NB = sel.shape[2]

    def kv_spec(j):
        return pl.BlockSpec((KV_FRONT + S, KV_WIDTH), lambda i: (0, j))

    return pl.pallas_call(
        _sel_attn_kernel,
        grid=(S // Q_BLOCK,),
        in_specs=[pl.BlockSpec(memory_space=pltpu.SMEM),
                  pl.BlockSpec((Q_BLOCK, NSA_WIDTH), lambda i: (i, 0)),
                  kv_spec(0), kv_spec(1), kv_spec(2), kv_spec(3),
                  pl.BlockSpec((NSA_KV_HEADS, Q_BLOCK, NB), lambda i: (0, i, 0)),
                  pl.BlockSpec((Q_BLOCK, NSA_WIDTH), lambda i: (i, 0)),
                  pl.BlockSpec((Q_BLOCK, LANES), lambda i: (i, 0)),
                  pl.BlockSpec((1, NSA_WIDTH), lambda i: (0, 0))],
        out_specs=pl.BlockSpec((Q_BLOCK, NSA_WIDTH), lambda i: (i, 0)),
        out_shape=jax.ShapeDtypeStruct((S, NSA_WIDTH), BF16),
        scratch_shapes=[pltpu.VMEM((NSA_HEADS, Q_BLOCK, WINDOW + Q_BLOCK), F32),
                        pltpu.VMEM((NSA_HEADS, Q_BLOCK, WINDOW + Q_BLOCK), F32),
                        pltpu.VMEM((Q_BLOCK, NSA_WIDTH), F32)],
        compiler_params=_cparams(("arbitrary",), 60),
        name="nsa_selected_window",
    )(rel_bias, u_bf, kv_pad, kv_pad, kv_pad, kv_pad, sel, o_c, gate_logits, g_nsa.reshape(1, NSA_WIDTH))


def _layer_norm_rows(z, g, b):
    mu = jnp.mean(z, axis=-1, keepdims=True)
    zc = z - mu
    var = jnp.mean(zc * zc, axis=-1, keepdims=True)
    return zc * lax.rsqrt(var + NORM_EPS) * g + b


def _outproj_kernel(yl_ref, yn_ref, w_ref, x_ref, g_ref, b_ref, o_ref, or_ref):
    tm = x_ref.shape[0]
    acc = jnp.dot(yl_ref[...], w_ref[0:LRU_WIDTH, :], preferred_element_type=F32)
    acc = acc + jnp.dot(yn_ref[...], w_ref[LRU_WIDTH:LRU_WIDTH + NSA_WIDTH, :], preferred_element_type=F32)
    y = _layer_norm_rows(DN_ALPHA * x_ref[...] + acc, g_ref[...], b_ref[...])
    o_ref[...] = y
    for c in range(ROW_TILES):
        or_ref[pl.ds(c, tm, stride=ROW_TILES), :] = y[:, c * LANES:(c + 1) * LANES]


def _out_proj_ln(y_lru, y_nsa, w_out_b, x, g, b, tm=256):
    S, D = x.shape
    vec = pl.BlockSpec((1, D), lambda i: (0, 0))
    return pl.pallas_call(
        _outproj_kernel,
        grid=(S // tm,),
        in_specs=[pl.BlockSpec((tm, LRU_WIDTH), lambda i: (i, 0)),
                  pl.BlockSpec((tm, NSA_WIDTH), lambda i: (i, 0)),
                  pl.BlockSpec((LRU_WIDTH + NSA_WIDTH, D), lambda i: (0, 0)),
                  pl.BlockSpec((tm, D), lambda i: (i, 0)), vec, vec],
        out_specs=[pl.BlockSpec((tm, D), lambda i: (i, 0)),
                   pl.BlockSpec((tm * ROW_TILES, LANES), lambda i: (i, 0))],
        out_shape=[jax.ShapeDtypeStruct((S, D), F32), jax.ShapeDtypeStruct((S * ROW_TILES, LANES), F32)],
        compiler_params=_cparams(("arbitrary",), 48),
        name="out_proj_ln",
    )(y_lru, y_nsa, w_out_b, x, g.reshape(1, D), b.reshape(1, D))


def _router_kernel(x_ref, w_ref, b_ref, e_ref, g_ref, r_ref, cnt_ref, carry):
    i = pl.program_id(0)
    tm = x_ref.shape[0]

    @pl.when(i == 0)
    def _():
        carry[...] = jnp.zeros(carry.shape, F32)

    logits = jnp.dot(x_ref[...], w_ref[...], preferred_element_type=F32,
                     precision=lax.Precision.HIGHEST) + b_ref[...]
    lane = lax.broadcasted_iota(jnp.int32, (tm, LANES), 1)
    lanef = lane.astype(F32)
    s = logits
    picks = []
    chosen = jnp.zeros((tm, LANES), jnp.bool_)
    for _ in range(TOP_K):
        mx = jnp.max(s, axis=-1, keepdims=True)
        first = jnp.min(jnp.where(s == mx, lanef, float(LANES)), axis=-1, keepdims=True)
        pick = lanef == first
        picks.append((pick, first))
        chosen = chosen | pick
        s = jnp.where(pick, -3e38, s)
    top = jnp.max(logits, axis=-1, keepdims=True)
    ex = jnp.where(chosen, jnp.exp(logits - top), 0.0)
    gate = ex / jnp.sum(ex, axis=-1, keepdims=True)

    onehot = jnp.where(chosen, 1.0, 0.0)
    rr = lax.broadcasted_iota(jnp.int32, (tm, tm), 0)
    rc = lax.broadcasted_iota(jnp.int32, (tm, tm), 1)
    before = (rr > rc).astype(F32).astype(BF16)
    rank = jnp.dot(before, onehot.astype(BF16), preferred_element_type=F32) + carry[0:1, :]
    carry[0:1, :] = carry[0:1, :] + jnp.sum(onehot, axis=0, keepdims=True)
    cnt_ref[...] = carry[...]

    e_out = jnp.zeros((tm, LANES), jnp.int32)
    g_out = jnp.zeros((tm, LANES), F32)
    r_out = jnp.zeros((tm, LANES), F32)
    for k, (pick, first) in enumerate(picks):
        e_out = jnp.where(lane == k, first.astype(jnp.int32), e_out)
        g_out = jnp.where(lane == k, jnp.sum(jnp.where(pick, gate, 0.0), axis=-1, keepdims=True), g_out)
        r_out = jnp.where(lane == k, jnp.sum(jnp.where(pick, rank, 0.0), axis=-1, keepdims=True), r_out)
    e_ref[...] = e_out
    g_ref[...] = g_out
    r_ref[...] = r_out


def _router(x, w_pad, b_pad, tm=256):
    S, D = x.shape
    out = pl.BlockSpec((tm, LANES), lambda i: (i, 0))
    return pl.pallas_call(
        _router_kernel,
        grid=(S // tm,),
        in_specs=[pl.BlockSpec((tm, D), lambda i: (i, 0)),
                  pl.BlockSpec((D, LANES), lambda i: (0, 0)),
                  pl.BlockSpec((1, LANES), lambda i: (0, 0))],
        out_specs=[out, out, out, pl.BlockSpec((8, LANES), lambda i: (0, 0))],
        out_shape=[jax.ShapeDtypeStruct((S, LANES), jnp.int32), jax.ShapeDtypeStruct((S, LANES), F32),
                   jax.ShapeDtypeStruct((S, LANES), F32), jax.ShapeDtypeStruct((8, LANES), F32)],
        scratch_shapes=[pltpu.VMEM((8, LANES), F32)],
        compiler_params=_cparams(("arbitrary",), 32),
        name="moe_router",
    )(x, w_pad, b_pad)


def _expert_kernel(sb_ref, se_ref, sf_ref, sj_ref, so_ref, sv_ref, tok_ref,
                   x_hbm, wgu_ref, bgu_ref, wdn_ref, bdn_ref, o_ref, wgu_b, wdn_b, acc, stage, xrows, sem):
    s = pl.program_id(0)
    n_steps = pl.num_programs(0)
    nrow = sv_ref[s]
    valid = nrow > 0
    f = sf_ref[s]
    j = sj_ref[s]
    SUB = MOE_SUB
    FT2 = wgu_ref.shape[1]
    RT = ROW_TILES
    UNROLL = 8

    def gather_rows(step):
        base = sb_ref[step] * SUB
        slot = sj_ref[step] % 2

        def issue(i, carry):
            for u in range(UNROLL):
                r = i * UNROLL + u
                t = tok_ref[base + r]
                pltpu.make_async_copy(x_hbm.at[t], stage.at[slot, :, r, :], sem.at[slot]).start()
            return carry

        lax.fori_loop(0, sv_ref[step] // UNROLL, issue, 0)

    @pl.when(s == 0)
    def _():
        gather_rows(0)

    nxt = jnp.minimum(s + 1, n_steps - 1)

    @pl.when((s + 1 < n_steps) & (sv_ref[nxt] > 0) & (sf_ref[nxt] == 0))
    def _():
        gather_rows(nxt)

    @pl.when(valid & (j == 0))
    def _():
        wgu_b[...] = wgu_ref[...].astype(BF16)
        wdn_b[...] = wdn_ref[...].astype(BF16)

    @pl.when(jnp.logical_not(valid))
    def _():
        o_ref[...] = jnp.zeros(o_ref.shape, F32)

    def process(m):
        rows = pl.ds(pl.multiple_of(j * SUB, SUB), m)

        @pl.when(f == 0)
        def _():
            slot = j % 2
            filled = stage.at[slot, :, pl.ds(0, m), :]
            pltpu.make_async_copy(filled, filled, sem.at[slot]).wait()
            for c in range(RT):
                xrows[rows, c * LANES:(c + 1) * LANES] = stage[slot, c, 0:m, :].astype(BF16)

        gu = jnp.dot(xrows[rows, :], wgu_b[...], preferred_element_type=F32) + bgu_ref[...]
        gate = jnp.minimum(gu, SWIGLU_LIMIT)
        act = gate * jax.nn.sigmoid(SWIGLU_ALPHA * gate)
        up1 = jnp.clip(gu, -SWIGLU_LIMIT, SWIGLU_LIMIT) + 1.0
        pr = lax.broadcasted_iota(jnp.int32, (2 * LANES, LANES), 0)
        pc = lax.broadcasted_iota(jnp.int32, (2 * LANES, LANES), 1)
        pick_even = (pr == 2 * pc).astype(F32).astype(BF16)
        hs = []
        for c in range(FT2 // (2 * LANES)):
            parts = []
            for k in range(2):
                cs = slice((2 * c + k) * LANES, (2 * c + k + 1) * LANES)
                parts.append((act[:, cs] * pltpu.roll(up1[:, cs], LANES - 1, 1)).astype(BF16))
            pair = jnp.concatenate(parts, axis=1)
            hs.append(jnp.dot(pair, pick_even, preferred_element_type=F32).astype(BF16))
        h = jnp.concatenate(hs, axis=1)
        y = jnp.dot(h, wdn_b[...], preferred_element_type=F32)

        @pl.when(f == 0)
        def _():
            acc[rows, :] = y + bdn_ref[...]

        @pl.when((f > 0) & (f < MOE_NF - 1))
        def _():
            acc[rows, :] = acc[rows, :] + y

        @pl.when(f == MOE_NF - 1)
        def _():
            yfin = acc[rows, :] + y
            for c in range(RT):
                o_ref[pl.ds(c, m, stride=RT), :] = yfin[:, c * LANES:(c + 1) * LANES]
            if m < SUB:
                o_ref[pl.ds(m * RT, (SUB - m) * RT), :] = jnp.zeros(((SUB - m) * RT, LANES), F32)

    for m in range(MOE_ROWQ, SUB + 1, MOE_ROWQ):
        pl.when(nrow == m)(functools.partial(process, m))


def _experts(meta, tok, x_rows, w_gu, b_gu3, w_dn, b_dn3, layer):
    sb, se, sf, sj, so, sv = meta
    NS = sb.shape[0]
    R = tok.shape[0]
    D = D_MODEL
    RT = ROW_TILES

    def wmap(fn):
        return lambda s, sb, se, sf, sj, so, sv, tok: fn(s, se, sf, so)

    grid_spec = pltpu.PrefetchScalarGridSpec(
        num_scalar_prefetch=7,
        grid=(NS,),
        in_specs=[
            pl.BlockSpec(memory_space=pl.ANY),
            pl.BlockSpec((None, None, D, 2 * MOE_FT), wmap(lambda s, se, sf, so: (layer, se[s], 0, sf[s]))),
            pl.BlockSpec((None, None, 1, 2 * MOE_FT), wmap(lambda s, se, sf, so: (layer, se[s], 0, sf[s]))),
            pl.BlockSpec((None, None, MOE_FT, D), wmap(lambda s, se, sf, so: (layer, se[s], sf[s], 0))),
            pl.BlockSpec((None, None, 1, D), wmap(lambda s, se, sf, so: (layer, se[s], 0, 0))),
        ],
        out_specs=pl.BlockSpec((MOE_SUB * RT, LANES), wmap(lambda s, se, sf, so: (so[s], 0))),
        scratch_shapes=[pltpu.VMEM((D, 2 * MOE_FT), BF16), pltpu.VMEM((MOE_FT, D), BF16),
                        pltpu.VMEM((MOE_J * MOE_SUB, D), F32),
                        pltpu.VMEM((2, RT, MOE_SUB, LANES), F32),
                        pltpu.VMEM((MOE_J * MOE_SUB, D), BF16),
                        pltpu.SemaphoreType.DMA((2,))],
    )
    return pl.pallas_call(
        _expert_kernel,
        grid_spec=grid_spec,
        out_shape=jax.ShapeDtypeStruct((R * RT, LANES), F32),
        compiler_params=_cparams(("arbitrary",), 60),
        name="moe_experts",
    )(sb, se, sf, sj, so, sv, tok, x_rows, w_gu, b_gu3, w_dn, b_dn3)


def _plan_kernel(cnt_ref, sb, se, sf, sj, so, sv, sub_start, src_start, n_used):
    n_sub_max = src_start.shape[0]
    n_steps_max = sb.shape[0]

    def expert(e, carry):
        b0, g0, s = carry
        c = cnt_ref[e]
        ns = (c + MOE_SUB - 1) // MOE_SUB
        sub_start[e] = b0

        def sub_tile(lb, _):
            src_start[b0 + lb] = g0 + lb * MOE_SUB
            n_used[b0 + lb] = jnp.minimum(MOE_SUB, c - lb * MOE_SUB)
            return 0

        lax.fori_loop(0, ns, sub_tile, 0)

        def chunk(ci, s):
            cb0 = b0 + ci * MOE_J
            nj = jnp.minimum(MOE_J, ns - ci * MOE_J)

            def ff_tile(f, s):
                def slot(j, s):
                    rem = c - (ci * MOE_J + j) * MOE_SUB
                    sb[s] = cb0 + j
                    se[s] = e
                    sf[s] = f
                    sj[s] = j
                    so[s] = jnp.where(f == MOE_NF - 1, cb0 + j, cb0)
                    sv[s] = jnp.clip((rem + MOE_ROWQ - 1) // MOE_ROWQ * MOE_ROWQ, MOE_ROWQ, MOE_SUB)
                    return s + 1

                return lax.fori_loop(0, nj, slot, s)

            return lax.fori_loop(0, MOE_NF, ff_tile, s)

        s = lax.fori_loop(0, (ns + MOE_J - 1) // MOE_J, chunk, s)
        return b0 + ns, g0 + c, s

    b_end, _, s_end = lax.fori_loop(0, N_EXPERTS, expert, (jnp.int32(0), jnp.int32(0), jnp.int32(0)))

    def unused_sub_tile(b, _):
        src_start[b] = 0
        n_used[b] = 0
        return 0

    lax.fori_loop(b_end, n_sub_max, unused_sub_tile, 0)

    last = jnp.maximum(s_end - 1, 0)

    def unused_step(s, _):
        sb[s] = sb[last]
        se[s] = se[last]
        sf[s] = sf[last]
        sj[s] = sj[last]
        so[s] = jnp.minimum(b_end + (s - s_end), n_sub_max - 1)
        sv[s] = 0
        return 0

    lax.fori_loop(s_end, n_steps_max, unused_step, 0)


def _moe_tables(e_flat, r_flat, cnt, n_sub_max):
    NS = MOE_NF * n_sub_max
    smem = pl.BlockSpec(memory_space=pltpu.SMEM)
    i32 = lambda n: jax.ShapeDtypeStruct((n,), jnp.int32)
    *meta, sub_start, src_start, n_used = pl.pallas_call(
        _plan_kernel,
        in_specs=[smem],
        out_specs=[smem] * 9,
        out_shape=[i32(NS)] * 6 + [i32(N_EXPERTS), i32(n_sub_max), i32(n_sub_max)],
        name="moe_plan",
    )(cnt)
    dest = sub_start[e_flat] * MOE_SUB + r_flat

    n_asg = e_flat.shape[0]
    _, tok_sorted = lax.sort_key_val(dest, jnp.arange(n_asg, dtype=jnp.int32) // TOP_K)
    lane = jnp.arange(MOE_SUB, dtype=jnp.int32)[None, :]
    src = jnp.minimum(src_start[:, None] + lane, n_asg - 1)
    tok = jnp.where(lane < n_used[:, None], tok_sorted[src], 0).reshape(-1)
    return dest, tok, tuple(meta)


def _combine_kernel(dest_ref, y_hbm, g4_ref, x_ref, g_ref, b_ref, o_ref, ob_ref, stage, ytile, ysum, sem):
    i = pl.program_id(0)
    n = pl.num_programs(0)
    tm = x_ref.shape[0]
    RT = ROW_TILES
    n_rows = tm * TOP_K

    def start_row(step, to_slot, r):
        d = dest_ref[step * n_rows + r]
        pltpu.make_async_copy(y_hbm.at[pl.ds(pl.multiple_of(d * RT, RT), RT), :],
                              stage.at[to_slot, pl.ds(pl.multiple_of(r * RT, RT), RT), :],
                              sem.at[to_slot]).start()

    def gather_rows(step, to_slot):
        def issue(r, carry):
            start_row(step, to_slot, r)
            return carry

        lax.fori_loop(0, n_rows, issue, 0, unroll=8)

    @pl.when(i == 0)
    def _():
        gather_rows(0, 0)

    @pl.when(i + 1 < n)
    def _():
        gather_rows(i + 1, (i + 1) % 2)

    slot = i % 2
    pltpu.make_async_copy(y_hbm.at[pl.ds(0, n_rows * RT), :], stage.at[slot], sem.at[slot]).wait()

    def token_sum(t, carry):
        r0 = pl.multiple_of(t * (TOP_K * RT), TOP_K * RT)
        y = g4_ref[TOP_K * t] * stage[slot, pl.ds(r0, RT), :]
        for k in range(1, TOP_K):
            y = y + g4_ref[TOP_K * t + k] * stage[slot, pl.ds(r0 + k * RT, RT), :]
        ytile[pl.ds(pl.multiple_of(t * RT, RT), RT), :] = y
        return carry

    lax.fori_loop(0, tm, token_sum, 0, unroll=4)
    for c in range(RT):
        ysum[:, c * LANES:(c + 1) * LANES] = ytile[pl.ds(c, tm, stride=RT), :]
    out = _layer_norm_rows(DN_ALPHA * x_ref[...] + ysum[...], g_ref[...], b_ref[...])
    o_ref[...] = out
    ob_ref[...] = out.astype(BF16)


def _combine_ln(dest_flat, y_rows, g4, x, g, b, tm=128):
    S, D = x.shape
    vec = pl.BlockSpec((1, D), lambda i, d: (0, 0))
    grid_spec = pltpu.PrefetchScalarGridSpec(
        num_scalar_prefetch=1,
        grid=(S // tm,),
        in_specs=[pl.BlockSpec(memory_space=pl.ANY),
                  pl.BlockSpec((tm * TOP_K,), lambda i, d: (i,), memory_space=pltpu.SMEM),
                  pl.BlockSpec((tm, D), lambda i, d: (i, 0)), vec, vec],
        out_specs=[pl.BlockSpec((tm, D), lambda i, d: (i, 0)), pl.BlockSpec((tm, D), lambda i, d: (i, 0))],
        scratch_shapes=[pltpu.VMEM((2, tm * TOP_K * ROW_TILES, LANES), F32),
                        pltpu.VMEM((tm * ROW_TILES, LANES), F32),
                        pltpu.VMEM((tm, D), F32),
                        pltpu.SemaphoreType.DMA((2,))],
    )
    return pl.pallas_call(
        _combine_kernel,
        grid_spec=grid_spec,
        out_shape=[jax.ShapeDtypeStruct((S, D), F32), jax.ShapeDtypeStruct((S, D), BF16)],
        compiler_params=_cparams(("arbitrary",), 48),
        name="moe_combine_ln",
    )(dest_flat, y_rows, g4, x, g.reshape(1, D), b.reshape(1, D))


def _nsa_group(u_bf, gate_logits, cmp_pos, cmp_w1, cmp_w2, rel_bias, g_nsa):
    S = u_bf.shape[0]
    G = NSA_KV_HEADS
    kvc = u_bf[:, NSA_WIDTH:NSA_WIDTH + 2 * KV_WIDTH]
    kv_rows = kvc.reshape(S, 2 * G, HEAD_DIM).transpose(1, 0, 2).reshape(2 * G, S // CMP_STRIDE,
                                                                          CMP_STRIDE * HEAD_DIM)
    kcv = _compress(kv_rows, cmp_pos, cmp_w1, cmp_w2)
    o_c, sel = _cmp_attention(rel_bias, u_bf, kcv)
    kv_pad = jnp.pad(u_bf[:, NSA_WIDTH + 2 * KV_WIDTH:], ((KV_FRONT, 0), (0, 0)))
    return _sel_attention(rel_bias, u_bf, kv_pad, sel, o_c, gate_logits, g_nsa)


def _moe(x, x_rows, w_router, b_router, w_gu, b_gu, w_dn, b_dn, ln_g, ln_b, layer):
    S, D = x.shape
    E = N_EXPERTS
    w_pad = jnp.pad(w_router, ((0, 0), (0, LANES - E)))
    b_pad = jnp.pad(b_router, (0, LANES - E), constant_values=NEG_INF).reshape(1, LANES)
    e_out, g_out, r_out, cnt_out = _router(x, w_pad, b_pad)
    e_flat = e_out[:, :TOP_K].reshape(-1)
    r_flat = r_out[:, :TOP_K].reshape(-1).astype(jnp.int32)
    cnt = cnt_out[0, :E].astype(jnp.int32)
    n_sub_max = (S * TOP_K) // MOE_SUB + E
    flat, tok, meta = _moe_tables(e_flat, r_flat, cnt, n_sub_max)
    y_rows = _experts(meta, tok, x_rows.reshape(S, ROW_TILES, LANES), w_gu, b_gu.reshape(DEPTH, E, 1, 2 * D_FF), w_dn,
                      b_dn.reshape(DEPTH, E, 1, D), layer)
    return _combine_ln(flat, y_rows, g_out[:, :TOP_K].reshape(-1), x, ln_g, ln_b)


def kernel(x, w_in, conv_w, conv_b, lru_wa, lru_ba, lru_wi, lru_bi, lru_lambda, cmp_pos, cmp_w1, cmp_w2,
           rel_bias, g_lru, g_nsa, w_out, ln1_g, ln1_b, w_router, b_router, w_gate_up, b_gate_up, w_down,
           b_down, ln2_g, ln2_b):
    B, S, D = x.shape
    assert B == 1
    x = x.reshape(S, D)
    xb = x.astype(BF16)
    n_gate = 3 * NSA_HEADS
    for l in range(DEPTH):
        u_f32 = _project(xb, w_in, l, 0, 2 * LRU_WIDTH // 512, 512, F32)
        u_bf = _project(xb, w_in, l, 2 * LRU_WIDTH // 512, (MAIN_COLS - 2 * LRU_WIDTH) // 512, 512, BF16)
        w_gate = jnp.pad(w_in[l, :, MAIN_COLS:], ((0, 0), (0, LANES - n_gate)))[None]
        gate_logits = _project(xb, w_gate, 0, 0, 1, LANES, F32)
        y_lru = _lru_group(u_f32, conv_w[l], conv_b[l], lru_wa[l], lru_ba[l], lru_wi[l], lru_bi[l],
                           lru_lambda[l], g_lru[l])
        y_nsa = _nsa_group(u_bf, gate_logits, cmp_pos[l], cmp_w1[l], cmp_w2[l], rel_bias, g_nsa[l])
        x, x_rows = _out_proj_ln(y_lru, y_nsa, w_out[l].astype(BF16), x, ln1_g[l], ln1_b[l])
        x, xb = _moe(x, x_rows, w_router[l], b_router[l], w_gate_up, b_gate_up, w_down, b_down,
                     ln2_g[l], ln2_b[l], l)
    return x.reshape(B, S, D)
```

```python
import functools
import math

import numpy as np
import jax
import jax.numpy as jnp
from jax import lax
from jax.experimental import pallas as pl
from jax.experimental.pallas import tpu as pltpu

D_MODEL = 2048
DEPTH = 2
LRU_WIDTH = 1024
LRU_BLOCKS = 8
LRU_BLOCK_W = LRU_WIDTH // LRU_BLOCKS
CONV_WIDTH = 4
LRU_C = 8.0
NSA_HEADS = 8
NSA_KV_HEADS = 2
HEADS_PER_GROUP = NSA_HEADS // NSA_KV_HEADS
HEAD_DIM = 128
NSA_WIDTH = NSA_HEADS * HEAD_DIM
KV_WIDTH = NSA_KV_HEADS * HEAD_DIM
CMP_BLOCK = 32
CMP_STRIDE = 16
SEL_BLOCK = 64
N_SEL = 16
WINDOW = 512
Q_BLOCK = 128
N_BUCKETS = 32
MAX_DISTANCE = 128
N_EXPERTS = 32
TOP_K = 4
D_FF = 2048
SWIGLU_LIMIT = 7.0
SWIGLU_ALPHA = 1.702
DN_ALPHA = (2 * DEPTH) ** 0.25
NORM_EPS = 1e-5
NEG_INF = -1e30
FORCED_SCORE = 1e9
ATTN_SCALE = HEAD_DIM ** -0.5

LOG2E = math.log2(math.e)
N_FORCED = 3
FAR_CHUNK = 1024
KV_FRONT = FAR_CHUNK
PAD_BLK = KV_FRONT // SEL_BLOCK
M_INIT = -1e9

LANES = 128
MIB = 1024 * 1024
MAIN_COLS = 2 * LRU_WIDTH + NSA_WIDTH + 6 * KV_WIDTH
BF16 = jnp.bfloat16
F32 = jnp.float32

MOE_SUB = 256
MOE_ROWQ = 64
MOE_J = 5
MOE_FT = 512
MOE_NF = D_FF // MOE_FT
ROW_TILES = D_MODEL // LANES


def _bucket_thresholds():
    max_exact = N_BUCKETS // 2
    out = []
    for b in range(1, N_BUCKETS):
        if b <= max_exact:
            out.append(b)
            continue
        d = max_exact
        while True:
            v = math.log(d / max_exact) / math.log(MAX_DISTANCE / max_exact) * (N_BUCKETS - max_exact)
            assert abs(v - round(v)) > 1e-3 or d == max_exact
            if min(max_exact + int(v), N_BUCKETS - 1) >= b:
                break
            d += 1
        out.append(d)
    return tuple(out)


BUCKET_TH = _bucket_thresholds()


def _cparams(semantics, vmem_mib):
    return pltpu.CompilerParams(dimension_semantics=semantics, vmem_limit_bytes=vmem_mib * MIB)


def _nt_dot(a, b):
    return lax.dot_general(a, b, (((1,), (1,)), ((), ())), preferred_element_type=F32)


def _bias_of_dist(rb_ref, head, d):
    b = jnp.full(d.shape, rb_ref[0, head], F32)
    for k in range(1, N_BUCKETS):
        b = jnp.where(d >= BUCKET_TH[k - 1], rb_ref[k, head], b)
    return b


def _mm_kernel(x_ref, w_ref, o_ref, wb_ref):
    @pl.when(pl.program_id(1) == 0)
    def _():
        wb_ref[...] = w_ref[...].astype(BF16)

    o_ref[...] = jnp.dot(x_ref[...], wb_ref[...], preferred_element_type=F32).astype(o_ref.dtype)


def _project(xb, w3, layer, col_blk0, n_blks, tn, out_dtype, tm=1024):
    S, D = xb.shape
    return pl.pallas_call(
        _mm_kernel,
        grid=(n_blks, S // tm),
        in_specs=[pl.BlockSpec((tm, D), lambda n, m: (m, 0)),
                  pl.BlockSpec((None, D, tn), lambda n, m: (layer, 0, col_blk0 + n))],
        out_specs=pl.BlockSpec((tm, tn), lambda n, m: (m, n)),
        out_shape=jax.ShapeDtypeStruct((S, n_blks * tn), out_dtype),
        scratch_shapes=[pltpu.VMEM((D, tn), BF16)],
        compiler_params=_cparams(("arbitrary", "arbitrary"), 48),
        name="in_proj",
    )(xb, w3)


def _gelu_tanh(x):
    return 0.5 * x * (1.0 + jnp.tanh(math.sqrt(2.0 / math.pi) * (x + 0.044715 * (x * x * x))))


def _lru_kernel(xr_ref, xg_ref, cw_ref, cb_ref, wa_ref, ba_ref, wi_ref, bi_ref, lam_ref, g_ref,
                o_ref, xbuf, hc):
    i = pl.program_id(0)
    T = xr_ref.shape[0]
    W = xr_ref.shape[1]
    HALO = 8

    @pl.when(i == 0)
    def _():
        xbuf[0:HALO, :] = jnp.zeros((HALO, W), F32)
        hc[...] = jnp.zeros(hc.shape, F32)

    @pl.when(i > 0)
    def _():
        xbuf[0:HALO, :] = xbuf[T:T + HALO, :]

    xbuf[HALO:HALO + T, :] = xr_ref[...]
    cw = cw_ref[...]
    xc = cb_ref[...] + xbuf[HALO - 3:HALO - 3 + T, :] * cw[0:1]
    for k in range(1, CONV_WIDTH):
        xc = xc + xbuf[HALO - 3 + k:HALO - 3 + k + T, :] * cw[k:k + 1]

    xcb = xc.astype(BF16)
    ra, ia = [], []
    for n in range(LRU_BLOCKS):
        blk = xcb[:, n * LRU_BLOCK_W:(n + 1) * LRU_BLOCK_W]
        ra.append(jnp.dot(blk, wa_ref[n].astype(BF16), preferred_element_type=F32))
        ia.append(jnp.dot(blk, wi_ref[n].astype(BF16), preferred_element_type=F32))
    r = jax.nn.sigmoid(jnp.concatenate(ra, axis=1) + ba_ref[...])
    ig = jax.nn.sigmoid(jnp.concatenate(ia, axis=1) + bi_ref[...])

    z = -lam_ref[...]
    softplus = jnp.maximum(z, 0.0) + jnp.log(1.0 + jnp.exp(-jnp.abs(z)))
    log_a = -LRU_C * r * softplus
    a = jnp.exp(log_a)
    u = jnp.sqrt(1.0 - a * a) * (ig * xc)

    SUBL = 8
    sub = lax.broadcasted_iota(jnp.int32, (T, W), 0) % SUBL
    A, U = a, u
    d = 1
    while d < SUBL:
        keep = sub >= d
        a_sh = jnp.where(keep, pltpu.roll(A, d, 0), 1.0)
        u_sh = jnp.where(keep, pltpu.roll(U, d, 0), 0.0)
        U = A * u_sh + U
        A = A * a_sh
        d *= 2
    state = hc[...]
    groups = []
    for k in range(T // SUBL):
        hg = U[k * SUBL:(k + 1) * SUBL] + A[k * SUBL:(k + 1) * SUBL] * state
        state = hg[SUBL - 1:SUBL]
        groups.append(hg)
    h = jnp.concatenate(groups, axis=0)
    hc[...] = state

    y = h * _gelu_tanh(xg_ref[...])
    ms = jnp.mean(y * y, axis=-1, keepdims=True)
    o_ref[...] = (y * lax.rsqrt(ms + NORM_EPS) * g_ref[...]).astype(o_ref.dtype)


def _lru_group(u_f32, conv_w, conv_b, wa, ba, wi, bi, lam, g_lru, T=256):
    S = u_f32.shape[0]
    W = LRU_WIDTH
    row = lambda v: v.reshape(1, W)
    vec = pl.BlockSpec((1, W), lambda i: (0, 0))
    mat = pl.BlockSpec((LRU_BLOCKS, LRU_BLOCK_W, LRU_BLOCK_W), lambda i: (0, 0, 0))
    return pl.pallas_call(
        _lru_kernel,
        grid=(S // T,),
        in_specs=[pl.BlockSpec((T, W), lambda i: (i, 0)),
                  pl.BlockSpec((T, W), lambda i: (i, 1)),
                  pl.BlockSpec((CONV_WIDTH, W), lambda i: (0, 0)),
                  vec, mat, vec, mat, vec, vec, vec],
        out_specs=pl.BlockSpec((T, W), lambda i: (i, 0)),
        out_shape=jax.ShapeDtypeStruct((S, W), BF16),
        scratch_shapes=[pltpu.VMEM((T + 8, W), F32), pltpu.VMEM((1, W), F32)],
        compiler_params=_cparams(("arbitrary",), 48),
        name="rg_lru",
    )(u_f32, u_f32, conv_w, row(conv_b), wa, row(ba), wi, row(bi), row(lam), row(g_lru))


def _compress_kernel(x_ref, pos_ref, w1_ref, w2_ref, o_ref):
    NJ = x_ref.shape[0]
    half = (CMP_BLOCK // 2) * HEAD_DIM
    x = x_ref[...].astype(F32)
    a = (x + pos_ref[0:1, :]).astype(BF16)
    b = (x + pos_ref[1:2, :]).astype(BF16)
    y1 = jnp.dot(a, w1_ref[0:half, :].astype(BF16), preferred_element_type=F32)
    y2 = jnp.dot(b, w1_ref[half:2 * half, :].astype(BF16), preferred_element_type=F32)
    hmid = y1 + pltpu.roll(y2, NJ - 1, 0)
    o_ref[...] = jnp.dot(_gelu_tanh(hmid).astype(BF16), w2_ref[...].astype(BF16),
                         preferred_element_type=F32).astype(o_ref.dtype)


def _compress(kv_rows, pos, w1, w2):
    _, NJ, RW = kv_rows.shape
    return pl.pallas_call(
        _compress_kernel,
        grid=(4,),
        in_specs=[pl.BlockSpec((None, NJ, RW), lambda a: (a, 0, 0)),
                  pl.BlockSpec((None, 2, RW), lambda a: (a // 2, 0, 0)),
                  pl.BlockSpec((None, CMP_BLOCK * HEAD_DIM, HEAD_DIM), lambda a: (a // 2, 0, 0)),
                  pl.BlockSpec((None, HEAD_DIM, HEAD_DIM), lambda a: (a // 2, 0, 0))],
        out_specs=pl.BlockSpec((None, NJ, HEAD_DIM), lambda a: (a, 0, 0)),
        out_shape=jax.ShapeDtypeStruct((4, NJ, HEAD_DIM), BF16),
        compiler_params=_cparams(("arbitrary",), 32),
        name="kv_compress",
    )(kv_rows, pos.reshape(2, 2, RW), w1, w2)


def _cmp_attn_kernel(rb_ref, q_ref, kcv_ref, oc_ref, sel_ref, tcat):
    qb = pl.program_id(0)
    t0 = qb * Q_BLOCK
    NJ = kcv_ref.shape[1]
    NB = NJ * CMP_STRIDE // SEL_BLOCK
    NBP = sel_ref.shape[2]
    n_sel = min(N_SEL, NB)
    LOCAL = 2 * Q_BLOCK // CMP_STRIDE

    @pl.when(qb == 0)
    def _():
        i = lax.broadcasted_iota(jnp.int32, (Q_BLOCK, LANES), 0)
        j = lax.broadcasted_iota(jnp.int32, (Q_BLOCK, LANES), 1)
        d = jnp.maximum(i - CMP_STRIDE * j + (Q_BLOCK - CMP_STRIDE), 0)
        for h in range(NSA_HEADS):
            corr = jnp.where(j < LOCAL, (_bias_of_dist(rb_ref, h, d) - rb_ref[N_BUCKETS - 1, h]) * LOG2E, 0.0)
            hi = corr.astype(BF16)
            tcat[h, :, 0:LANES] = hi
            tcat[h, :, LANES:2 * LANES] = (corr - hi.astype(F32)).astype(BF16)

    def tile(W):
        row = lax.broadcasted_iota(jnp.int32, (Q_BLOCK, W), 0)
        col = lax.broadcasted_iota(jnp.int32, (Q_BLOCK, W), 1)
        mask_add = jnp.where(CMP_STRIDE * col + (CMP_BLOCK - 1) <= t0 + row, 0.0, NEG_INF)
        jj = lax.broadcasted_iota(jnp.int32, (2 * LANES, W), 0) % LANES
        cc = lax.broadcasted_iota(jnp.int32, (2 * LANES, W), 1)
        place = (cc == (Q_BLOCK // CMP_STRIDE) * (qb - 1) + jj).astype(F32).astype(BF16)
        bj = lax.broadcasted_iota(jnp.int32, (NB, W), 0)
        bc = lax.broadcasted_iota(jnp.int32, (NB, W), 1)
        ratio = SEL_BLOCK // CMP_STRIDE
        overlap_t = ((bc <= ratio * bj + ratio - 1) & (bc >= ratio * bj - 1)).astype(F32).astype(BF16)

        sj = lax.broadcasted_iota(jnp.int32, (NB, Q_BLOCK), 0)
        st = t0 + lax.broadcasted_iota(jnp.int32, (NB, Q_BLOCK), 1)
        cur = st // SEL_BLOCK
        blk_ok = sj <= cur
        forced = (sj == 0) | (sj == cur) | (sj == cur - 1)

        scores = []
        for g in range(NSA_KV_HEADS):
            kc = kcv_ref[g, 0:W, :]
            vc = kcv_ref[NSA_KV_HEADS + g, 0:W, :]
            psum = jnp.zeros((Q_BLOCK, W), F32)
            for h in range(HEADS_PER_GROUP):
                hh = g * HEADS_PER_GROUP + h
                qh = (q_ref[:, hh * HEAD_DIM:(hh + 1) * HEAD_DIM].astype(F32) * (ATTN_SCALE * LOG2E)).astype(BF16)
                z = _nt_dot(qh, kc) + jnp.dot(tcat[hh], place, preferred_element_type=F32) + mask_add
                m = jnp.maximum(jnp.max(z, axis=-1, keepdims=True), M_INIT)
                e = jnp.exp2(z - m)
                l = jnp.sum(e, axis=-1, keepdims=True)
                p = e * (1.0 / jnp.where(l > 0.0, l, 1.0))
                oc_ref[:, hh * HEAD_DIM:(hh + 1) * HEAD_DIM] = jnp.dot(
                    p.astype(BF16), vc, preferred_element_type=F32)
                psum = psum + p
            p_hi = psum.astype(BF16)
            p_lo = (psum - p_hi.astype(F32)).astype(BF16)
            imp_t = _nt_dot(overlap_t, p_hi) + _nt_dot(overlap_t, p_lo)
            scores.append(jnp.where(blk_ok, jnp.where(forced, FORCED_SCORE, imp_t), NEG_INF))

        score = jnp.concatenate(scores, axis=1)
        sjf = jnp.concatenate([sj.astype(F32)] * NSA_KV_HEADS, axis=1)
        chosen = jnp.concatenate([jnp.where(forced, 1.0, 0.0)] * NSA_KV_HEADS, axis=1) > 0.5
        score = jnp.where(chosen, -3e38, score)
        for _ in range(max(n_sel - N_FORCED, 0)):
            mx = jnp.max(score, axis=0, keepdims=True)
            first = jnp.min(jnp.where(score == mx, sjf, float(NB)), axis=0, keepdims=True)
            pick = sjf == first
            chosen = chosen | pick
            score = jnp.where(pick, -3e38, score)
        picked = jnp.where(chosen, 1.0, 0.0)
        for g in range(NSA_KV_HEADS):
            pen_t = jnp.where((picked[:, g * Q_BLOCK:(g + 1) * Q_BLOCK] > 0.5) & blk_ok, 0.0, NEG_INF)
            padded = jnp.concatenate([jnp.full((PAD_BLK, Q_BLOCK), NEG_INF, F32), pen_t,
                                      jnp.zeros((NBP - PAD_BLK - NB, Q_BLOCK), F32)], axis=0)
            pen = jnp.concatenate([padded[r:r + LANES].T for r in range(0, NBP, LANES)], axis=1)
            sel_ref[g] = pen.astype(sel_ref.dtype)

    step = min(LANES, NJ)
    need = (Q_BLOCK // CMP_STRIDE) * qb + (Q_BLOCK - CMP_BLOCK) // CMP_STRIDE + 1
    width = jnp.clip((need + step - 1) // step * step, min(2 * step, NJ), NJ)
    for W in range(min(2 * step, NJ), NJ + 1, step):
        pl.when(width == W)(functools.partial(tile, W))


def _cmp_attention(rel_bias, u_bf, kcv):
    S = u_bf.shape[0]
    NJ = kcv.shape[1]
    NB = -(-(S // SEL_BLOCK + PAD_BLK) // LANES) * LANES
    return pl.pallas_call(
        _cmp_attn_kernel,
        grid=(S // Q_BLOCK,),
        in_specs=[pl.BlockSpec(memory_space=pltpu.SMEM),
                  pl.BlockSpec((Q_BLOCK, NSA_WIDTH), lambda i: (i, 0)),
                  pl.BlockSpec((4, NJ, HEAD_DIM), lambda i: (0, 0, 0))],
        out_specs=[pl.BlockSpec((Q_BLOCK, NSA_WIDTH), lambda i: (i, 0)),
                   pl.BlockSpec((NSA_KV_HEADS, Q_BLOCK, NB), lambda i: (0, i, 0))],
        out_shape=[jax.ShapeDtypeStruct((S, NSA_WIDTH), F32),
                   jax.ShapeDtypeStruct((NSA_KV_HEADS, S, NB), BF16)],
        scratch_shapes=[pltpu.VMEM((NSA_HEADS, Q_BLOCK, 2 * LANES), BF16)],
        compiler_params=_cparams(("arbitrary",), 32),
        name="nsa_compressed",
    )(rel_bias, u_bf, kcv)


def _sel_attn_kernel(rb_ref, q_ref, ks_ref, vs_ref, kw_ref, vw_ref, sel_ref, oc_ref, gl_ref, g_ref,
                     o_ref, tbl_s, tbl_w, yacc):
    qb = pl.program_id(0)
    Q = Q_BLOCK
    HPG = HEADS_PER_GROUP
    TW = WINDOW + Q
    FAR = FAR_CHUNK
    NBP = sel_ref.shape[2]
    t0 = qb * Q

    @pl.when(qb == 0)
    def _():
        i = lax.broadcasted_iota(jnp.int32, (Q, TW), 0)
        c = lax.broadcasted_iota(jnp.int32, (Q, TW), 1)
        d = i - c + WINDOW
        dd = jnp.maximum(d, 0)
        for h in range(NSA_HEADS):
            corr = (_bias_of_dist(rb_ref, h, dd) - rb_ref[N_BUCKETS - 1, h]) * LOG2E
            tbl_s[h] = jnp.where(d < 0, NEG_INF, corr)
            tbl_w[h] = jnp.where((d < 0) | (d >= WINDOW), NEG_INF, corr)

    gates = jax.nn.sigmoid(gl_ref[...])
    col_w = lax.broadcasted_iota(jnp.int32, (1, TW), 1)
    before_start = jnp.where(col_w < WINDOW - t0, NEG_INF, 0.0)
    p_tail = pl.multiple_of(t0 + (KV_FRONT - WINDOW), Q)

    G = NSA_KV_HEADS
    q4 = [jnp.concatenate(
        [(q_ref[:, (g * HPG + h) * HEAD_DIM:(g * HPG + h + 1) * HEAD_DIM].astype(F32)
          * (ATTN_SCALE * LOG2E)).astype(BF16) for h in range(HPG)], axis=0) for g in range(G)]

    def attend(g, k_ref, v_ref, p0, width, add, carry):
        m, l, acc = carry
        cols = slice(g * HEAD_DIM, (g + 1) * HEAD_DIM)
        kk = k_ref[pl.ds(p0, width), cols]
        vv = v_ref[pl.ds(p0, width), cols]
        s = _nt_dot(q4[g], kk)
        z = (s.reshape(HPG, Q, width) + add).reshape(HPG * Q, width)
        m_new = jnp.maximum(m, jnp.max(z, axis=-1, keepdims=True))
        alpha = jnp.exp2(m - m_new)
        e = jnp.exp2(z - m_new)
        l = alpha * l + jnp.sum(e, axis=-1, keepdims=True)
        acc = alpha * acc + jnp.dot(e.astype(BF16), vv, preferred_element_type=F32)
        return m_new, l, acc

    def block_mask(g, p0, width):
        bp = lax.broadcasted_iota(jnp.int32, (NBP, width), 0)
        kb = lax.broadcasted_iota(jnp.int32, (NBP, width), 1) // SEL_BLOCK
        expand = (bp - p0 // SEL_BLOCK == kb).astype(F32).astype(BF16)
        return jnp.dot(sel_ref[g], expand, preferred_element_type=F32)[None]

    init = (jnp.full((HPG * Q, 1), M_INIT, F32), jnp.zeros((HPG * Q, 1), F32),
            jnp.zeros((HPG * Q, HEAD_DIM), F32))

    def far_body(i, carry):
        p0 = pl.multiple_of(p_tail - FAR * (i + 1), Q)
        return tuple(attend(g, ks_ref, vs_ref, p0, FAR, block_mask(g, p0, FAR), carry[g]) for g in range(G))

    n_far = (jnp.maximum(t0 - WINDOW, 0) + FAR - 1) // FAR
    far = lax.fori_loop(0, n_far, far_body, (init,) * G)

    for g in range(G):
        heads = slice(g * HPG, (g + 1) * HPG)
        m_s, l_s, acc_s = attend(g, ks_ref, vs_ref, p_tail, TW, tbl_s[heads] + block_mask(g, p_tail, TW), far[g])
        o_s = acc_s / jnp.where(l_s > 0.0, l_s, 1.0)

        m_w, l_w, acc_w = attend(g, kw_ref, vw_ref, p_tail, TW, tbl_w[heads] + before_start[None], init)
        o_w = acc_w / jnp.where(l_w > 0.0, l_w, 1.0)

        for h in range(HPG):
            hh = g * HPG + h
            hc = slice(hh * HEAD_DIM, (hh + 1) * HEAD_DIM)
            rows = slice(h * Q, (h + 1) * Q)
            yacc[:, hc] = (gates[:, 3 * hh:3 * hh + 1] * oc_ref[:, hc]
                           + gates[:, 3 * hh + 1:3 * hh + 2] * o_s[rows]
                           + gates[:, 3 * hh + 2:3 * hh + 3] * o_w[rows])

    y = yacc[...]
    ms = jnp.mean(y * y, axis=-1, keepdims=True)
    o_ref[...] = (y * lax.rsqrt(ms + NORM_EPS) * g_ref[...]).astype(o_ref.dtype)


def _sel_attention(rel_bias, u_bf, kv_pad, sel, o_c, gate_logits, g_nsa):
    S = u_bf.shape[0]
    NB = sel.shape[2]

    def kv_spec(j):
        return pl.BlockSpec((KV_FRONT + S, KV_WIDTH), lambda i: (0, j))

    return pl.pallas_call(
        _sel_attn_kernel,
        grid=(S // Q_BLOCK,),
        in_specs=[pl.BlockSpec(memory_space=pltpu.SMEM),
                  pl.BlockSpec((Q_BLOCK, NSA_WIDTH), lambda i: (i, 0)),
                  kv_spec(0), kv_spec(1), kv_spec(2), kv_spec(3),
                  pl.BlockSpec((NSA_KV_HEADS, Q_BLOCK, NB), lambda i: (0, i, 0)),
                  pl.BlockSpec((Q_BLOCK, NSA_WIDTH), lambda i: (i, 0)),
                  pl.BlockSpec((Q_BLOCK, LANES), lambda i: (i, 0)),
                  pl.BlockSpec((1, NSA_WIDTH), lambda i: (0, 0))],
        out_specs=pl.BlockSpec((Q_BLOCK, NSA_WIDTH), lambda i: (i, 0)),
        out_shape=jax.ShapeDtypeStruct((S, NSA_WIDTH), BF16),
        scratch_shapes=[pltpu.VMEM((NSA_HEADS, Q_BLOCK, WINDOW + Q_BLOCK), F32),
                        pltpu.VMEM((NSA_HEADS, Q_BLOCK, WINDOW + Q_BLOCK), F32),
                        pltpu.VMEM((Q_BLOCK, NSA_WIDTH), F32)],
        compiler_params=_cparams(("arbitrary",), 60),
        name="nsa_selected_window",
    )(rel_bias, u_bf, kv_pad, kv_pad, kv_pad, kv_pad, sel, o_c, gate_logits, g_nsa.reshape(1, NSA_WIDTH))


def _layer_norm_rows(z, g, b):
    mu = jnp.mean(z, axis=-1, keepdims=True)
    zc = z - mu
    var = jnp.mean(zc * zc, axis=-1, keepdims=True)
    return zc * lax.rsqrt(var + NORM_EPS) * g + b


def _outproj_kernel(yl_ref, yn_ref, w_ref, x_ref, g_ref, b_ref, o_ref, or_ref):
    tm = x_ref.shape[0]
    acc = jnp.dot(yl_ref[...], w_ref[0:LRU_WIDTH, :], preferred_element_type=F32)
    acc = acc + jnp.dot(yn_ref[...], w_ref[LRU_WIDTH:LRU_WIDTH + NSA_WIDTH, :], preferred_element_type=F32)
    y = _layer_norm_rows(DN_ALPHA * x_ref[...] + acc, g_ref[...], b_ref[...])
    o_ref[...] = y
    for c in range(ROW_TILES):
        or_ref[pl.ds(c, tm, stride=ROW_TILES), :] = y[:, c * LANES:(c + 1) * LANES]


def _out_proj_ln(y_lru, y_nsa, w_out_b, x, g, b, tm=256):
    S, D = x.shape
    vec = pl.BlockSpec((1, D), lambda i: (0, 0))
    return pl.pallas_call(
        _outproj_kernel,
        grid=(S // tm,),
        in_specs=[pl.BlockSpec((tm, LRU_WIDTH), lambda i: (i, 0)),
                  pl.BlockSpec((tm, NSA_WIDTH), lambda i: (i, 0)),
                  pl.BlockSpec((LRU_WIDTH + NSA_WIDTH, D), lambda i: (0, 0)),
                  pl.BlockSpec((tm, D), lambda i: (i, 0)), vec, vec],
        out_specs=[pl.BlockSpec((tm, D), lambda i: (i, 0)),
                   pl.BlockSpec((tm * ROW_TILES, LANES), lambda i: (i, 0))],
        out_shape=[jax.ShapeDtypeStruct((S, D), F32), jax.ShapeDtypeStruct((S * ROW_TILES, LANES), F32)],
        compiler_params=_cparams(("arbitrary",), 48),
        name="out_proj_ln",
    )(y_lru, y_nsa, w_out_b, x, g.reshape(1, D), b.reshape(1, D))


def _router_kernel(x_ref, w_ref, b_ref, e_ref, g_ref, r_ref, cnt_ref, carry):
    i = pl.program_id(0)
    tm = x_ref.shape[0]

    @pl.when(i == 0)
    def _():
        carry[...] = jnp.zeros(carry.shape, F32)

    logits = jnp.dot(x_ref[...], w_ref[...], preferred_element_type=F32,
                     precision=lax.Precision.HIGHEST) + b_ref[...]
    lane = lax.broadcasted_iota(jnp.int32, (tm, LANES), 1)
    lanef = lane.astype(F32)
    s = logits
    picks = []
    chosen = jnp.zeros((tm, LANES), jnp.bool_)
    for _ in range(TOP_K):
        mx = jnp.max(s, axis=-1, keepdims=True)
        first = jnp.min(jnp.where(s == mx, lanef, float(LANES)), axis=-1, keepdims=True)
        pick = lanef == first
        picks.append((pick, first))
        chosen = chosen | pick
        s = jnp.where(pick, -3e38, s)
    top = jnp.max(logits, axis=-1, keepdims=True)
    ex = jnp.where(chosen, jnp.exp(logits - top), 0.0)
    gate = ex / jnp.sum(ex, axis=-1, keepdims=True)

    onehot = jnp.where(chosen, 1.0, 0.0)
    rr = lax.broadcasted_iota(jnp.int32, (tm, tm), 0)
    rc = lax.broadcasted_iota(jnp.int32, (tm, tm), 1)
    before = (rr > rc).astype(F32).astype(BF16)
    rank = jnp.dot(before, onehot.astype(BF16), preferred_element_type=F32) + carry[0:1, :]
    carry[0:1, :] = carry[0:1, :] + jnp.sum(onehot, axis=0, keepdims=True)
    cnt_ref[...] = carry[...]

    e_out = jnp.zeros((tm, LANES), jnp.int32)
    g_out = jnp.zeros((tm, LANES), F32)
    r_out = jnp.zeros((tm, LANES), F32)
    for k, (pick, first) in enumerate(picks):
        e_out = jnp.where(lane == k, first.astype(jnp.int32), e_out)
        g_out = jnp.where(lane == k, jnp.sum(jnp.where(pick, gate, 0.0), axis=-1, keepdims=True), g_out)
        r_out = jnp.where(lane == k, jnp.sum(jnp.where(pick, rank, 0.0), axis=-1, keepdims=True), r_out)
    e_ref[...] = e_out
    g_ref[...] = g_out
    r_ref[...] = r_out


def _router(x, w_pad, b_pad, tm=256):
    S, D = x.shape
    out = pl.BlockSpec((tm, LANES), lambda i: (i, 0))
    return pl.pallas_call(
        _router_kernel,
        grid=(S // tm,),
        in_specs=[pl.BlockSpec((tm, D), lambda i: (i, 0)),
                  pl.BlockSpec((D, LANES), lambda i: (0, 0)),
                  pl.BlockSpec((1, LANES), lambda i: (0, 0))],
        out_specs=[out, out, out, pl.BlockSpec((8, LANES), lambda i: (0, 0))],
        out_shape=[jax.ShapeDtypeStruct((S, LANES), jnp.int32), jax.ShapeDtypeStruct((S, LANES), F32),
                   jax.ShapeDtypeStruct((S, LANES), F32), jax.ShapeDtypeStruct((8, LANES), F32)],
        scratch_shapes=[pltpu.VMEM((8, LANES), F32)],
        compiler_params=_cparams(("arbitrary",), 32),
        name="moe_router",
    )(x, w_pad, b_pad)


def _expert_kernel(sb_ref, se_ref, sf_ref, sj_ref, so_ref, sv_ref, tok_ref,
                   x_hbm, wgu_ref, bgu_ref, wdn_ref, bdn_ref, o_ref, wgu_b, wdn_b, acc, stage, xrows, sem):
    s = pl.program_id(0)
    n_steps = pl.num_programs(0)
    nrow = sv_ref[s]
    valid = nrow > 0
    f = sf_ref[s]
    j = sj_ref[s]
    SUB = MOE_SUB
    FT2 = wgu_ref.shape[1]
    RT = ROW_TILES
    UNROLL = 8

    def gather_rows(step):
        base = sb_ref[step] * SUB
        slot = sj_ref[step] % 2

        def issue(i, carry):
            for u in range(UNROLL):
                r = i * UNROLL + u
                t = tok_ref[base + r]
                pltpu.make_async_copy(x_hbm.at[t], stage.at[slot, :, r, :], sem.at[slot]).start()
            return carry

        lax.fori_loop(0, sv_ref[step] // UNROLL, issue, 0)

    @pl.when(s == 0)
    def _():
        gather_rows(0)

    nxt = jnp.minimum(s + 1, n_steps - 1)

    @pl.when((s + 1 < n_steps) & (sv_ref[nxt] > 0) & (sf_ref[nxt] == 0))
    def _():
        gather_rows(nxt)

    @pl.when(valid & (j == 0))
    def _():
        wgu_b[...] = wgu_ref[...].astype(BF16)
        wdn_b[...] = wdn_ref[...].astype(BF16)

    @pl.when(jnp.logical_not(valid))
    def _():
        o_ref[...] = jnp.zeros(o_ref.shape, F32)

    def process(m):
        rows = pl.ds(pl.multiple_of(j * SUB, SUB), m)

        @pl.when(f == 0)
        def _():
            slot = j % 2
            filled = stage.at[slot, :, pl.ds(0, m), :]
            pltpu.make_async_copy(filled, filled, sem.at[slot]).wait()
            for c in range(RT):
                xrows[rows, c * LANES:(c + 1) * LANES] = stage[slot, c, 0:m, :].astype(BF16)

        gu = jnp.dot(xrows[rows, :], wgu_b[...], preferred_element_type=F32) + bgu_ref[...]
        gate = jnp.minimum(gu, SWIGLU_LIMIT)
        act = gate * jax.nn.sigmoid(SWIGLU_ALPHA * gate)
        up1 = jnp.clip(gu, -SWIGLU_LIMIT, SWIGLU_LIMIT) + 1.0
        pr = lax.broadcasted_iota(jnp.int32, (2 * LANES, LANES), 0)
        pc = lax.broadcasted_iota(jnp.int32, (2 * LANES, LANES), 1)
        pick_even = (pr == 2 * pc).astype(F32).astype(BF16)
        hs = []
        for c in range(FT2 // (2 * LANES)):
            parts = []
            for k in range(2):
                cs = slice((2 * c + k) * LANES, (2 * c + k + 1) * LANES)
                parts.append((act[:, cs] * pltpu.roll(up1[:, cs], LANES - 1, 1)).astype(BF16))
            pair = jnp.concatenate(parts, axis=1)
            hs.append(jnp.dot(pair, pick_even, preferred_element_type=F32).astype(BF16))
        h = jnp.concatenate(hs, axis=1)
        y = jnp.dot(h, wdn_b[...], preferred_element_type=F32)

        @pl.when(f == 0)
        def _():
            acc[rows, :] = y + bdn_ref[...]

        @pl.when((f > 0) & (f < MOE_NF - 1))
        def _():
            acc[rows, :] = acc[rows, :] + y

        @pl.when(f == MOE_NF - 1)
        def _():
            yfin = acc[rows, :] + y
            for c in range(RT):
                o_ref[pl.ds(c, m, stride=RT), :] = yfin[:, c * LANES:(c + 1) * LANES]
            if m < SUB:
                o_ref[pl.ds(m * RT, (SUB - m) * RT), :] = jnp.zeros(((SUB - m) * RT, LANES), F32)

    for m in range(MOE_ROWQ, SUB + 1, MOE_ROWQ):
        pl.when(nrow == m)(functools.partial(process, m))


def _experts(meta, tok, x_rows, w_gu, b_gu3, w_dn, b_dn3, layer):
    sb, se, sf, sj, so, sv = meta
    NS = sb.shape[0]
    R = tok.shape[0]
    D = D_MODEL
    RT = ROW_TILES

    def wmap(fn):
        return lambda s, sb, se, sf, sj, so, sv, tok: fn(s, se, sf, so)

    grid_spec = pltpu.PrefetchScalarGridSpec(
        num_scalar_prefetch=7,
        grid=(NS,),
        in_specs=[
            pl.BlockSpec(memory_space=pl.ANY),
            pl.BlockSpec((None, None, D, 2 * MOE_FT), wmap(lambda s, se, sf, so: (layer, se[s], 0, sf[s]))),
            pl.BlockSpec((None, None, 1, 2 * MOE_FT), wmap(lambda s, se, sf, so: (layer, se[s], 0, sf[s]))),
            pl.BlockSpec((None, None, MOE_FT, D), wmap(lambda s, se, sf, so: (layer, se[s], sf[s], 0))),
            pl.BlockSpec((None, None, 1, D), wmap(lambda s, se, sf, so: (layer, se[s], 0, 0))),
        ],
        out_specs=pl.BlockSpec((MOE_SUB * RT, LANES), wmap(lambda s, se, sf, so: (so[s], 0))),
        scratch_shapes=[pltpu.VMEM((D, 2 * MOE_FT), BF16), pltpu.VMEM((MOE_FT, D), BF16),
                        pltpu.VMEM((MOE_J * MOE_SUB, D), F32),
                        pltpu.VMEM((2, RT, MOE_SUB, LANES), F32),
                        pltpu.VMEM((MOE_J * MOE_SUB, D), BF16),
                        pltpu.SemaphoreType.DMA((2,))],
    )
    return pl.pallas_call(
        _expert_kernel,
        grid_spec=grid_spec,
        out_shape=jax.ShapeDtypeStruct((R * RT, LANES), F32),
        compiler_params=_cparams(("arbitrary",), 60),
        name="moe_experts",
    )(sb, se, sf, sj, so, sv, tok, x_rows, w_gu, b_gu3, w_dn, b_dn3)


def _plan_kernel(cnt_ref, sb, se, sf, sj, so, sv, sub_start, src_start, n_used):
    n_sub_max = src_start.shape[0]
    n_steps_max = sb.shape[0]

    def expert(e, carry):
        b0, g0, s = carry
        c = cnt_ref[e]
        ns = (c + MOE_SUB - 1) // MOE_SUB
        sub_start[e] = b0

        def sub_tile(lb, _):
            src_start[b0 + lb] = g0 + lb * MOE_SUB
            n_used[b0 + lb] = jnp.minimum(MOE_SUB, c - lb * MOE_SUB)
            return 0

        lax.fori_loop(0, ns, sub_tile, 0)

        def chunk(ci, s):
            cb0 = b0 + ci * MOE_J
            nj = jnp.minimum(MOE_J, ns - ci * MOE_J)

            def ff_tile(f, s):
                def slot(j, s):
                    rem = c - (ci * MOE_J + j) * MOE_SUB
                    sb[s] = cb0 + j
                    se[s] = e
                    sf[s] = f
                    sj[s] = j
                    so[s] = jnp.where(f == MOE_NF - 1, cb0 + j, cb0)
                    sv[s] = jnp.clip((rem + MOE_ROWQ - 1) // MOE_ROWQ * MOE_ROWQ, MOE_ROWQ, MOE_SUB)
                    return s + 1

                return lax.fori_loop(0, nj, slot, s)

            return lax.fori_loop(0, MOE_NF, ff_tile, s)

        s = lax.fori_loop(0, (ns + MOE_J - 1) // MOE_J, chunk, s)
        return b0 + ns, g0 + c, s

    b_end, _, s_end = lax.fori_loop(0, N_EXPERTS, expert, (jnp.int32(0), jnp.int32(0), jnp.int32(0)))

    def unused_sub_tile(b, _):
        src_start[b] = 0
        n_used[b] = 0
        return 0

    lax.fori_loop(b_end, n_sub_max, unused_sub_tile, 0)

    last = jnp.maximum(s_end - 1, 0)

    def unused_step(s, _):
        sb[s] = sb[last]
        se[s] = se[last]
        sf[s] = sf[last]
        sj[s] = sj[last]
        so[s] = jnp.minimum(b_end + (s - s_end), n_sub_max - 1)
        sv[s] = 0
        return 0

    lax.fori_loop(s_end, n_steps_max, unused_step, 0)


def _moe_tables(e_flat, r_flat, cnt, n_sub_max):
    NS = MOE_NF * n_sub_max
    smem = pl.BlockSpec(memory_space=pltpu.SMEM)
    i32 = lambda n: jax.ShapeDtypeStruct((n,), jnp.int32)
    *meta, sub_start, src_start, n_used = pl.pallas_call(
        _plan_kernel,
        in_specs=[smem],
        out_specs=[smem] * 9,
        out_shape=[i32(NS)] * 6 + [i32(N_EXPERTS), i32(n_sub_max), i32(n_sub_max)],
        name="moe_plan",
    )(cnt)
    dest = sub_start[e_flat] * MOE_SUB + r_flat

    n_asg = e_flat.shape[0]
    _, tok_sorted = lax.sort_key_val(dest, jnp.arange(n_asg, dtype=jnp.int32) // TOP_K)
    lane = jnp.arange(MOE_SUB, dtype=jnp.int32)[None, :]
    src = jnp.minimum(src_start[:, None] + lane, n_asg - 1)
    tok = jnp.where(lane < n_used[:, None], tok_sorted[src], 0).reshape(-1)
    return dest, tok, tuple(meta)


def _combine_kernel(dest_ref, y_hbm, g4_ref, x_ref, g_ref, b_ref, o_ref, ob_ref, stage, ytile, ysum, sem):
    i = pl.program_id(0)
    n = pl.num_programs(0)
    tm = x_ref.shape[0]
    RT = ROW_TILES
    n_rows = tm * TOP_K

    def start_row(step, to_slot, r):
        d = dest_ref[step * n_rows + r]
        pltpu.make_async_copy(y_hbm.at[pl.ds(pl.multiple_of(d * RT, RT), RT), :],
                              stage.at[to_slot, pl.ds(pl.multiple_of(r * RT, RT), RT), :],
                              sem.at[to_slot]).start()

    def gather_rows(step, to_slot):
        def issue(r, carry):
            start_row(step, to_slot, r)
            return carry

        lax.fori_loop(0, n_rows, issue, 0, unroll=8)

    @pl.when(i == 0)
    def _():
        gather_rows(0, 0)

    @pl.when(i + 1 < n)
    def _():
        gather_rows(i + 1, (i + 1) % 2)

    slot = i % 2
    pltpu.make_async_copy(y_hbm.at[pl.ds(0, n_rows * RT), :], stage.at[slot], sem.at[slot]).wait()

    def token_sum(t, carry):
        r0 = pl.multiple_of(t * (TOP_K * RT), TOP_K * RT)
        y = g4_ref[TOP_K * t] * stage[slot, pl.ds(r0, RT), :]
        for k in range(1, TOP_K):
            y = y + g4_ref[TOP_K * t + k] * stage[slot, pl.ds(r0 + k * RT, RT), :]
        ytile[pl.ds(pl.multiple_of(t * RT, RT), RT), :] = y
        return carry

    lax.fori_loop(0, tm, token_sum, 0, unroll=4)
    for c in range(RT):
        ysum[:, c * LANES:(c + 1) * LANES] = ytile[pl.ds(c, tm, stride=RT), :]
    out = _layer_norm_rows(DN_ALPHA * x_ref[...] + ysum[...], g_ref[...], b_ref[...])
    o_ref[...] = out
    ob_ref[...] = out.astype(BF16)


def _combine_ln(dest_flat, y_rows, g4, x, g, b, tm=128):
    S, D = x.shape
    vec = pl.BlockSpec((1, D), lambda i, d: (0, 0))
    grid_spec = pltpu.PrefetchScalarGridSpec(
        num_scalar_prefetch=1,
        grid=(S // tm,),
        in_specs=[pl.BlockSpec(memory_space=pl.ANY),
                  pl.BlockSpec((tm * TOP_K,), lambda i, d: (i,), memory_space=pltpu.SMEM),
                  pl.BlockSpec((tm, D), lambda i, d: (i, 0)), vec, vec],
        out_specs=[pl.BlockSpec((tm, D), lambda i, d: (i, 0)), pl.BlockSpec((tm, D), lambda i, d: (i, 0))],
        scratch_shapes=[pltpu.VMEM((2, tm * TOP_K * ROW_TILES, LANES), F32),
                        pltpu.VMEM((tm * ROW_TILES, LANES), F32),
                        pltpu.VMEM((tm, D), F32),
                        pltpu.SemaphoreType.DMA((2,))],
    )
    return pl.pallas_call(
        _combine_kernel,
        grid_spec=grid_spec,
        out_shape=[jax.ShapeDtypeStruct((S, D), F32), jax.ShapeDtypeStruct((S, D), BF16)],
        compiler_params=_cparams(("arbitrary",), 48),
        name="moe_combine_ln",
    )(dest_flat, y_rows, g4, x, g.reshape(1, D), b.reshape(1, D))


def _nsa_group(u_bf, u_win, gate_logits, cmp_pos, cmp_w1, cmp_w2, rel_bias, g_nsa):
    S = u_bf.shape[0]
    G = NSA_KV_HEADS
    kvc = u_bf[:, NSA_WIDTH:NSA_WIDTH + 2 * KV_WIDTH]
    kv_rows = kvc.reshape(S, 2 * G, HEAD_DIM).transpose(1, 0, 2).reshape(2 * G, S // CMP_STRIDE,
                                                                          CMP_STRIDE * HEAD_DIM)
    kcv = _compress(kv_rows, cmp_pos, cmp_w1, cmp_w2)
    o_c, sel = _cmp_attention(rel_bias, u_bf, kcv)
    kv_pad = jnp.pad(jnp.concatenate([u_bf[:, NSA_WIDTH + 2 * KV_WIDTH:], u_win], axis=1), ((KV_FRONT, 0), (0, 0)))
    return _sel_attention(rel_bias, u_bf, kv_pad, sel, o_c, gate_logits, g_nsa)


def _moe(x, x_rows, w_router, b_router, w_gu, b_gu, w_dn, b_dn, ln_g, ln_b, layer):
    S, D = x.shape
    E = N_EXPERTS
    w_pad = jnp.pad(w_router, ((0, 0), (0, LANES - E)))
    b_pad = jnp.pad(b_router, (0, LANES - E), constant_values=NEG_INF).reshape(1, LANES)
    e_out, g_out, r_out, cnt_out = _router(x, w_pad, b_pad)
    e_flat = e_out[:, :TOP_K].reshape(-1)
    r_flat = r_out[:, :TOP_K].reshape(-1).astype(jnp.int32)
    cnt = cnt_out[0, :E].astype(jnp.int32)
    n_sub_max = (S * TOP_K) // MOE_SUB + E
    flat, tok, meta = _moe_tables(e_flat, r_flat, cnt, n_sub_max)
    y_rows = _experts(meta, tok, x_rows.reshape(S, ROW_TILES, LANES), w_gu, b_gu.reshape(DEPTH, E, 1, 2 * D_FF), w_dn,
                      b_dn.reshape(DEPTH, E, 1, D), layer)
    return _combine_ln(flat, y_rows, g_out[:, :TOP_K].reshape(-1), x, ln_g, ln_b)


def kernel(x, w_in, conv_w, conv_b, lru_wa, lru_ba, lru_wi, lru_bi, lru_lambda, cmp_pos, cmp_w1, cmp_w2,
           rel_bias, g_lru, g_nsa, w_out, ln1_g, ln1_b, w_router, b_router, w_gate_up, b_gate_up, w_down,
           b_down, ln2_g, ln2_b):
    B, S, D = x.shape
    assert B == 1
    x = x.reshape(S, D)
    xb = x.astype(BF16)
    n_gate = 3 * NSA_HEADS
    for l in range(DEPTH):
        u_f32 = _project(xb, w_in, l, 0, 2, 1024, F32)
        u_bf = _project(xb, w_in, l, 2, 2, 1024, BF16)
        u_win = _project(xb, w_in, l, 8, 1, 512, BF16)
        w_gate = jnp.pad(w_in[l, :, MAIN_COLS:], ((0, 0), (0, LANES - n_gate)))[None]
        gate_logits = _project(xb, w_gate, 0, 0, 1, LANES, F32)
        y_lru = _lru_group(u_f32, conv_w[l], conv_b[l], lru_wa[l], lru_ba[l], lru_wi[l], lru_bi[l],
                           lru_lambda[l], g_lru[l])
        y_nsa = _nsa_group(u_bf, u_win, gate_logits, cmp_pos[l], cmp_w1[l], cmp_w2[l], rel_bias, g_nsa[l])
        x, x_rows = _out_proj_ln(y_lru, y_nsa, w_out[l].astype(BF16), x, ln1_g[l], ln1_b[l])
        x, xb = _moe(x, x_rows, w_router[l], b_router[l], w_gate_up, b_gate_up, w_down, b_down,
                     ln2_g[l], ln2_b[l], l)
    return x.reshape(B, S, D)
```

```python
import functools
import math

import numpy as np
import jax
import jax.numpy as jnp
from jax import lax
from jax.experimental import pallas as pl
from jax.experimental.pallas import tpu as pltpu

D_MODEL = 2048
DEPTH = 2
LRU_WIDTH = 1024
LRU_BLOCKS = 8
LRU_BLOCK_W = LRU_WIDTH // LRU_BLOCKS
CONV_WIDTH = 4
LRU_C = 8.0
NSA_HEADS = 8
NSA_KV_HEADS = 2
HEADS_PER_GROUP = NSA_HEADS // NSA_KV_HEADS
HEAD_DIM = 128
NSA_WIDTH = NSA_HEADS * HEAD_DIM
KV_WIDTH = NSA_KV_HEADS * HEAD_DIM
CMP_BLOCK = 32
CMP_STRIDE = 16
SEL_BLOCK = 64
N_SEL = 16
WINDOW = 512
Q_BLOCK = 128
N_BUCKETS = 32
MAX_DISTANCE = 128
N_EXPERTS = 32
TOP_K = 4
D_FF = 2048
SWIGLU_LIMIT = 7.0
SWIGLU_ALPHA = 1.702
DN_ALPHA = (2 * DEPTH) ** 0.25
NORM_EPS = 1e-5
NEG_INF = -1e30
FORCED_SCORE = 1e9
ATTN_SCALE = HEAD_DIM ** -0.5

LOG2E = math.log2(math.e)
N_FORCED = 3
FAR_CHUNK = 1024
KV_FRONT = FAR_CHUNK
PAD_BLK = KV_FRONT // SEL_BLOCK
M_INIT = -1e9

LANES = 128
MIB = 1024 * 1024
MAIN_COLS = 2 * LRU_WIDTH + NSA_WIDTH + 6 * KV_WIDTH
BF16 = jnp.bfloat16
F32 = jnp.float32

MOE_SUB = 256
MOE_ROWQ = 64
MOE_J = 5
MOE_FT = 512
MOE_NF = D_FF // MOE_FT
ROW_TILES = D_MODEL // LANES


def _bucket_thresholds():
    max_exact = N_BUCKETS // 2
    out = []
    for b in range(1, N_BUCKETS):
        if b <= max_exact:
            out.append(b)
            continue
        d = max_exact
        while True:
            v = math.log(d / max_exact) / math.log(MAX_DISTANCE / max_exact) * (N_BUCKETS - max_exact)
            assert abs(v - round(v)) > 1e-3 or d == max_exact
            if min(max_exact + int(v), N_BUCKETS - 1) >= b:
                break
            d += 1
        out.append(d)
    return tuple(out)


BUCKET_TH = _bucket_thresholds()


def _cparams(semantics, vmem_mib):
    return pltpu.CompilerParams(dimension_semantics=semantics, vmem_limit_bytes=vmem_mib * MIB)


def _nt_dot(a, b):
    return lax.dot_general(a, b, (((1,), (1,)), ((), ())), preferred_element_type=F32)


def _bias_of_dist(rb_ref, head, d):
    b = jnp.full(d.shape, rb_ref[0, head], F32)
    for k in range(1, N_BUCKETS):
        b = jnp.where(d >= BUCKET_TH[k - 1], rb_ref[k, head], b)
    return b


def _mm_kernel(x_ref, w_ref, o_ref, wb_ref):
    @pl.when(pl.program_id(1) == 0)
    def _():
        wb_ref[...] = w_ref[...].astype(BF16)

    o_ref[...] = jnp.dot(x_ref[...], wb_ref[...], preferred_element_type=F32).astype(o_ref.dtype)


def _project(xb, w3, layer, col_blk0, n_blks, tn, out_dtype, tm=1024):
    S, D = xb.shape
    return pl.pallas_call(
        _mm_kernel,
        grid=(n_blks, S // tm),
        in_specs=[pl.BlockSpec((tm, D), lambda n, m: (m, 0)),
                  pl.BlockSpec((None, D, tn), lambda n, m: (layer, 0, col_blk0 + n))],
        out_specs=pl.BlockSpec((tm, tn), lambda n, m: (m, n)),
        out_shape=jax.ShapeDtypeStruct((S, n_blks * tn), out_dtype),
        scratch_shapes=[pltpu.VMEM((D, tn), BF16)],
        compiler_params=_cparams(("arbitrary", "arbitrary"), 48),
        name="in_proj",
    )(xb, w3)


def _gelu_tanh(x):
    return 0.5 * x * (1.0 + jnp.tanh(math.sqrt(2.0 / math.pi) * (x + 0.044715 * (x * x * x))))


def _lru_kernel(xr_ref, xg_ref, cw_ref, cb_ref, wa_ref, ba_ref, wi_ref, bi_ref, lam_ref, g_ref,
                o_ref, xbuf, hc):
    i = pl.program_id(0)
    T = xr_ref.shape[0]
    W = xr_ref.shape[1]
    HALO = 8

    @pl.when(i == 0)
    def _():
        xbuf[0:HALO, :] = jnp.zeros((HALO, W), F32)
        hc[...] = jnp.zeros(hc.shape, F32)

    @pl.when(i > 0)
    def _():
        xbuf[0:HALO, :] = xbuf[T:T + HALO, :]

    xbuf[HALO:HALO + T, :] = xr_ref[...]
    cw = cw_ref[...]
    xc = cb_ref[...] + xbuf[HALO - 3:HALO - 3 + T, :] * cw[0:1]
    for k in range(1, CONV_WIDTH):
        xc = xc + xbuf[HALO - 3 + k:HALO - 3 + k + T, :] * cw[k:k + 1]

    xcb = xc.astype(BF16)
    ra, ia = [], []
    for n in range(LRU_BLOCKS):
        blk = xcb[:, n * LRU_BLOCK_W:(n + 1) * LRU_BLOCK_W]
        ra.append(jnp.dot(blk, wa_ref[n].astype(BF16), preferred_element_type=F32))
        ia.append(jnp.dot(blk, wi_ref[n].astype(BF16), preferred_element_type=F32))
    r = jax.nn.sigmoid(jnp.concatenate(ra, axis=1) + ba_ref[...])
    ig = jax.nn.sigmoid(jnp.concatenate(ia, axis=1) + bi_ref[...])

    z = -lam_ref[...]
    softplus = jnp.maximum(z, 0.0) + jnp.log(1.0 + jnp.exp(-jnp.abs(z)))
    log_a = -LRU_C * r * softplus
    a = jnp.exp(log_a)
    u = jnp.sqrt(1.0 - a * a) * (ig * xc)

    SUBL = 8
    sub = lax.broadcasted_iota(jnp.int32, (T, W), 0) % SUBL
    A, U = a, u
    d = 1
    while d < SUBL:
        keep = sub >= d
        a_sh = jnp.where(keep, pltpu.roll(A, d, 0), 1.0)
        u_sh = jnp.where(keep, pltpu.roll(U, d, 0), 0.0)
        U = A * u_sh + U
        A = A * a_sh
        d *= 2
    state = hc[...]
    groups = []
    for k in range(T // SUBL):
        hg = U[k * SUBL:(k + 1) * SUBL] + A[k * SUBL:(k + 1) * SUBL] * state
        state = hg[SUBL - 1:SUBL]
        groups.append(hg)
    h = jnp.concatenate(groups, axis=0)
    hc[...] = state

    y = h * _gelu_tanh(xg_ref[...])
    ms = jnp.mean(y * y, axis=-1, keepdims=True)
    o_ref[...] = (y * lax.rsqrt(ms + NORM_EPS) * g_ref[...]).astype(o_ref.dtype)


def _lru_group(u_f32, conv_w, conv_b, wa, ba, wi, bi, lam, g_lru, T=256):
    S = u_f32.shape[0]
    W = LRU_WIDTH
    row = lambda v: v.reshape(1, W)
    vec = pl.BlockSpec((1, W), lambda i: (0, 0))
    mat = pl.BlockSpec((LRU_BLOCKS, LRU_BLOCK_W, LRU_BLOCK_W), lambda i: (0, 0, 0))
    return pl.pallas_call(
        _lru_kernel,
        grid=(S // T,),
        in_specs=[pl.BlockSpec((T, W), lambda i: (i, 0)),
                  pl.BlockSpec((T, W), lambda i: (i, 1)),
                  pl.BlockSpec((CONV_WIDTH, W), lambda i: (0, 0)),
                  vec, mat, vec, mat, vec, vec, vec],
        out_specs=pl.BlockSpec((T, W), lambda i: (i, 0)),
        out_shape=jax.ShapeDtypeStruct((S, W), BF16),
        scratch_shapes=[pltpu.VMEM((T + 8, W), F32), pltpu.VMEM((1, W), F32)],
        compiler_params=_cparams(("arbitrary",), 48),
        name="rg_lru",
    )(u_f32, u_f32, conv_w, row(conv_b), wa, row(ba), wi, row(bi), row(lam), row(g_lru))


def _compress_kernel(x_ref, pos_ref, w1_ref, w2_ref, o_ref):
    NJ = x_ref.shape[0]
    half = (CMP_BLOCK // 2) * HEAD_DIM
    x = x_ref[...].astype(F32)
    a = (x + pos_ref[0:1, :]).astype(BF16)
    b = (x + pos_ref[1:2, :]).astype(BF16)
    y1 = jnp.dot(a, w1_ref[0:half, :].astype(BF16), preferred_element_type=F32)
    y2 = jnp.dot(b, w1_ref[half:2 * half, :].astype(BF16), preferred_element_type=F32)
    hmid = y1 + pltpu.roll(y2, NJ - 1, 0)
    o_ref[...] = jnp.dot(_gelu_tanh(hmid).astype(BF16), w2_ref[...].astype(BF16),
                         preferred_element_type=F32).astype(o_ref.dtype)


def _compress(kv_rows, pos, w1, w2):
    _, NJ, RW = kv_rows.shape
    return pl.pallas_call(
        _compress_kernel,
        grid=(4,),
        in_specs=[pl.BlockSpec((None, NJ, RW), lambda a: (a, 0, 0)),
                  pl.BlockSpec((None, 2, RW), lambda a: (a // 2, 0, 0)),
                  pl.BlockSpec((None, CMP_BLOCK * HEAD_DIM, HEAD_DIM), lambda a: (a // 2, 0, 0)),
                  pl.BlockSpec((None, HEAD_DIM, HEAD_DIM), lambda a: (a // 2, 0, 0))],
        out_specs=pl.BlockSpec((None, NJ, HEAD_DIM), lambda a: (a, 0, 0)),
        out_shape=jax.ShapeDtypeStruct((4, NJ, HEAD_DIM), BF16),
        compiler_params=_cparams(("arbitrary",), 32),
        name="kv_compress",
    )(kv_rows, pos.reshape(2, 2, RW), w1, w2)


def _cmp_attn_kernel(rb_ref, q_ref, kcv_ref, oc_ref, sel_ref, tcat):
    qb = pl.program_id(0)
    t0 = qb * Q_BLOCK
    NJ = kcv_ref.shape[1]
    NB = NJ * CMP_STRIDE // SEL_BLOCK
    NBP = sel_ref.shape[2]
    n_sel = min(N_SEL, NB)
    LOCAL = 2 * Q_BLOCK // CMP_STRIDE

    @pl.when(qb == 0)
    def _():
        i = lax.broadcasted_iota(jnp.int32, (Q_BLOCK, LANES), 0)
        j = lax.broadcasted_iota(jnp.int32, (Q_BLOCK, LANES), 1)
        d = jnp.maximum(i - CMP_STRIDE * j + (Q_BLOCK - CMP_STRIDE), 0)
        for h in range(NSA_HEADS):
            corr = jnp.where(j < LOCAL, (_bias_of_dist(rb_ref, h, d) - rb_ref[N_BUCKETS - 1, h]) * LOG2E, 0.0)
            hi = corr.astype(BF16)
            tcat[h, :, 0:LANES] = hi
            tcat[h, :, LANES:2 * LANES] = (corr - hi.astype(F32)).astype(BF16)

    def tile(W):
        row = lax.broadcasted_iota(jnp.int32, (Q_BLOCK, W), 0)
        col = lax.broadcasted_iota(jnp.int32, (Q_BLOCK, W), 1)
        mask_add = jnp.where(CMP_STRIDE * col + (CMP_BLOCK - 1) <= t0 + row, 0.0, NEG_INF)
        jj = lax.broadcasted_iota(jnp.int32, (2 * LANES, W), 0) % LANES
        cc = lax.broadcasted_iota(jnp.int32, (2 * LANES, W), 1)
        place = (cc == (Q_BLOCK // CMP_STRIDE) * (qb - 1) + jj).astype(F32).astype(BF16)
        bj = lax.broadcasted_iota(jnp.int32, (NB, W), 0)
        bc = lax.broadcasted_iota(jnp.int32, (NB, W), 1)
        ratio = SEL_BLOCK // CMP_STRIDE
        overlap_t = ((bc <= ratio * bj + ratio - 1) & (bc >= ratio * bj - 1)).astype(F32).astype(BF16)

        sj = lax.broadcasted_iota(jnp.int32, (NB, Q_BLOCK), 0)
        st = t0 + lax.broadcasted_iota(jnp.int32, (NB, Q_BLOCK), 1)
        cur = st // SEL_BLOCK
        blk_ok = sj <= cur
        forced = (sj == 0) | (sj == cur) | (sj == cur - 1)

        scores = []
        for g in range(NSA_KV_HEADS):
            kc = kcv_ref[g, 0:W, :]
            vc = kcv_ref[NSA_KV_HEADS + g, 0:W, :]
            psum = jnp.zeros((Q_BLOCK, W), F32)
            for h in range(HEADS_PER_GROUP):
                hh = g * HEADS_PER_GROUP + h
                qh = (q_ref[:, hh * HEAD_DIM:(hh + 1) * HEAD_DIM].astype(F32) * (ATTN_SCALE * LOG2E)).astype(BF16)
                z = _nt_dot(qh, kc) + jnp.dot(tcat[hh], place, preferred_element_type=F32) + mask_add
                m = jnp.maximum(jnp.max(z, axis=-1, keepdims=True), M_INIT)
                e = jnp.exp2(z - m)
                l = jnp.sum(e, axis=-1, keepdims=True)
                p = e * (1.0 / jnp.where(l > 0.0, l, 1.0))
                oc_ref[:, hh * HEAD_DIM:(hh + 1) * HEAD_DIM] = jnp.dot(
                    p.astype(BF16), vc, preferred_element_type=F32)
                psum = psum + p
            p_hi = psum.astype(BF16)
            p_lo = (psum - p_hi.astype(F32)).astype(BF16)
            imp_t = _nt_dot(overlap_t, p_hi) + _nt_dot(overlap_t, p_lo)
            scores.append(jnp.where(blk_ok, jnp.where(forced, FORCED_SCORE, imp_t), NEG_INF))

        score = jnp.concatenate(scores, axis=1)
        sjf = jnp.concatenate([sj.astype(F32)] * NSA_KV_HEADS, axis=1)
        chosen = jnp.concatenate([jnp.where(forced, 1.0, 0.0)] * NSA_KV_HEADS, axis=1) > 0.5
        score = jnp.where(chosen, -3e38, score)
        for _ in range(max(n_sel - N_FORCED, 0)):
            mx = jnp.max(score, axis=0, keepdims=True)
            first = jnp.min(jnp.where(score == mx, sjf, float(NB)), axis=0, keepdims=True)
            pick = sjf == first
            chosen = chosen | pick
            score = jnp.where(pick, -3e38, score)
        picked = jnp.where(chosen, 1.0, 0.0)
        for g in range(NSA_KV_HEADS):
            pen_t = jnp.where((picked[:, g * Q_BLOCK:(g + 1) * Q_BLOCK] > 0.5) & blk_ok, 0.0, NEG_INF)
            padded = jnp.concatenate([jnp.full((PAD_BLK, Q_BLOCK), NEG_INF, F32), pen_t,
                                      jnp.zeros((NBP - PAD_BLK - NB, Q_BLOCK), F32)], axis=0)
            pen = jnp.concatenate([padded[r:r + LANES].T for r in range(0, NBP, LANES)], axis=1)
            sel_ref[g] = pen.astype(sel_ref.dtype)

    step = min(LANES, NJ)
    need = (Q_BLOCK // CMP_STRIDE) * qb + (Q_BLOCK - CMP_BLOCK) // CMP_STRIDE + 1
    width = jnp.clip((need + step - 1) // step * step, min(2 * step, NJ), NJ)
    for W in range(min(2 * step, NJ), NJ + 1, step):
        pl.when(width == W)(functools.partial(tile, W))


def _cmp_attention(rel_bias, u_bf, kcv):
    S = u_bf.shape[0]
    NJ = kcv.shape[1]
    NB = -(-(S // SEL_BLOCK + PAD_BLK) // LANES) * LANES
    return pl.pallas_call(
        _cmp_attn_kernel,
        grid=(S // Q_BLOCK,),
        in_specs=[pl.BlockSpec(memory_space=pltpu.SMEM),
                  pl.BlockSpec((Q_BLOCK, NSA_WIDTH), lambda i: (i, 0)),
                  pl.BlockSpec((4, NJ, HEAD_DIM), lambda i: (0, 0, 0))],
        out_specs=[pl.BlockSpec((Q_BLOCK, NSA_WIDTH), lambda i: (i, 0)),
                   pl.BlockSpec((NSA_KV_HEADS, Q_BLOCK, NB), lambda i: (0, i, 0))],
        out_shape=[jax.ShapeDtypeStruct((S, NSA_WIDTH), F32),
                   jax.ShapeDtypeStruct((NSA_KV_HEADS, S, NB), BF16)],
        scratch_shapes=[pltpu.VMEM((NSA_HEADS, Q_BLOCK, 2 * LANES), BF16)],
        compiler_params=_cparams(("arbitrary",), 32),
        name="nsa_compressed",
    )(rel_bias, u_bf, kcv)


def _sel_attn_kernel(rb_ref, q_ref, ks_ref, vs_ref, kw_ref, vw_ref, sel_ref, oc_ref, gl_ref, g_ref,
                     o_ref, tbl_s, tbl_w, yacc):
    qb = pl.program_id(0)
    Q = Q_BLOCK
    HPG = HEADS_PER_GROUP
    TW = WINDOW + Q
    FAR = FAR_CHUNK
    NBP = sel_ref.shape[2]
    t0 = qb * Q

    @pl.when(qb == 0)
    def _():
        i = lax.broadcasted_iota(jnp.int32, (Q, TW), 0)
        c = lax.broadcasted_iota(jnp.int32, (Q, TW), 1)
        d = i - c + WINDOW
        dd = jnp.maximum(d, 0)
        for h in range(NSA_HEADS):
            corr = (_bias_of_dist(rb_ref, h, dd) - rb_ref[N_BUCKETS - 1, h]) * LOG2E
            tbl_s[h] = jnp.where(d < 0, NEG_INF, corr)
            tbl_w[h] = jnp.where((d < 0) | (d >= WINDOW), NEG_INF, corr)

    gates = jax.nn.sigmoid(gl_ref[...])
    col_w = lax.broadcasted_iota(jnp.int32, (1, TW), 1)
    before_start = jnp.where(col_w < WINDOW - t0, NEG_INF, 0.0)
    p_tail = pl.multiple_of(t0 + (KV_FRONT - WINDOW), Q)

    G = NSA_KV_HEADS
    q4 = [jnp.concatenate(
        [(q_ref[:, (g * HPG + h) * HEAD_DIM:(g * HPG + h + 1) * HEAD_DIM].astype(F32)
          * (ATTN_SCALE * LOG2E)).astype(BF16) for h in range(HPG)], axis=0) for g in range(G)]

    def attend(g, k_ref, v_ref, p0, width, add, carry):
        m, l, acc = carry
        cols = slice(g * HEAD_DIM, (g + 1) * HEAD_DIM)
        kk = k_ref[pl.ds(p0, width), cols]
        vv = v_ref[pl.ds(p0, width), cols]
        s = _nt_dot(q4[g], kk)
        z = (s.reshape(HPG, Q, width) + add).reshape(HPG * Q, width)
        m_new = jnp.maximum(m, jnp.max(z, axis=-1, keepdims=True))
        alpha = jnp.exp2(m - m_new)
        e = jnp.exp2(z - m_new)
        l = alpha * l + jnp.sum(e, axis=-1, keepdims=True)
        acc = alpha * acc + jnp.dot(e.astype(BF16), vv, preferred_element_type=F32)
        return m_new, l, acc

    def block_mask(g, p0, width):
        bp = lax.broadcasted_iota(jnp.int32, (NBP, width), 0)
        kb = lax.broadcasted_iota(jnp.int32, (NBP, width), 1) // SEL_BLOCK
        expand = (bp - p0 // SEL_BLOCK == kb).astype(F32).astype(BF16)
        return jnp.dot(sel_ref[g], expand, preferred_element_type=F32)[None]

    init = (jnp.full((HPG * Q, 1), M_INIT, F32), jnp.zeros((HPG * Q, 1), F32),
            jnp.zeros((HPG * Q, HEAD_DIM), F32))

    def far_body(i, carry):
        p0 = pl.multiple_of(p_tail - FAR * (i + 1), Q)
        return tuple(attend(g, ks_ref, vs_ref, p0, FAR, block_mask(g, p0, FAR), carry[g]) for g in range(G))

    n_far = (jnp.maximum(t0 - WINDOW, 0) + FAR - 1) // FAR
    far = lax.fori_loop(0, n_far, far_body, (init,) * G)

    for g in range(G):
        heads = slice(g * HPG, (g + 1) * HPG)
        m_s, l_s, acc_s = attend(g, ks_ref, vs_ref, p_tail, TW, tbl_s[heads] + block_mask(g, p_tail, TW), far[g])
        o_s = acc_s / jnp.where(l_s > 0.0, l_s, 1.0)

        m_w, l_w, acc_w = attend(g, kw_ref, vw_ref, p_tail, TW, tbl_w[heads] + before_start[None], init)
        o_w = acc_w / jnp.where(l_w > 0.0, l_w, 1.0)

        for h in range(HPG):
            hh = g * HPG + h
            hc = slice(hh * HEAD_DIM, (hh + 1) * HEAD_DIM)
            rows = slice(h * Q, (h + 1) * Q)
            yacc[:, hc] = (gates[:, 3 * hh:3 * hh + 1] * oc_ref[:, hc]
                           + gates[:, 3 * hh + 1:3 * hh + 2] * o_s[rows]
                           + gates[:, 3 * hh + 2:3 * hh + 3] * o_w[rows])

    y = yacc[...]
    ms = jnp.mean(y * y, axis=-1, keepdims=True)
    o_ref[...] = (y * lax.rsqrt(ms + NORM_EPS) * g_ref[...]).astype(o_ref.dtype)


def _sel_attention(rel_bias, u_bf, kv_pad, sel, o_c, gate_logits, g_nsa):
    S = u_bf.shape[0]
    NB = sel.shape[2]

    def kv_spec(j):
        return pl.BlockSpec((KV_FRONT + S, KV_WIDTH), lambda i: (0, j))

    return pl.pallas_call(
        _sel_attn_kernel,
        grid=(S // Q_BLOCK,),
        in_specs=[pl.BlockSpec(memory_space=pltpu.SMEM),
                  pl.BlockSpec((Q_BLOCK, NSA_WIDTH), lambda i: (i, 0)),
                  kv_spec(0), kv_spec(1), kv_spec(2), kv_spec(3),
                  pl.BlockSpec((NSA_KV_HEADS, Q_BLOCK, NB), lambda i: (0, i, 0)),
                  pl.BlockSpec((Q_BLOCK, NSA_WIDTH), lambda i: (i, 0)),
                  pl.BlockSpec((Q_BLOCK, LANES), lambda i: (i, 0)),
                  pl.BlockSpec((1, NSA_WIDTH), lambda i: (0, 0))],
        out_specs=pl.BlockSpec((Q_BLOCK, NSA_WIDTH), lambda i: (i, 0)),
        out_shape=jax.ShapeDtypeStruct((S, NSA_WIDTH), BF16),
        scratch_shapes=[pltpu.VMEM((NSA_HEADS, Q_BLOCK, WINDOW + Q_BLOCK), F32),
                        pltpu.VMEM((NSA_HEADS, Q_BLOCK, WINDOW + Q_BLOCK), F32),
                        pltpu.VMEM((Q_BLOCK, NSA_WIDTH), F32)],
        compiler_params=_cparams(("arbitrary",), 60),
        name="nsa_selected_window",
    )(rel_bias, u_bf, kv_pad, kv_pad, kv_pad, kv_pad, sel, o_c, gate_logits, g_nsa.reshape(1, NSA_WIDTH))


def _layer_norm_rows(z, g, b):
    mu = jnp.mean(z, axis=-1, keepdims=True)
    zc = z - mu
    var = jnp.mean(zc * zc, axis=-1, keepdims=True)
    return zc * lax.rsqrt(var + NORM_EPS) * g + b


def _outproj_kernel(yl_ref, yn_ref, w_ref, x_ref, g_ref, b_ref, o_ref, or_ref):
    tm = x_ref.shape[0]
    acc = jnp.dot(yl_ref[...], w_ref[0:LRU_WIDTH, :], preferred_element_type=F32)
    acc = acc + jnp.dot(yn_ref[...], w_ref[LRU_WIDTH:LRU_WIDTH + NSA_WIDTH, :], preferred_element_type=F32)
    y = _layer_norm_rows(DN_ALPHA * x_ref[...] + acc, g_ref[...], b_ref[...])
    o_ref[...] = y
    for c in range(ROW_TILES):
        or_ref[pl.ds(c, tm, stride=ROW_TILES), :] = y[:, c * LANES:(c + 1) * LANES]


def _out_proj_ln(y_lru, y_nsa, w_out_b, x, g, b, tm=256):
    S, D = x.shape
    vec = pl.BlockSpec((1, D), lambda i: (0, 0))
    return pl.pallas_call(
        _outproj_kernel,
        grid=(S // tm,),
        in_specs=[pl.BlockSpec((tm, LRU_WIDTH), lambda i: (i, 0)),
                  pl.BlockSpec((tm, NSA_WIDTH), lambda i: (i, 0)),
                  pl.BlockSpec((LRU_WIDTH + NSA_WIDTH, D), lambda i: (0, 0)),
                  pl.BlockSpec((tm, D), lambda i: (i, 0)), vec, vec],
        out_specs=[pl.BlockSpec((tm, D), lambda i: (i, 0)),
                   pl.BlockSpec((tm * ROW_TILES, LANES), lambda i: (i, 0))],
        out_shape=[jax.ShapeDtypeStruct((S, D), F32), jax.ShapeDtypeStruct((S * ROW_TILES, LANES), F32)],
        compiler_params=_cparams(("arbitrary",), 48),
        name="out_proj_ln",
    )(y_lru, y_nsa, w_out_b, x, g.reshape(1, D), b.reshape(1, D))


def _router_kernel(x_ref, w_ref, b_ref, e_ref, g_ref, r_ref, cnt_ref, carry):
    i = pl.program_id(0)
    tm = x_ref.shape[0]

    @pl.when(i == 0)
    def _():
        carry[...] = jnp.zeros(carry.shape, F32)

    logits = jnp.dot(x_ref[...], w_ref[...], preferred_element_type=F32,
                     precision=lax.Precision.HIGHEST) + b_ref[...]
    lane = lax.broadcasted_iota(jnp.int32, (tm, LANES), 1)
    lanef = lane.astype(F32)
    s = logits
    picks = []
    chosen = jnp.zeros((tm, LANES), jnp.bool_)
    for _ in range(TOP_K):
        mx = jnp.max(s, axis=-1, keepdims=True)
        first = jnp.min(jnp.where(s == mx, lanef, float(LANES)), axis=-1, keepdims=True)
        pick = lanef == first
        picks.append((pick, first))
        chosen = chosen | pick
        s = jnp.where(pick, -3e38, s)
    top = jnp.max(logits, axis=-1, keepdims=True)
    ex = jnp.where(chosen, jnp.exp(logits - top), 0.0)
    gate = ex / jnp.sum(ex, axis=-1, keepdims=True)

    onehot = jnp.where(chosen, 1.0, 0.0)
    rr = lax.broadcasted_iota(jnp.int32, (tm, tm), 0)
    rc = lax.broadcasted_iota(jnp.int32, (tm, tm), 1)
    before = (rr > rc).astype(F32).astype(BF16)
    rank = jnp.dot(before, onehot.astype(BF16), preferred_element_type=F32) + carry[0:1, :]
    carry[0:1, :] = carry[0:1, :] + jnp.sum(onehot, axis=0, keepdims=True)
    cnt_ref[...] = carry[...]

    e_out = jnp.zeros((tm, LANES), jnp.int32)
    g_out = jnp.zeros((tm, LANES), F32)
    r_out = jnp.zeros((tm, LANES), F32)
    for k, (pick, first) in enumerate(picks):
        e_out = jnp.where(lane == k, first.astype(jnp.int32), e_out)
        g_out = jnp.where(lane == k, jnp.sum(jnp.where(pick, gate, 0.0), axis=-1, keepdims=True), g_out)
        r_out = jnp.where(lane == k, jnp.sum(jnp.where(pick, rank, 0.0), axis=-1, keepdims=True), r_out)
    e_ref[...] = e_out
    g_ref[...] = g_out
    r_ref[...] = r_out


def _router(x, w_pad, b_pad, tm=512):
    S, D = x.shape
    out = pl.BlockSpec((tm, LANES), lambda i: (i, 0))
    return pl.pallas_call(
        _router_kernel,
        grid=(S // tm,),
        in_specs=[pl.BlockSpec((tm, D), lambda i: (i, 0)),
                  pl.BlockSpec((D, LANES), lambda i: (0, 0)),
                  pl.BlockSpec((1, LANES), lambda i: (0, 0))],
        out_specs=[out, out, out, pl.BlockSpec((8, LANES), lambda i: (0, 0))],
        out_shape=[jax.ShapeDtypeStruct((S, LANES), jnp.int32), jax.ShapeDtypeStruct((S, LANES), F32),
                   jax.ShapeDtypeStruct((S, LANES), F32), jax.ShapeDtypeStruct((8, LANES), F32)],
        scratch_shapes=[pltpu.VMEM((8, LANES), F32)],
        compiler_params=_cparams(("arbitrary",), 32),
        name="moe_router",
    )(x, w_pad, b_pad)


def _expert_kernel(sb_ref, se_ref, sf_ref, sj_ref, so_ref, sv_ref, tok_ref,
                   x_hbm, wgu_ref, bgu_ref, wdn_ref, bdn_ref, o_ref, wgu_b, wdn_b, acc, stage, xrows, sem):
    s = pl.program_id(0)
    n_steps = pl.num_programs(0)
    nrow = sv_ref[s]
    valid = nrow > 0
    f = sf_ref[s]
    j = sj_ref[s]
    SUB = MOE_SUB
    FT2 = wgu_ref.shape[1]
    RT = ROW_TILES
    UNROLL = 8

    def gather_rows(step):
        base = sb_ref[step] * SUB
        slot = sj_ref[step] % 2

        def issue(i, carry):
            for u in range(UNROLL):
                r = i * UNROLL + u
                t = tok_ref[base + r]
                pltpu.make_async_copy(x_hbm.at[t], stage.at[slot, :, r, :], sem.at[slot]).start()
            return carry

        lax.fori_loop(0, sv_ref[step] // UNROLL, issue, 0)

    @pl.when(s == 0)
    def _():
        gather_rows(0)

    nxt = jnp.minimum(s + 1, n_steps - 1)

    @pl.when((s + 1 < n_steps) & (sv_ref[nxt] > 0) & (sf_ref[nxt] == 0))
    def _():
        gather_rows(nxt)

    @pl.when(valid & (j == 0))
    def _():
        wgu_b[...] = wgu_ref[...].astype(BF16)
        wdn_b[...] = wdn_ref[...].astype(BF16)

    @pl.when(jnp.logical_not(valid))
    def _():
        o_ref[...] = jnp.zeros(o_ref.shape, F32)

    def process(m):
        rows = pl.ds(pl.multiple_of(j * SUB, SUB), m)

        @pl.when(f == 0)
        def _():
            slot = j % 2
            filled = stage.at[slot, :, pl.ds(0, m), :]
            pltpu.make_async_copy(filled, filled, sem.at[slot]).wait()
            for c in range(RT):
                xrows[rows, c * LANES:(c + 1) * LANES] = stage[slot, c, 0:m, :].astype(BF16)

        gu = jnp.dot(xrows[rows, :], wgu_b[...], preferred_element_type=F32) + bgu_ref[...]
        gate = jnp.minimum(gu, SWIGLU_LIMIT)
        act = gate * jax.nn.sigmoid(SWIGLU_ALPHA * gate)
        up1 = jnp.clip(gu, -SWIGLU_LIMIT, SWIGLU_LIMIT) + 1.0
        pr = lax.broadcasted_iota(jnp.int32, (2 * LANES, LANES), 0)
        pc = lax.broadcasted_iota(jnp.int32, (2 * LANES, LANES), 1)
        pick_even = (pr == 2 * pc).astype(F32).astype(BF16)
        hs = []
        for c in range(FT2 // (2 * LANES)):
            parts = []
            for k in range(2):
                cs = slice((2 * c + k) * LANES, (2 * c + k + 1) * LANES)
                parts.append((act[:, cs] * pltpu.roll(up1[:, cs], LANES - 1, 1)).astype(BF16))
            pair = jnp.concatenate(parts, axis=1)
            hs.append(jnp.dot(pair, pick_even, preferred_element_type=F32).astype(BF16))
        h = jnp.concatenate(hs, axis=1)
        y = jnp.dot(h, wdn_b[...], preferred_element_type=F32)

        @pl.when(f == 0)
        def _():
            acc[rows, :] = y + bdn_ref[...]

        @pl.when((f > 0) & (f < MOE_NF - 1))
        def _():
            acc[rows, :] = acc[rows, :] + y

        @pl.when(f == MOE_NF - 1)
        def _():
            yfin = acc[rows, :] + y
            for c in range(RT):
                o_ref[pl.ds(c, m, stride=RT), :] = yfin[:, c * LANES:(c + 1) * LANES]
            if m < SUB:
                o_ref[pl.ds(m * RT, (SUB - m) * RT), :] = jnp.zeros(((SUB - m) * RT, LANES), F32)

    for m in range(MOE_ROWQ, SUB + 1, MOE_ROWQ):
        pl.when(nrow == m)(functools.partial(process, m))


def _experts(meta, tok, x_rows, w_gu, b_gu3, w_dn, b_dn3, layer):
    sb, se, sf, sj, so, sv = meta
    NS = sb.shape[0]
    R = tok.shape[0]
    D = D_MODEL
    RT = ROW_TILES

    def wmap(fn):
        return lambda s, sb, se, sf, sj, so, sv, tok: fn(s, se, sf, so)

    grid_spec = pltpu.PrefetchScalarGridSpec(
        num_scalar_prefetch=7,
        grid=(NS,),
        in_specs=[
            pl.BlockSpec(memory_space=pl.ANY),
            pl.BlockSpec((None, None, D, 2 * MOE_FT), wmap(lambda s, se, sf, so: (layer, se[s], 0, sf[s]))),
            pl.BlockSpec((None, None, 1, 2 * MOE_FT), wmap(lambda s, se, sf, so: (layer, se[s], 0, sf[s]))),
            pl.BlockSpec((None, None, MOE_FT, D), wmap(lambda s, se, sf, so: (layer, se[s], sf[s], 0))),
            pl.BlockSpec((None, None, 1, D), wmap(lambda s, se, sf, so: (layer, se[s], 0, 0))),
        ],
        out_specs=pl.BlockSpec((MOE_SUB * RT, LANES), wmap(lambda s, se, sf, so: (so[s], 0))),
        scratch_shapes=[pltpu.VMEM((D, 2 * MOE_FT), BF16), pltpu.VMEM((MOE_FT, D), BF16),
                        pltpu.VMEM((MOE_J * MOE_SUB, D), F32),
                        pltpu.VMEM((2, RT, MOE_SUB, LANES), F32),
                        pltpu.VMEM((MOE_J * MOE_SUB, D), BF16),
                        pltpu.SemaphoreType.DMA((2,))],
    )
    return pl.pallas_call(
        _expert_kernel,
        grid_spec=grid_spec,
        out_shape=jax.ShapeDtypeStruct((R * RT, LANES), F32),
        compiler_params=_cparams(("arbitrary",), 60),
        name="moe_experts",
    )(sb, se, sf, sj, so, sv, tok, x_rows, w_gu, b_gu3, w_dn, b_dn3)


def _plan_kernel(cnt_ref, sb, se, sf, sj, so, sv, sub_start, src_start, n_used):
    n_sub_max = src_start.shape[0]
    n_steps_max = sb.shape[0]

    def expert(e, carry):
        b0, g0, s = carry
        c = cnt_ref[e]
        ns = (c + MOE_SUB - 1) // MOE_SUB
        sub_start[e] = b0

        def sub_tile(lb, _):
            src_start[b0 + lb] = g0 + lb * MOE_SUB
            n_used[b0 + lb] = jnp.minimum(MOE_SUB, c - lb * MOE_SUB)
            return 0

        lax.fori_loop(0, ns, sub_tile, 0)

        def chunk(ci, s):
            cb0 = b0 + ci * MOE_J
            nj = jnp.minimum(MOE_J, ns - ci * MOE_J)

            def ff_tile(f, s):
                def slot(j, s):
                    rem = c - (ci * MOE_J + j) * MOE_SUB
                    sb[s] = cb0 + j
                    se[s] = e
                    sf[s] = f
                    sj[s] = j
                    so[s] = jnp.where(f == MOE_NF - 1, cb0 + j, cb0)
                    sv[s] = jnp.clip((rem + MOE_ROWQ - 1) // MOE_ROWQ * MOE_ROWQ, MOE_ROWQ, MOE_SUB)
                    return s + 1

                return lax.fori_loop(0, nj, slot, s)

            return lax.fori_loop(0, MOE_NF, ff_tile, s)

        s = lax.fori_loop(0, (ns + MOE_J - 1) // MOE_J, chunk, s)
        return b0 + ns, g0 + c, s

    b_end, _, s_end = lax.fori_loop(0, N_EXPERTS, expert, (jnp.int32(0), jnp.int32(0), jnp.int32(0)))

    def unused_sub_tile(b, _):
        src_start[b] = 0
        n_used[b] = 0
        return 0

    lax.fori_loop(b_end, n_sub_max, unused_sub_tile, 0)

    last = jnp.maximum(s_end - 1, 0)

    def unused_step(s, _):
        sb[s] = sb[last]
        se[s] = se[last]
        sf[s] = sf[last]
        sj[s] = sj[last]
        so[s] = jnp.minimum(b_end + (s - s_end), n_sub_max - 1)
        sv[s] = 0
        return 0

    lax.fori_loop(s_end, n_steps_max, unused_step, 0)


def _moe_tables(e_flat, r_flat, cnt, n_sub_max):
    NS = MOE_NF * n_sub_max
    smem = pl.BlockSpec(memory_space=pltpu.SMEM)
    i32 = lambda n: jax.ShapeDtypeStruct((n,), jnp.int32)
    *meta, sub_start, src_start, n_used = pl.pallas_call(
        _plan_kernel,
        in_specs=[smem],
        out_specs=[smem] * 9,
        out_shape=[i32(NS)] * 6 + [i32(N_EXPERTS), i32(n_sub_max), i32(n_sub_max)],
        name="moe_plan",
    )(cnt)
    dest = sub_start[e_flat] * MOE_SUB + r_flat

    n_asg = e_flat.shape[0]
    _, tok_sorted = lax.sort_key_val(dest, jnp.arange(n_asg, dtype=jnp.int32) // TOP_K)
    lane = jnp.arange(MOE_SUB, dtype=jnp.int32)[None, :]
    src = jnp.minimum(src_start[:, None] + lane, n_asg - 1)
    tok = jnp.where(lane < n_used[:, None], tok_sorted[src], 0).reshape(-1)
    return dest, tok, tuple(meta)


def _combine_kernel(dest_ref, y_hbm, g4_ref, x_ref, g_ref, b_ref, o_ref, ob_ref, stage, ytile, ysum, sem):
    i = pl.program_id(0)
    n = pl.num_programs(0)
    tm = x_ref.shape[0]
    RT = ROW_TILES
    n_rows = tm * TOP_K

    def start_row(step, to_slot, r):
        d = dest_ref[step * n_rows + r]
        pltpu.make_async_copy(y_hbm.at[pl.ds(pl.multiple_of(d * RT, RT), RT), :],
                              stage.at[to_slot, pl.ds(pl.multiple_of(r * RT, RT), RT), :],
                              sem.at[to_slot]).start()

    def gather_rows(step, to_slot):
        def issue(r, carry):
            start_row(step, to_slot, r)
            return carry

        lax.fori_loop(0, n_rows, issue, 0, unroll=8)

    @pl.when(i == 0)
    def _():
        gather_rows(0, 0)

    @pl.when(i + 1 < n)
    def _():
        gather_rows(i + 1, (i + 1) % 2)

    slot = i % 2
    pltpu.make_async_copy(y_hbm.at[pl.ds(0, n_rows * RT), :], stage.at[slot], sem.at[slot]).wait()

    def token_sum(t, carry):
        r0 = pl.multiple_of(t * (TOP_K * RT), TOP_K * RT)
        y = g4_ref[TOP_K * t] * stage[slot, pl.ds(r0, RT), :]
        for k in range(1, TOP_K):
            y = y + g4_ref[TOP_K * t + k] * stage[slot, pl.ds(r0 + k * RT, RT), :]
        ytile[pl.ds(pl.multiple_of(t * RT, RT), RT), :] = y
        return carry

    lax.fori_loop(0, tm, token_sum, 0, unroll=4)
    for c in range(RT):
        ysum[:, c * LANES:(c + 1) * LANES] = ytile[pl.ds(c, tm, stride=RT), :]
    out = _layer_norm_rows(DN_ALPHA * x_ref[...] + ysum[...], g_ref[...], b_ref[...])
    o_ref[...] = out
    ob_ref[...] = out.astype(BF16)


def _combine_ln(dest_flat, y_rows, g4, x, g, b, tm=256):
    S, D = x.shape
    vec = pl.BlockSpec((1, D), lambda i, d: (0, 0))
    grid_spec = pltpu.PrefetchScalarGridSpec(
        num_scalar_prefetch=1,
        grid=(S // tm,),
        in_specs=[pl.BlockSpec(memory_space=pl.ANY),
                  pl.BlockSpec((tm * TOP_K,), lambda i, d: (i,), memory_space=pltpu.SMEM),
                  pl.BlockSpec((tm, D), lambda i, d: (i, 0)), vec, vec],
        out_specs=[pl.BlockSpec((tm, D), lambda i, d: (i, 0)), pl.BlockSpec((tm, D), lambda i, d: (i, 0))],
        scratch_shapes=[pltpu.VMEM((2, tm * TOP_K * ROW_TILES, LANES), F32),
                        pltpu.VMEM((tm * ROW_TILES, LANES), F32),
                        pltpu.VMEM((tm, D), F32),
                        pltpu.SemaphoreType.DMA((2,))],
    )
    return pl.pallas_call(
        _combine_kernel,
        grid_spec=grid_spec,
        out_shape=[jax.ShapeDtypeStruct((S, D), F32), jax.ShapeDtypeStruct((S, D), BF16)],
        compiler_params=_cparams(("arbitrary",), 48),
        name="moe_combine_ln",
    )(dest_flat, y_rows, g4, x, g.reshape(1, D), b.reshape(1, D))


def _nsa_group(u_bf, u_win, gate_logits, cmp_pos, cmp_w1, cmp_w2, rel_bias, g_nsa):
    S = u_bf.shape[0]
    G = NSA_KV_HEADS
    kvc = u_bf[:, NSA_WIDTH:NSA_WIDTH + 2 * KV_WIDTH]
    kv_rows = kvc.reshape(S, 2 * G, HEAD_DIM).transpose(1, 0, 2).reshape(2 * G, S // CMP_STRIDE,
                                                                          CMP_STRIDE * HEAD_DIM)
    kcv = _compress(kv_rows, cmp_pos, cmp_w1, cmp_w2)
    o_c, sel = _cmp_attention(rel_bias, u_bf, kcv)
    kv_pad = jnp.pad(jnp.concatenate([u_bf[:, NSA_WIDTH + 2 * KV_WIDTH:], u_win], axis=1), ((KV_FRONT, 0), (0, 0)))
    return _sel_attention(rel_bias, u_bf, kv_pad, sel, o_c, gate_logits, g_nsa)


def _moe(x, x_rows, w_router, b_router, w_gu, b_gu, w_dn, b_dn, ln_g, ln_b, layer):
    S, D = x.shape
    E = N_EXPERTS
    w_pad = jnp.pad(w_router, ((0, 0), (0, LANES - E)))
    b_pad = jnp.pad(b_router, (0, LANES - E), constant_values=NEG_INF).reshape(1, LANES)
    e_out, g_out, r_out, cnt_out = _router(x, w_pad, b_pad)
    e_flat = e_out[:, :TOP_K].reshape(-1)
    r_flat = r_out[:, :TOP_K].reshape(-1).astype(jnp.int32)
    cnt = cnt_out[0, :E].astype(jnp.int32)
    n_sub_max = (S * TOP_K) // MOE_SUB + E
    flat, tok, meta = _moe_tables(e_flat, r_flat, cnt, n_sub_max)
    y_rows = _experts(meta, tok, x_rows.reshape(S, ROW_TILES, LANES), w_gu, b_gu.reshape(DEPTH, E, 1, 2 * D_FF), w_dn,
                      b_dn.reshape(DEPTH, E, 1, D), layer)
    return _combine_ln(flat, y_rows, g_out[:, :TOP_K].reshape(-1), x, ln_g, ln_b)


def kernel(x, w_in, conv_w, conv_b, lru_wa, lru_ba, lru_wi, lru_bi, lru_lambda, cmp_pos, cmp_w1, cmp_w2,
           rel_bias, g_lru, g_nsa, w_out, ln1_g, ln1_b, w_router, b_router, w_gate_up, b_gate_up, w_down,
           b_down, ln2_g, ln2_b):
    B, S, D = x.shape
    assert B == 1
    x = x.reshape(S, D)
    xb = x.astype(BF16)
    n_gate = 3 * NSA_HEADS
    for l in range(DEPTH):
        u_f32 = _project(xb, w_in, l, 0, 2, 1024, F32)
        u_bf = _project(xb, w_in, l, 2, 2, 1024, BF16)
        u_win = _project(xb, w_in, l, 8, 1, 512, BF16)
        w_gate = jnp.pad(w_in[l, :, MAIN_COLS:], ((0, 0), (0, LANES - n_gate)))[None]
        gate_logits = _project(xb, w_gate, 0, 0, 1, LANES, F32)
        y_lru = _lru_group(u_f32, conv_w[l], conv_b[l], lru_wa[l], lru_ba[l], lru_wi[l], lru_bi[l],
                           lru_lambda[l], g_lru[l])
        y_nsa = _nsa_group(u_bf, u_win, gate_logits, cmp_pos[l], cmp_w1[l], cmp_w2[l], rel_bias, g_nsa[l])
        x, x_rows = _out_proj_ln(y_lru, y_nsa, w_out[l].astype(BF16), x, ln1_g[l], ln1_b[l])
        x, xb = _moe(x, x_rows, w_router[l], b_router[l], w_gate_up, b_gate_up, w_down, b_down,
                     ln2_g[l], ln2_b[l], l)
    return x.reshape(B, S, D)
```
